```python
import math
import jax, jax.numpy as jnp
from jax import lax
import numpy as np

D_MODEL = 1024
BATCH = 4
SEQ = 8192
DEPTH = 4
DEC_BATCH = 32
DEC_SEQ = 2048
PAST_LEN = 128

ATTN_WIDTH = D_MODEL // 2
RWKV_WIDTH = D_MODEL - ATTN_WIDTH
ATTN_HEADS = 4
ATTN_QK_DIM = ATTN_WIDTH // (2 * ATTN_HEADS)
ATTN_V_DIM = 2 * ATTN_QK_DIM
RWKV_HEAD = 64
RWKV_HEADS = RWKV_WIDTH // RWKV_HEAD
DECAY_LORA = 32
AAA_LORA = 32
GATE_LORA = 96
ATTN_COLS = 3 * ATTN_WIDTH
RWKV_COLS = 3 * RWKV_WIDTH + 2 * DECAY_LORA + 2 * AAA_LORA + GATE_LORA
IN_COLS = ATTN_COLS + RWKV_COLS
RWKV_SPLITS = tuple(int(s) for s in np.cumsum([RWKV_WIDTH] * 3 + [DECAY_LORA] * 2 + [AAA_LORA] * 2 + [GATE_LORA])[:-1])
N_EXPERTS = 16
EXPERT_FF = 2 * D_MODEL
CAPACITY_FACTOR = 2
Q_BLOCK = 128
NORM_EPS = 1e-6
GN_EPS = 64e-5

kernel_name = 'hybrid_diffattn_rwkv7_ec_encoder'


def rmsnorm(x, g):
    xf = x.astype(jnp.float32)
    y = xf * lax.rsqrt(jnp.mean(xf * xf, axis=-1, keepdims=True) + NORM_EPS) * g.astype(jnp.float32)
    return y.astype(x.dtype)


def diff_attention(q, k, v, lam, subln_g, lambda_init):
    B, T = q.shape[0], q.shape[1]
    nb = T // Q_BLOCK
    slopes = 2.0 ** (-8.0 * jnp.arange(1, ATTN_HEADS + 1, dtype=jnp.float32) / ATTN_HEADS)
    kpos = jnp.arange(T, dtype=jnp.float32)
    scale = ATTN_QK_DIM ** -0.5
    qb = q.reshape(B, nb, Q_BLOCK, ATTN_HEADS, 2, ATTN_QK_DIM).transpose(1, 0, 2, 3, 4, 5)

    def block(args):
        q_blk, i = args
        s = jnp.einsum('bqhmd,bkhmd->bhmqk', q_blk, k).astype(jnp.float32) * scale
        qpos = (i * Q_BLOCK + jnp.arange(Q_BLOCK)).astype(jnp.float32)
        bias = -slopes[:, None, None] * jnp.abs(qpos[:, None] - kpos[None, :])
        p = jax.nn.softmax(s + bias[None, :, None], axis=-1)
        pd = (p[:, :, 0] - lam * p[:, :, 1]).astype(v.dtype)
        return jnp.einsum('bhqk,bkhe->bqhe', pd, v)

    o = lax.map(block, (qb, jnp.arange(nb)))
    o = o.transpose(1, 0, 2, 3, 4).reshape(B, T, ATTN_HEADS, ATTN_V_DIM)
    o = rmsnorm(o, subln_g) * (1.0 - lambda_init)
    return o.reshape(B, T, ATTN_WIDTH)


def wkv7_scan(r, w, k, v, a, b, reverse):
    B, T, H, N = r.shape
    tm = lambda z: jnp.swapaxes(z.astype(jnp.float32), 0, 1)

    def step(S, inp):
        r_t, w_t, k_t, v_t, a_t, b_t = inp
        Sa = jnp.einsum('bhij,bhj->bhi', S, a_t)
        S = S * w_t[:, :, None, :] + Sa[..., None] * b_t[:, :, None, :] + v_t[..., None] * k_t[:, :, None, :]
        y = jnp.einsum('bhij,bhj->bhi', S, r_t)
        return S, y

    S0 = jnp.zeros((B, H, N, N), jnp.float32)
    _, y = lax.scan(step, S0, (tm(r), tm(w), tm(k), tm(v), tm(a), tm(b)), reverse=reverse)
    return jnp.swapaxes(y, 0, 1)


def rwkv7_mix(z, mu, w0, w2, a0, a2, g2, k_k, k_a, r_k, lnx_w, lnx_b):
    B, T, _ = z.shape
    prev = jnp.pad(z, ((0, 0), (1, 0), (0, 0)))[:, :-1]
    nxt = jnp.pad(z, ((0, 0), (0, 1), (0, 0)))[:, 1:]
    z = z + mu[0] * (prev - z) + mu[1] * (nxt - z)
    r, k, v, xw_f, xw_b, xa_f, xa_b, xg = jnp.split(z, RWKV_SPLITS, axis=-1)
    hd = lambda t: t.reshape(B, T, RWKV_HEADS, RWKV_HEAD)
    kk = hd(k * k_k).astype(jnp.float32)
    kk = kk / jnp.maximum(jnp.sqrt(jnp.sum(kk * kk, axis=-1, keepdims=True)), 1e-12)
    ys = []
    for d, (xw, xa, rev) in enumerate(((xw_f, xa_f, False), (xw_b, xa_b, True))):
        wlog = -jax.nn.softplus(-(w0[d] + jnp.tanh(xw) @ w2[d])) - 0.5
        decay = jnp.exp(-jnp.exp(wlog.astype(jnp.float32)))
        iclr = jax.nn.sigmoid(a0[d] + xa @ a2[d])
        kd = k * (1.0 + (iclr - 1.0) * k_a)
        ys.append(wkv7_scan(hd(r), hd(decay), hd(kd), hd(v), -kk, kk * hd(iclr).astype(jnp.float32), rev))
    y = ys[0] + ys[1]
    mean = jnp.mean(y, axis=-1, keepdims=True)
    var = jnp.mean((y - mean) ** 2, axis=-1, keepdims=True)
    y = ((y - mean) * lax.rsqrt(var + GN_EPS)).reshape(B, T, RWKV_WIDTH) * lnx_w.astype(jnp.float32) + lnx_b.astype(jnp.float32)
    bonus = (jnp.sum(hd(r) * hd(k) * r_k, axis=-1, keepdims=True) * hd(v)).reshape(B, T, RWKV_WIDTH)
    g = jax.nn.sigmoid(xg) @ g2
    return ((y + bonus.astype(jnp.float32)) * g.astype(jnp.float32)).astype(z.dtype)


def expert_choice_ffn(h, w_router, w_gate, w_up, w_down):
    shp = h.shape
    x = h.reshape(-1, D_MODEL)
    n = x.shape[0]
    cap = CAPACITY_FACTOR * n // N_EXPERTS
    aff = jax.nn.softmax((x @ w_router).astype(jnp.float32), axis=-1)
    gates, idx = lax.top_k(aff.T, cap)
    xe = x[idx]

    def expert(args):
        xs, wg, wu, wd = args
        return (jax.nn.silu(xs @ wg) * (xs @ wu)) @ wd

    ye = lax.map(expert, (xe, w_gate, w_up, w_down))
    contrib = (gates[..., None].astype(ye.dtype) * ye).reshape(-1, D_MODEL)
    y = jnp.zeros_like(x).at[idx.reshape(-1)].add(contrib)
    return y.reshape(shp)


def trunk(x, norm1_g, w_in, lambda_qk, subln_g, shift_mu, w0, w2, a0, a2, g2, k_k, k_a, r_k,
          lnx_w, lnx_b, w_out, norm2_g, w_router, w_gate, w_up, w_down, final_g):
    B, T, _ = x.shape
    for l in range(DEPTH):
        lambda_init = 0.8 - 0.6 * math.exp(-0.3 * l)
        h = rmsnorm(x, norm1_g[l])
        zin = h @ w_in[l]
        za, zr = zin[..., :ATTN_COLS], zin[..., ATTN_COLS:]
        q, k, v = jnp.split(za, 3, axis=-1)
        q = q.reshape(B, T, ATTN_HEADS, 2, ATTN_QK_DIM)
        k = k.reshape(B, T, ATTN_HEADS, 2, ATTN_QK_DIM)
        v = v.reshape(B, T, ATTN_HEADS, ATTN_V_DIM)
        lq = lambda_qk[l].astype(jnp.float32)
        lam = jnp.exp(jnp.sum(lq[0] * lq[1])) - jnp.exp(jnp.sum(lq[2] * lq[3])) + lambda_init
        oa = diff_attention(q, k, v, lam, subln_g[l], lambda_init)
        orw = rwkv7_mix(zr, shift_mu[l], w0[l], w2[l], a0[l], a2[l], g2[l], k_k[l], k_a[l], r_k[l], lnx_w[l], lnx_b[l])
        x = x + jnp.concatenate([oa, orw], axis=-1) @ w_out[l]
        x = x + expert_choice_ffn(rmsnorm(x, norm2_g[l]), w_router[l], w_gate[l], w_up[l], w_down[l])
    return rmsnorm(x, final_g)


def setup_inputs(seed: int = 0) -> dict:
    key = jax.random.key(seed)
    ks = jax.random.split(key, 32)
    nrm = lambda i, shape, s: jax.random.normal(ks[i], shape, jnp.float32) * s
    L = DEPTH
    return {
        'x_prompt': nrm(0, (BATCH, SEQ, D_MODEL), 1.0),
        'x_sample': nrm(1, (DEC_BATCH, DEC_SEQ, D_MODEL), 1.0),
        'norm1_g': 1.0 + nrm(2, (L, D_MODEL), 0.02),
        'w_in': nrm(3, (L, D_MODEL, IN_COLS), D_MODEL ** -0.5),
        'lambda_qk': nrm(4, (L, 4, ATTN_QK_DIM), 0.1),
        'subln_g': 1.0 + nrm(5, (L, ATTN_V_DIM), 0.02),
        'shift_mu': jax.random.uniform(ks[6], (L, 2, RWKV_COLS), jnp.float32, 0.0, 0.5),
        'w0': jax.random.uniform(ks[7], (L, 2, RWKV_WIDTH), jnp.float32, -6.0, -0.5),
        'w2': nrm(8, (L, 2, DECAY_LORA, RWKV_WIDTH), 0.1 * DECAY_LORA ** -0.5),
        'a0': nrm(9, (L, 2, RWKV_WIDTH), 0.1),
        'a2': nrm(10, (L, 2, AAA_LORA, RWKV_WIDTH), 0.1 * AAA_LORA ** -0.5),
        'g2': nrm(11, (L, GATE_LORA, RWKV_WIDTH), GATE_LORA ** -0.5),
        'k_k': 0.85 + nrm(12, (L, RWKV_WIDTH), 0.02),
        'k_a': 1.0 + nrm(13, (L, RWKV_WIDTH), 0.02),
        'r_k': nrm(14, (L, RWKV_HEADS, RWKV_HEAD), 0.1),
        'lnx_w': 1.0 + nrm(15, (L, RWKV_WIDTH), 0.02),
        'lnx_b': nrm(16, (L, RWKV_WIDTH), 0.02),
        'w_out': nrm(17, (L, D_MODEL, D_MODEL), D_MODEL ** -0.5),
        'norm2_g': 1.0 + nrm(18, (L, D_MODEL), 0.02),
        'w_router': nrm(19, (L, D_MODEL, N_EXPERTS), D_MODEL ** -0.5),
        'w_gate': nrm(20, (L, N_EXPERTS, D_MODEL, EXPERT_FF), D_MODEL ** -0.5),
        'w_up': nrm(21, (L, N_EXPERTS, D_MODEL, EXPERT_FF), D_MODEL ** -0.5),
        'w_down': nrm(22, (L, N_EXPERTS, EXPERT_FF, D_MODEL), EXPERT_FF ** -0.5),
        'final_g': 1.0 + nrm(23, (D_MODEL,), 0.02),
    }


def reference(x_prompt, x_sample, norm1_g, w_in, lambda_qk, subln_g, shift_mu, w0, w2, a0, a2, g2, k_k, k_a, r_k,
              lnx_w, lnx_b, w_out, norm2_g, w_router, w_gate, w_up, w_down, final_g):
    weights = (norm1_g, w_in, lambda_qk, subln_g, shift_mu, w0, w2, a0, a2, g2, k_k, k_a, r_k,
               lnx_w, lnx_b, w_out, norm2_g, w_router, w_gate, w_up, w_down, final_g)
    y_prompt = trunk(x_prompt, *weights)
    y_sample = trunk(x_sample, *weights)
    return (y_prompt, y_sample)
```

```python
import functools
import math

import jax
import jax.numpy as jnp
from jax import lax
from jax.experimental import pallas as pl
from jax.experimental.pallas import tpu as pltpu

F32 = jnp.float32
BF16 = jnp.bfloat16
HIGHEST = lax.Precision.HIGHEST

D_MODEL = 1024
DEPTH = 4
ATTN_WIDTH = 512
ATTN_HEADS = 4
ATTN_QK_DIM = 64
ATTN_V_DIM = 128
RWKV_WIDTH = 512
RWKV_HEAD = 64
RWKV_HEADS = 8
DECAY_LORA = 32
AAA_LORA = 32
GATE_LORA = 96
LORA_PAD = 128
ATTN_COLS = 3 * ATTN_WIDTH
RWKV_MAIN = 3 * RWKV_WIDTH
RWKV_COLS_PAD = RWKV_MAIN + 5 * LORA_PAD
N_EXPERTS = 16
EXPERT_FF = 2 * D_MODEL
CAPACITY_FACTOR = 2
NORM_EPS = 1e-6
GN_EPS = 64e-5

CHUNK = 64
GROUP = 4
GROUP_W = GROUP * RWKV_HEAD
N_LEVELS = CHUNK.bit_length() - 1
VMEM_LIMIT = 56 * 1024 * 1024


def _cparams(sem):
    return pltpu.CompilerParams(dimension_semantics=sem, vmem_limit_bytes=VMEM_LIMIT)


def _dot(a, b):
    return jnp.dot(a, b, preferred_element_type=F32)


def _dot_nt(a, b):
    return lax.dot_general(a, b, (((1,), (1,)), ((), ())), preferred_element_type=F32)


def _dot_tn(a, b):
    return lax.dot_general(a, b, (((0,), (0,)), ((), ())), preferred_element_type=F32)


def _dot_f32(a, b):
    return jnp.dot(a, b, preferred_element_type=F32, precision=HIGHEST)


def _in_proj_kernel(x_ref, g_ref, w_ref, za_ref, zr_ref):
    x = x_ref[...]
    ms = jnp.mean(x * x, axis=-1, keepdims=True)
    h = (x * lax.rsqrt(ms + NORM_EPS) * g_ref[...]).astype(BF16)
    za_ref[...] = _dot(h, w_ref[:, :ATTN_COLS]).astype(BF16)
    zr_ref[...] = _dot(h, w_ref[:, ATTN_COLS:])


def in_proj(x, g, w, tm):
    n = x.shape[0]
    cols = w.shape[1]
    return pl.pallas_call(
        _in_proj_kernel,
        grid=(n // tm,),
        in_specs=[
            pl.BlockSpec((tm, D_MODEL), lambda i: (i, 0)),
            pl.BlockSpec((1, D_MODEL), lambda i: (0, 0)),
            pl.BlockSpec((D_MODEL, cols), lambda i: (0, 0)),
        ],
        out_specs=[
            pl.BlockSpec((tm, ATTN_COLS), lambda i: (i, 0)),
            pl.BlockSpec((tm, RWKV_COLS_PAD), lambda i: (i, 0)),
        ],
        out_shape=[
            jax.ShapeDtypeStruct((n, ATTN_COLS), BF16),
            jax.ShapeDtypeStruct((n, RWKV_COLS_PAD), F32),
        ],
        compiler_params=_cparams(("parallel",)),
        name="in_proj",
    )(x, g, w)


def _attn_kernel(slopes_ref, lq_ref, q_ref, k_ref, v_ref, g_ref, o_ref, m_ref, l_ref, acc_ref,
                 *, seq, tq, tk, lambda_init):
    h = pl.program_id(1)
    i = pl.program_id(2)
    slope = slopes_ref[h]
    q = q_ref[...]
    lane = lax.broadcasted_iota(jnp.int32, q.shape, 1)
    zero = jnp.zeros_like(q)
    q2 = jnp.concatenate([jnp.where(lane < ATTN_QK_DIM, q, zero),
                          jnp.where(lane >= ATTN_QK_DIM, q, zero)], axis=0)
    rel = (lax.broadcasted_iota(jnp.int32, (tq, tk), 0)
           - lax.broadcasted_iota(jnp.int32, (tq, tk), 1)).astype(F32)
    m_ref[...] = jnp.full(m_ref.shape, -jnp.inf, F32)
    l_ref[...] = jnp.zeros(l_ref.shape, F32)
    acc_ref[...] = jnp.zeros(acc_ref.shape, F32)

    def body(j, carry):
        start = pl.multiple_of(j * tk, tk)
        k = k_ref[pl.ds(start, tk), :]
        v = v_ref[pl.ds(start, tk), :]
        s = _dot_nt(q2, k)
        off = (i * tq - j * tk).astype(F32)
        bias = -slope * jnp.abs(rel + off)
        s = (s.reshape(2, tq, tk) + bias[None]).reshape(2 * tq, tk)
        m_old = m_ref[...]
        m_new = jnp.maximum(m_old, jnp.max(s, axis=-1, keepdims=True))
        alpha = jnp.exp(m_old - m_new)
        p = jnp.exp(s - m_new)
        l_ref[...] = alpha * l_ref[...] + jnp.sum(p, axis=-1, keepdims=True)
        acc_ref[...] = alpha * acc_ref[...] + _dot(p.astype(BF16), v)
        m_ref[...] = m_new
        return carry

    lax.fori_loop(0, seq // tk, body, 0)

    lq = lq_ref[...]
    lam = (jnp.exp(jnp.sum(lq[0:1] * lq[1:2], axis=-1, keepdims=True))
           - jnp.exp(jnp.sum(lq[2:3] * lq[3:4], axis=-1, keepdims=True)) + lambda_init)
    o_all = acc_ref[...] / l_ref[...]
    o = o_all[:tq] - lam * o_all[tq:]
    ms = jnp.mean(o * o, axis=-1, keepdims=True)
    o = o * lax.rsqrt(ms + NORM_EPS) * g_ref[...] * (1.0 - lambda_init)
    o_ref[...] = o.astype(BF16)


def diff_attention(za, slopes, lq, subln_g, batch, seq, lambda_init, tq, tk):
    n = batch * seq
    nq = seq // tq
    kern = functools.partial(_attn_kernel, seq=seq, tq=tq, tk=tk, lambda_init=lambda_init)
    return pl.pallas_call(
        kern,
        grid=(batch, ATTN_HEADS, nq),
        in_specs=[
            pl.BlockSpec(memory_space=pltpu.SMEM),
            pl.BlockSpec((4, ATTN_QK_DIM), lambda b, h, i: (0, 0)),
            pl.BlockSpec((tq, ATTN_V_DIM), lambda b, h, i: (b * nq + i, h)),
            pl.BlockSpec((seq, ATTN_V_DIM), lambda b, h, i: (b, ATTN_HEADS + h)),
            pl.BlockSpec((seq, ATTN_V_DIM), lambda b, h, i: (b, 2 * ATTN_HEADS + h)),
            pl.BlockSpec((1, ATTN_V_DIM), lambda b, h, i: (0, 0)),
        ],
        out_specs=pl.BlockSpec((tq, ATTN_V_DIM), lambda b, h, i: (b * nq + i, h)),
        out_shape=jax.ShapeDtypeStruct((n, ATTN_WIDTH), BF16),
        scratch_shapes=[
            pltpu.VMEM((2 * tq, 1), F32),
            pltpu.VMEM((2 * tq, 1), F32),
            pltpu.VMEM((2 * tq, ATTN_V_DIM), F32),
        ],
        compiler_params=_cparams(("parallel", "parallel", "parallel")),
        name="diff_attention",
    )(slopes, lq, za, za, za, subln_g)


def _softplus(x):
    return jnp.maximum(x, 0.0) + jnp.log(1.0 + jnp.exp(-jnp.abs(x)))


def _sigmoid(x):
    return 1.0 / (1.0 + jnp.exp(-x))


def _prep_kernel(z_ref, zp_ref, zn_ref, mu_ref, w0_ref, w2_ref, a0_ref, a2_ref, g2_ref,
                 kk_ref, ka_ref, rk_ref, seg_ref,
                 r_ref, v_ref, a_ref, lw_ref, kd_ref, bb_ref, bonus_ref, gate_ref, *, seq, tm):
    i = pl.program_id(0)
    z = z_ref[...]
    row = lax.broadcasted_iota(jnp.int32, z.shape, 0)
    first = (i * tm) % seq == 0
    last = ((i + 1) * tm) % seq == 0
    zp = jnp.where(first, 0.0, zp_ref[7:8, :])
    zn = jnp.where(last, 0.0, zn_ref[0:1, :])
    prev = jnp.where(row == 0, zp, pltpu.roll(z, 1, 0))
    nxt = jnp.where(row == tm - 1, zn, pltpu.roll(z, tm - 1, 0))
    z = z + mu_ref[0:1, :] * (prev - z) + mu_ref[1:2, :] * (nxt - z)

    w = RWKV_WIDTH
    r = z[:, 0:w]
    k = z[:, w:2 * w]
    v = z[:, 2 * w:3 * w]
    seg = seg_ref[...]
    kk = k * kk_ref[...]
    ss = _dot_f32(kk * kk, seg)
    kk = kk / jnp.maximum(jnp.sqrt(ss), 1e-12)
    r_ref[...] = r
    v_ref[...] = v
    a_ref[...] = -kk
    for d in range(2):
        xw = z[:, RWKV_MAIN + d * LORA_PAD:RWKV_MAIN + (d + 1) * LORA_PAD]
        xa = z[:, RWKV_MAIN + (2 + d) * LORA_PAD:RWKV_MAIN + (3 + d) * LORA_PAD]
        wlog = -_softplus(-(w0_ref[d:d + 1, :] + _dot(jnp.tanh(xw).astype(BF16), w2_ref[d]))) - 0.5
        lw_ref[d] = -jnp.exp(wlog)
        iclr = _sigmoid(a0_ref[d:d + 1, :] + _dot(xa.astype(BF16), a2_ref[d]))
        kd_ref[d] = k * (1.0 + (iclr - 1.0) * ka_ref[...])
        bb_ref[d] = kk * iclr
    xg = z[:, RWKV_MAIN + 4 * LORA_PAD:RWKV_MAIN + 5 * LORA_PAD]
    gate_ref[...] = _dot(_sigmoid(xg).astype(BF16), g2_ref[...])
    bonus_ref[...] = _dot_f32(r * k * rk_ref[...], seg) * v


def rwkv_prep(zr, mu, w0, w2, a0, a2, g2, k_k, k_a, r_k, seg, seq, tm):
    n, cols = zr.shape
    nb8 = n // 8
    tb8 = tm // 8
    full = lambda shape: pl.BlockSpec(shape, lambda i: (0,) * len(shape))
    tok = pl.BlockSpec((tm, RWKV_WIDTH), lambda i: (i, 0))
    tok2 = pl.BlockSpec((2, tm, RWKV_WIDTH), lambda i: (0, i, 0))
    sds = jax.ShapeDtypeStruct((n, RWKV_WIDTH), F32)
    sds2 = jax.ShapeDtypeStruct((2, n, RWKV_WIDTH), F32)
    return pl.pallas_call(
        functools.partial(_prep_kernel, seq=seq, tm=tm),
        grid=(n // tm,),
        in_specs=[
            pl.BlockSpec((tm, cols), lambda i: (i, 0)),
            pl.BlockSpec((8, cols), lambda i: (jnp.maximum(i * tb8 - 1, 0), 0)),
            pl.BlockSpec((8, cols), lambda i: (jnp.minimum((i + 1) * tb8, nb8 - 1), 0)),
            full((2, cols)), full((2, RWKV_WIDTH)), full((2, LORA_PAD, RWKV_WIDTH)),
            full((2, RWKV_WIDTH)), full((2, LORA_PAD, RWKV_WIDTH)), full((LORA_PAD, RWKV_WIDTH)),
            full((1, RWKV_WIDTH)), full((1, RWKV_WIDTH)), full((1, RWKV_WIDTH)),
            full((RWKV_WIDTH, RWKV_WIDTH)),
        ],
        out_specs=[tok, tok, tok, tok2, tok2, tok2, tok, tok],
        out_shape=[sds, sds, sds, sds2, sds2, sds2, sds, sds],
        compiler_params=_cparams(("parallel",)),
        name="rwkv_prep",
    )(zr, zr, zr, mu, w0, w2, a0, a2, g2, k_k, k_a, r_k, seg)


def _scan_kernel(r_ref, v_ref, a_ref, lw_ref, k_ref, b_ref, tri_ref, strict_ref, incl_ref, lvl_ref, eye_ref, bd_ref,
                 y_ref, h_ref, *, tb):
    d = pl.program_id(1)
    i = pl.program_id(3)
    nc = tb // CHUNK

    @pl.when(i == 0)
    def _():
        h_ref[...] = jnp.zeros(h_ref.shape, F32)

    tri = tri_ref[...]
    strict = strict_ref[...] > 0.5
    incl = incl_ref[...] > 0.5
    eye = eye_ref[...]
    bdm = bd_ref[...] > 0.5
    ones8 = jnp.full((8, CHUNK), 0.125, F32)
    ones8w = jnp.ones((8, GROUP_W), F32)

    def bd(x):
        return jnp.where(bdm, jnp.concatenate([x] * GROUP, axis=0), 0.0).astype(BF16)

    def chunk(cc, carry):
        ci = cc + d * (nc - 1 - 2 * cc)
        st = pl.multiple_of(ci * CHUNK, CHUNK)
        sl = pl.ds(st, CHUNK)
        lw = lw_ref[sl, :]
        r = r_ref[sl, :]
        v = v_ref[sl, :]
        a = a_ref[sl, :]
        k = k_ref[sl, :]
        b = b_ref[sl, :]
        lcum = _dot_f32(tri, lw)
        ltot8 = _dot_f32(ones8, lw) * 8.0
        ltot = ltot8[0:1]
        at = a * jnp.exp(lcum - lw)
        rt = r * jnp.exp(lcum)
        einv = jnp.exp(-lcum)
        bt = b * einv
        kt = k * einv
        eend = jnp.exp(ltot - lcum)
        bw = b * eend
        kw = k * eend

        x = jnp.concatenate([at, rt], axis=0).astype(BF16)
        ybk = jnp.concatenate([bd(bt), bd(kt)], axis=0)
        amat = _dot_nt(x, ybk)
        aab = jnp.where(strict, amat[:CHUNK, :GROUP_W], 0.0)
        aak = jnp.where(strict, amat[:CHUNK, GROUP_W:], 0.0)
        arb = jnp.where(incl, amat[CHUNK:, :GROUP_W], 0.0)
        ark = jnp.where(incl, amat[CHUNK:, GROUP_W:], 0.0)

        p = eye + jnp.where(lvl_ref[0] > 0.5, aab, 0.0)
        for lvl in range(1, N_LEVELS):
            e = jnp.where(lvl_ref[lvl] > 0.5, aab, 0.0)
            g = _dot(p.astype(BF16), bd(e))
            p = p + _dot(g.astype(BF16), bd(p))

        h = h_ref[...]
        hx = _dot(x, h.astype(BF16))
        vbd = bd(v)
        av = _dot(aak.astype(BF16), vbd)
        u = _dot(p.astype(BF16), bd(hx[:CHUNK] + av))
        y = hx[CHUNK:] + _dot(jnp.concatenate([arb, ark], axis=1).astype(BF16),
                              jnp.concatenate([bd(u), vbd], axis=0))
        y_ref[sl, :] = y
        upd = _dot_tn(jnp.concatenate([bw, kw], axis=0).astype(BF16),
                      jnp.concatenate([u, v], axis=0).astype(BF16))
        wcol = lax.dot_general(jnp.exp(ltot8) * 0.125, ones8w, (((0,), (0,)), ((), ())),
                               preferred_element_type=F32, precision=HIGHEST)
        h_ref[...] = wcol * h + jnp.where(bdm, upd, 0.0)
        return carry

    lax.fori_loop(0, nc, chunk, 0)


def rwkv_scan(r, v, a, lw, kd, bb, consts, batch, seq, tb):
    n = batch * seq
    nblk = seq // tb
    ngrp = RWKV_WIDTH // GROUP_W
    tri, strict, incl, lvl, eye, bdm = consts

    def tmap(b, d, g, i):
        return (b * nblk + i + d * (nblk - 1 - 2 * i), g)

    def tmap_d(b, d, g, i):
        return (d, b * nblk + i + d * (nblk - 1 - 2 * i), g)

    tok = pl.BlockSpec((tb, GROUP_W), tmap)
    tok_d = pl.BlockSpec((None, tb, GROUP_W), tmap_d)
    per_d = lambda shape: pl.BlockSpec((None,) + shape, lambda b, d, g, i: (d, 0, 0))
    full = lambda shape: pl.BlockSpec(shape, lambda b, d, g, i: (0, 0))
    return pl.pallas_call(
        functools.partial(_scan_kernel, tb=tb),
        grid=(batch, 2, ngrp, nblk),
        in_specs=[tok, tok, tok, tok_d, tok_d, tok_d,
                  per_d((CHUNK, CHUNK)), per_d((CHUNK, GROUP * CHUNK)), per_d((CHUNK, GROUP * CHUNK)),
                  pl.BlockSpec((None, N_LEVELS, CHUNK, GROUP * CHUNK), lambda b, d, g, i: (d, 0, 0, 0)),
                  full((CHUNK, GROUP * CHUNK)), full((GROUP_W, GROUP_W))],
        out_specs=tok_d,
        out_shape=jax.ShapeDtypeStruct((2, n, RWKV_WIDTH), F32),
        scratch_shapes=[pltpu.VMEM((GROUP_W, GROUP_W), F32)],
        compiler_params=_cparams(("parallel", "parallel", "parallel", "arbitrary")),
        name="rwkv_scan",
    )(r, v, a, lw, kd, bb, tri, strict, incl, lvl, eye, bdm)


def scan_constants():
    t = jnp.arange(CHUNK)
    before_eq = (t[None, :] <= t[:, None])
    tri = jnp.stack([before_eq, before_eq.T]).astype(F32)
    incl = jnp.tile(tri, (1, 1, GROUP))
    eye = jnp.tile(jnp.eye(CHUNK, dtype=F32), (1, GROUP))
    strict = incl - eye[None]
    levels = []
    for lv in range(N_LEVELS):
        half = t // (1 << lv)
        fwd = ((half[:, None] // 2 == half[None, :] // 2) & (half[:, None] % 2 == 1) & (half[None, :] % 2 == 0))
        levels.append(jnp.stack([fwd, fwd.T]))
    lvl = jnp.tile(jnp.stack(levels, axis=1).astype(F32), (1, 1, 1, GROUP))
    blk = jnp.arange(GROUP_W) // RWKV_HEAD
    bdm = (blk[:, None] == blk[None, :]).astype(F32)
    return tri, strict, incl, lvl, eye, bdm


def _post_kernel(yf_ref, yb_ref, bonus_ref, gate_ref, oa_ref, x_ref, lw_ref, lb_ref, seg_ref,
                 wo_ref, g2_ref, wr_ref, xo_ref, h_ref, aff_ref):
    y = yf_ref[...] + yb_ref[...]
    seg = seg_ref[...]
    mean = _dot_f32(y, seg) * (1.0 / RWKV_HEAD)
    yc = y - mean
    var = _dot_f32(yc * yc, seg) * (1.0 / RWKV_HEAD)
    yn = yc * lax.rsqrt(var + GN_EPS) * lw_ref[...] + lb_ref[...]
    orw = ((yn + bonus_ref[...]) * gate_ref[...]).astype(BF16)
    x = (x_ref[...] + _dot(oa_ref[...], wo_ref[:ATTN_WIDTH, :]) + _dot(orw, wo_ref[ATTN_WIDTH:, :]))
    xo_ref[...] = x
    ms = jnp.mean(x * x, axis=-1, keepdims=True)
    h = x * lax.rsqrt(ms + NORM_EPS) * g2_ref[...]
    h_ref[...] = h.astype(BF16)
    logits = lax.dot_general(wr_ref[...], h, (((1,), (1,)), ((), ())),
                             preferred_element_type=F32, precision=HIGHEST)
    logits = logits - jnp.max(logits, axis=0, keepdims=True)
    e = jnp.exp(logits)
    aff_ref[...] = e / jnp.sum(e, axis=0, keepdims=True)


def post_mix(y2, bonus, gate, oa, x, lnx_w, lnx_b, seg, w_out, norm2_g, w_router_t, tm):
    n = x.shape[0]
    full = lambda shape: pl.BlockSpec(shape, lambda i: (0,) * len(shape))
    tok = lambda w: pl.BlockSpec((tm, w), lambda i: (i, 0))
    return pl.pallas_call(
        _post_kernel,
        grid=(n // tm,),
        in_specs=[
            pl.BlockSpec((None, tm, RWKV_WIDTH), lambda i: (0, i, 0)),
            pl.BlockSpec((None, tm, RWKV_WIDTH), lambda i: (1, i, 0)),
            tok(RWKV_WIDTH), tok(RWKV_WIDTH), tok(ATTN_WIDTH), tok(D_MODEL),
            full((1, RWKV_WIDTH)), full((1, RWKV_WIDTH)), full((RWKV_WIDTH, RWKV_WIDTH)),
            full((D_MODEL, D_MODEL)), full((1, D_MODEL)), full((N_EXPERTS, D_MODEL)),
        ],
        out_specs=[tok(D_MODEL), tok(D_MODEL), pl.BlockSpec((N_EXPERTS, tm), lambda i: (0, i))],
        out_shape=[
            jax.ShapeDtypeStruct((n, D_MODEL), F32),
            jax.ShapeDtypeStruct((n, D_MODEL), BF16),
            jax.ShapeDtypeStruct((N_EXPERTS, n), F32),
        ],
        compiler_params=_cparams(("parallel",)),
        name="post_mix",
    )(y2, y2, bonus, gate, oa, x, lnx_w, lnx_b, seg, w_out, norm2_g, w_router_t)


def _ffn_kernel(x_ref, gate_ref, wg_ref, wu_ref, wd_ref, o_ref):
    x = x_ref[...]
    hg = _dot(x, wg_ref[...])
    hu = _dot(x, wu_ref[...])
    act = (hg * _sigmoid(hg) * hu).astype(BF16)
    o_ref[...] = _dot(act, wd_ref[...]) * gate_ref[...]


def expert_ffn(xe, gates, wg, wu, wd, tm):
    e, cap, _ = xe.shape
    return pl.pallas_call(
        _ffn_kernel,
        grid=(e, cap // tm),
        in_specs=[
            pl.BlockSpec((None, tm, D_MODEL), lambda e, i: (e, i, 0)),
            pl.BlockSpec((None, tm, 1), lambda e, i: (e, i, 0)),
            pl.BlockSpec((None, D_MODEL, EXPERT_FF), lambda e, i: (e, 0, 0)),
            pl.BlockSpec((None, D_MODEL, EXPERT_FF), lambda e, i: (e, 0, 0)),
            pl.BlockSpec((None, EXPERT_FF, D_MODEL), lambda e, i: (e, 0, 0)),
        ],
        out_specs=pl.BlockSpec((None, tm, D_MODEL), lambda e, i: (e, i, 0)),
        out_shape=jax.ShapeDtypeStruct((e, cap, D_MODEL), F32),
        compiler_params=_cparams(("parallel", "parallel")),
        name="expert_ffn",
    )(xe, gates, wg, wu, wd)


def _rmsnorm_kernel(x_ref, g_ref, o_ref):
    x = x_ref[...]
    ms = jnp.mean(x * x, axis=-1, keepdims=True)
    o_ref[...] = x * lax.rsqrt(ms + NORM_EPS) * g_ref[...]


def final_norm(x, g, tm):
    n = x.shape[0]
    return pl.pallas_call(
        _rmsnorm_kernel,
        grid=(n // tm,),
        in_specs=[pl.BlockSpec((tm, D_MODEL), lambda i: (i, 0)),
                  pl.BlockSpec((1, D_MODEL), lambda i: (0, 0))],
        out_specs=pl.BlockSpec((tm, D_MODEL), lambda i: (i, 0)),
        out_shape=jax.ShapeDtypeStruct((n, D_MODEL), F32),
        compiler_params=_cparams(("parallel",)),
        name="final_norm",
    )(x, g)


def _pad_cols(w, width):
    return jnp.pad(w, ((0, 0),) * (w.ndim - 1) + ((0, width - w.shape[-1]),))


def _pad_lora_cols(z):
    o = RWKV_MAIN
    parts = [z[..., :o]]
    for width in (DECAY_LORA, DECAY_LORA, AAA_LORA, AAA_LORA, GATE_LORA):
        parts.append(_pad_cols(z[..., o:o + width], LORA_PAD))
        o += width
    return jnp.concatenate(parts, axis=-1)


def _pad_rows(w, rows):
    return jnp.pad(w, ((0, 0),) * (w.ndim - 2) + ((0, rows - w.shape[-2]), (0, 0)))


def prepare_weights(norm1_g, w_in, lambda_qk, subln_g, shift_mu, w0, w2, a0, a2, g2, k_k, k_a, r_k,
                    lnx_w, lnx_b, w_out, norm2_g, w_router, w_gate, w_up, w_down, final_g):
    scale = ATTN_QK_DIM ** -0.5
    w_attn = jnp.concatenate([w_in[..., :ATTN_WIDTH] * scale, w_in[..., ATTN_WIDTH:ATTN_COLS]], axis=-1)
    w_in_p = jnp.concatenate([w_attn, _pad_lora_cols(w_in[..., ATTN_COLS:])], axis=-1).astype(BF16)
    blk = jnp.arange(RWKV_WIDTH) // RWKV_HEAD
    return dict(
        norm1_g=norm1_g[:, None, :], w_in=w_in_p, lambda_qk=lambda_qk, subln_g=subln_g[:, None, :],
        mu=_pad_lora_cols(shift_mu), w0=w0, w2=_pad_rows(w2, LORA_PAD).astype(BF16),
        a0=a0, a2=_pad_rows(a2, LORA_PAD).astype(BF16), g2=_pad_rows(g2, LORA_PAD).astype(BF16),
        k_k=k_k[:, None, :], k_a=k_a[:, None, :], r_k=r_k.reshape(DEPTH, 1, RWKV_WIDTH),
        lnx_w=lnx_w[:, None, :], lnx_b=lnx_b[:, None, :], w_out=w_out.astype(BF16),
        norm2_g=norm2_g[:, None, :], w_router_t=jnp.swapaxes(w_router, 1, 2),
        w_gate=w_gate.astype(BF16), w_up=w_up.astype(BF16), w_down=w_down.astype(BF16),
        final_g=final_g[None, :],
        seg=(blk[:, None] == blk[None, :]).astype(F32),
        slopes=2.0 ** (-8.0 * jnp.arange(1, ATTN_HEADS + 1, dtype=F32) / ATTN_HEADS),
        scan_consts=scan_constants(),
    )


def route(aff_t, h, cap):
    gates, idx = lax.top_k(aff_t, cap)
    return gates, idx, h[idx]


def trunk(x, p, tiles):
    batch, seq, _ = x.shape
    n = batch * seq
    cap = CAPACITY_FACTOR * n // N_EXPERTS
    x = x.reshape(n, D_MODEL)
    for l in range(DEPTH):
        lambda_init = 0.8 - 0.6 * math.exp(-0.3 * l)
        za, zr = in_proj(x, p["norm1_g"][l], p["w_in"][l], tiles["tm"])
        oa = diff_attention(za, p["slopes"], p["lambda_qk"][l], p["subln_g"][l], batch, seq, lambda_init,
                            tiles["tq"], tiles["tk"])
        r, v, a, lw, kd, bb, bonus, gate = rwkv_prep(
            zr, p["mu"][l], p["w0"][l], p["w2"][l], p["a0"][l], p["a2"][l], p["g2"][l],
            p["k_k"][l], p["k_a"][l], p["r_k"][l], p["seg"], seq, tiles["tm"])
        y2 = rwkv_scan(r, v, a, lw, kd, bb, p["scan_consts"], batch, seq, tiles["tb"])
        x, h, aff_t = post_mix(y2, bonus, gate, oa, x, p["lnx_w"][l], p["lnx_b"][l], p["seg"],
                               p["w_out"][l], p["norm2_g"][l], p["w_router_t"][l], tiles["tm"])
        gates, idx, xe = route(aff_t, h, cap)
        ye = expert_ffn(xe, gates[..., None], p["w_gate"][l], p["w_up"][l], p["w_down"][l],
                        min(tiles["tf"], cap))
        x = x.at[idx.reshape(-1)].add(ye.reshape(-1, D_MODEL))
    return final_norm(x, p["final_g"], tiles["tm"]).reshape(batch, seq, D_MODEL)


def _tiles(seq):
    return dict(tm=min(512, seq), tq=min(512, seq), tk=min(512, seq), tb=min(256, seq), tf=512)


def kernel(x_prompt, x_sample, norm1_g, w_in, lambda_qk, subln_g, shift_mu, w0, w2, a0, a2, g2, k_k, k_a, r_k,
           lnx_w, lnx_b, w_out, norm2_g, w_router, w_gate, w_up, w_down, final_g):
    p = prepare_weights(norm1_g, w_in, lambda_qk, subln_g, shift_mu, w0, w2, a0, a2, g2, k_k, k_a, r_k,
                        lnx_w, lnx_b, w_out, norm2_g, w_router, w_gate, w_up, w_down, final_g)
    y_prompt = trunk(x_prompt, p, _tiles(x_prompt.shape[1]))
    y_sample = trunk(x_sample, p, _tiles(x_sample.shape[1]))
    return (y_prompt, y_sample)
```

```python
import functools
import math

import jax
import jax.numpy as jnp
from jax import lax
from jax.experimental import pallas as pl
from jax.experimental.pallas import tpu as pltpu

F32 = jnp.float32
BF16 = jnp.bfloat16
HIGHEST = lax.Precision.HIGHEST

D_MODEL = 1024
DEPTH = 4
ATTN_WIDTH = 512
ATTN_HEADS = 4
ATTN_QK_DIM = 64
ATTN_V_DIM = 128
RWKV_WIDTH = 512
RWKV_HEAD = 64
RWKV_HEADS = 8
DECAY_LORA = 32
AAA_LORA = 32
GATE_LORA = 96
LORA_PAD = 128
ATTN_COLS = 3 * ATTN_WIDTH
RWKV_MAIN = 3 * RWKV_WIDTH
RWKV_COLS_PAD = RWKV_MAIN + 5 * LORA_PAD
N_EXPERTS = 16
EXPERT_FF = 2 * D_MODEL
CAPACITY_FACTOR = 2
NORM_EPS = 1e-6
GN_EPS = 64e-5

CHUNK = 64
GROUP = 4
GROUP_W = GROUP * RWKV_HEAD
N_LEVELS = CHUNK.bit_length() - 1
VMEM_LIMIT = 56 * 1024 * 1024


def _cparams(sem):
    return pltpu.CompilerParams(dimension_semantics=sem, vmem_limit_bytes=VMEM_LIMIT)


def _dot(a, b):
    return jnp.dot(a, b, preferred_element_type=F32)


def _dot_nt(a, b):
    return lax.dot_general(a, b, (((1,), (1,)), ((), ())), preferred_element_type=F32)


def _dot_tn(a, b):
    return lax.dot_general(a, b, (((0,), (0,)), ((), ())), preferred_element_type=F32)


def _dot_f32(a, b):
    return jnp.dot(a, b, preferred_element_type=F32, precision=HIGHEST)


def _in_proj_kernel(x_ref, g_ref, w_ref, za_ref, zr_ref):
    x = x_ref[...]
    ms = jnp.mean(x * x, axis=-1, keepdims=True)
    h = (x * lax.rsqrt(ms + NORM_EPS) * g_ref[...]).astype(BF16)
    za_ref[...] = _dot(h, w_ref[:, :ATTN_COLS]).astype(BF16)
    zr_ref[...] = _dot(h, w_ref[:, ATTN_COLS:])


def in_proj(x, g, w, tm):
    n = x.shape[0]
    cols = w.shape[1]
    return pl.pallas_call(
        _in_proj_kernel,
        grid=(n // tm,),
        in_specs=[
            pl.BlockSpec((tm, D_MODEL), lambda i: (i, 0)),
            pl.BlockSpec((1, D_MODEL), lambda i: (0, 0)),
            pl.BlockSpec((D_MODEL, cols), lambda i: (0, 0)),
        ],
        out_specs=[
            pl.BlockSpec((tm, ATTN_COLS), lambda i: (i, 0)),
            pl.BlockSpec((tm, RWKV_COLS_PAD), lambda i: (i, 0)),
        ],
        out_shape=[
            jax.ShapeDtypeStruct((n, ATTN_COLS), BF16),
            jax.ShapeDtypeStruct((n, RWKV_COLS_PAD), F32),
        ],
        compiler_params=_cparams(("parallel",)),
        name="in_proj",
    )(x, g, w)


ROW_BLOCK = 64
POS_BITS = 9
M_INIT = -1e30


def _attn_kernel(kap_ref, lq_ref, q_ref, k_ref, v_ref, g_ref, qx_ref, kx_ref, dbias_ref, o_ref,
                 q2_ref, s_ref, p_ref, m_ref, al_ref, acc_ref, *, seq, tq, lambda_init):
    h = pl.program_id(1)
    i = pl.program_id(2)
    nk = seq // tq
    kap = kap_ref[h]
    q = q_ref[...]
    lane = lax.broadcasted_iota(jnp.int32, q.shape, 1)
    zero = jnp.zeros_like(q)
    q2 = jnp.concatenate([jnp.where(lane < ATTN_QK_DIM, q, zero),
                          jnp.where(lane >= ATTN_QK_DIM, q, zero)], axis=0)
    qx = qx_ref[...]
    qx2 = jnp.concatenate([qx, qx], axis=0)
    q2_ref[0] = jnp.concatenate([q2, qx2], axis=1)
    q2_ref[1] = jnp.concatenate([q2, -qx2], axis=1)
    kx = kx_ref[...]
    ones = jnp.ones((tq, ATTN_V_DIM), BF16)
    m_ref[...] = jnp.full(m_ref.shape, M_INIT, F32)
    acc_ref[...] = jnp.zeros(acc_ref.shape, F32)

    def softmax_and_values(j, kappa):
        start = pl.multiple_of(j * tq, tq)
        v_aug = jnp.concatenate([v_ref[pl.ds(start, tq), :], ones], axis=1)
        for rb in range(2 * tq // ROW_BLOCK):
            rows = slice(rb * ROW_BLOCK, (rb + 1) * ROW_BLOCK)
            s = s_ref[rows, :]
            m_old = m_ref[rows, :]
            m_new = jnp.maximum(m_old, jnp.max(s, axis=-1, keepdims=True) + kappa)
            p_ref[rows, :] = jnp.exp2(s - (m_new - kappa)).astype(BF16)
            al_ref[rows, :] = jnp.exp2(m_old - m_new)
            m_ref[rows, :] = m_new
        acc_ref[...] = al_ref[...] * acc_ref[...] + _dot(p_ref[...], v_aug)

    dstart = pl.multiple_of(i * tq, tq)
    s_diag = _dot_nt(q2, k_ref[pl.ds(dstart, tq), :])
    s_ref[...] = (s_diag.reshape(2, tq, tq) + dbias_ref[...][None]).reshape(2 * tq, tq)
    softmax_and_values(i, 0.0)

    def body(jj, carry):
        j = jj + (jj >= i).astype(jnp.int32)
        start = pl.multiple_of(j * tq, tq)
        k_aug = jnp.concatenate([k_ref[pl.ds(start, tq), :], kx], axis=1)
        s_ref[...] = _dot_nt(q2_ref[(j > i).astype(jnp.int32)], k_aug)
        softmax_and_values(j, -kap * jnp.abs(i - j).astype(F32))
        return carry

    lax.fori_loop(0, nk - 1, body, 0)

    lq = lq_ref[...]
    lam = (jnp.exp(jnp.sum(lq[0:1] * lq[1:2], axis=-1, keepdims=True))
           - jnp.exp(jnp.sum(lq[2:3] * lq[3:4], axis=-1, keepdims=True)) + lambda_init)
    acc = acc_ref[...]
    o_all = acc[:, :ATTN_V_DIM] / acc[:, ATTN_V_DIM:]
    o = o_all[:tq] - lam * o_all[tq:]
    ms = jnp.mean(o * o, axis=-1, keepdims=True)
    o = o * lax.rsqrt(ms + NORM_EPS) * g_ref[...] * (1.0 - lambda_init)
    o_ref[...] = o.astype(BF16)


def attention_constants(tq):
    assert tq <= (1 << POS_BITS)
    slopes = 2.0 ** (-8.0 * jnp.arange(1, ATTN_HEADS + 1, dtype=F32) / ATTN_HEADS)
    c = slopes * math.log2(math.e)
    cb = c[:, None] * (2.0 ** jnp.arange(POS_BITS, dtype=F32))[None, :]
    hi = cb.astype(BF16).astype(F32)
    lo = (cb - hi).astype(BF16).astype(F32)
    pos = jnp.arange(tq)
    bits = ((pos[:, None] >> jnp.arange(POS_BITS)[None, :]) & 1).astype(F32)
    hb = jnp.broadcast_to(bits[None], (ATTN_HEADS, tq, POS_BITS))
    const = lambda x: jnp.broadcast_to(x[:, None, :], (ATTN_HEADS, tq, POS_BITS))
    pad = jnp.zeros((ATTN_HEADS, tq, ATTN_V_DIM - 4 * POS_BITS), F32)
    qx = jnp.concatenate([const(hi), const(lo), hb, hb, pad], axis=-1).astype(BF16)
    kx = jnp.concatenate([hb, hb, const(-hi), const(-lo), pad], axis=-1).astype(BF16)
    dist = jnp.abs(pos[:, None] - pos[None, :]).astype(F32)
    dbias = -c[:, None, None] * dist[None]
    kap = c * tq
    return kap, qx, kx, dbias


def diff_attention(za, consts, lq, subln_g, batch, seq, lambda_init, tq):
    n = batch * seq
    nq = seq // tq
    kap, qx, kx, dbias = consts
    kern = functools.partial(_attn_kernel, seq=seq, tq=tq, lambda_init=lambda_init)
    per_head = lambda w: pl.BlockSpec((None, tq, w), lambda b, h, i: (h, 0, 0))
    return pl.pallas_call(
        kern,
        grid=(batch, ATTN_HEADS, nq),
        in_specs=[
            pl.BlockSpec(memory_space=pltpu.SMEM),
            pl.BlockSpec((4, ATTN_QK_DIM), lambda b, h, i: (0, 0)),
            pl.BlockSpec((tq, ATTN_V_DIM), lambda b, h, i: (b * nq + i, h)),
            pl.BlockSpec((seq, ATTN_V_DIM), lambda b, h, i: (b, ATTN_HEADS + h)),
            pl.BlockSpec((seq, ATTN_V_DIM), lambda b, h, i: (b, 2 * ATTN_HEADS + h)),
            pl.BlockSpec((1, ATTN_V_DIM), lambda b, h, i: (0, 0)),
            per_head(ATTN_V_DIM), per_head(ATTN_V_DIM), per_head(tq),
        ],
        out_specs=pl.BlockSpec((tq, ATTN_V_DIM), lambda b, h, i: (b * nq + i, h)),
        out_shape=jax.ShapeDtypeStruct((n, ATTN_WIDTH), BF16),
        scratch_shapes=[
            pltpu.VMEM((2, 2 * tq, 2 * ATTN_V_DIM), BF16),
            pltpu.VMEM((2 * tq, tq), F32),
            pltpu.VMEM((2 * tq, tq), BF16),
            pltpu.VMEM((2 * tq, 1), F32),
            pltpu.VMEM((2 * tq, 1), F32),
            pltpu.VMEM((2 * tq, 2 * ATTN_V_DIM), F32),
        ],
        compiler_params=_cparams(("parallel", "parallel", "parallel")),
        name="diff_attention",
    )(kap, lq, za, za, za, subln_g, qx, kx, dbias)


def _softplus(x):
    return jnp.maximum(x, 0.0) + jnp.log(1.0 + jnp.exp(-jnp.abs(x)))


def _sigmoid(x):
    return 1.0 / (1.0 + jnp.exp(-x))


def _prep_kernel(z_ref, zp_ref, zn_ref, mu_ref, w0_ref, w2_ref, a0_ref, a2_ref, g2_ref,
                 kk_ref, ka_ref, rk_ref, seg_ref,
                 r_ref, v_ref, a_ref, lw_ref, kd_ref, bb_ref, bonus_ref, gate_ref, *, seq, tm):
    i = pl.program_id(0)
    z = z_ref[...]
    row = lax.broadcasted_iota(jnp.int32, z.shape, 0)
    first = (i * tm) % seq == 0
    last = ((i + 1) * tm) % seq == 0
    zp = jnp.where(first, 0.0, zp_ref[7:8, :])
    zn = jnp.where(last, 0.0, zn_ref[0:1, :])
    prev = jnp.where(row == 0, zp, pltpu.roll(z, 1, 0))
    nxt = jnp.where(row == tm - 1, zn, pltpu.roll(z, tm - 1, 0))
    z = z + mu_ref[0:1, :] * (prev - z) + mu_ref[1:2, :] * (nxt - z)

    w = RWKV_WIDTH
    r = z[:, 0:w]
    k = z[:, w:2 * w]
    v = z[:, 2 * w:3 * w]
    seg = seg_ref[...]
    kk = k * kk_ref[...]
    ss = _dot_f32(kk * kk, seg)
    kk = kk / jnp.maximum(jnp.sqrt(ss), 1e-12)
    r_ref[...] = r
    v_ref[...] = v
    a_ref[...] = -kk
    for d in range(2):
        xw = z[:, RWKV_MAIN + d * LORA_PAD:RWKV_MAIN + (d + 1) * LORA_PAD]
        xa = z[:, RWKV_MAIN + (2 + d) * LORA_PAD:RWKV_MAIN + (3 + d) * LORA_PAD]
        wlog = -_softplus(-(w0_ref[d:d + 1, :] + _dot(jnp.tanh(xw).astype(BF16), w2_ref[d]))) - 0.5
        lw_ref[d] = -jnp.exp(wlog)
        iclr = _sigmoid(a0_ref[d:d + 1, :] + _dot(xa.astype(BF16), a2_ref[d]))
        kd_ref[d] = k * (1.0 + (iclr - 1.0) * ka_ref[...])
        bb_ref[d] = kk * iclr
    xg = z[:, RWKV_MAIN + 4 * LORA_PAD:RWKV_MAIN + 5 * LORA_PAD]
    gate_ref[...] = _dot(_sigmoid(xg).astype(BF16), g2_ref[...])
    bonus_ref[...] = _dot_f32(r * k * rk_ref[...], seg) * v


def rwkv_prep(zr, mu, w0, w2, a0, a2, g2, k_k, k_a, r_k, seg, seq, tm):
    n, cols = zr.shape
    nb8 = n // 8
    tb8 = tm // 8
    full = lambda shape: pl.BlockSpec(shape, lambda i: (0,) * len(shape))
    tok = pl.BlockSpec((tm, RWKV_WIDTH), lambda i: (i, 0))
    tok2 = pl.BlockSpec((2, tm, RWKV_WIDTH), lambda i: (0, i, 0))
    sds = jax.ShapeDtypeStruct((n, RWKV_WIDTH), F32)
    sds2 = jax.ShapeDtypeStruct((2, n, RWKV_WIDTH), F32)
    return pl.pallas_call(
        functools.partial(_prep_kernel, seq=seq, tm=tm),
        grid=(n // tm,),
        in_specs=[
            pl.BlockSpec((tm, cols), lambda i: (i, 0)),
            pl.BlockSpec((8, cols), lambda i: (jnp.maximum(i * tb8 - 1, 0), 0)),
            pl.BlockSpec((8, cols), lambda i: (jnp.minimum((i + 1) * tb8, nb8 - 1), 0)),
            full((2, cols)), full((2, RWKV_WIDTH)), full((2, LORA_PAD, RWKV_WIDTH)),
            full((2, RWKV_WIDTH)), full((2, LORA_PAD, RWKV_WIDTH)), full((LORA_PAD, RWKV_WIDTH)),
            full((1, RWKV_WIDTH)), full((1, RWKV_WIDTH)), full((1, RWKV_WIDTH)),
            full((RWKV_WIDTH, RWKV_WIDTH)),
        ],
        out_specs=[tok, tok, tok, tok2, tok2, tok2, tok, tok],
        out_shape=[sds, sds, sds, sds2, sds2, sds2, sds, sds],
        compiler_params=_cparams(("parallel",)),
        name="rwkv_prep",
    )(zr, zr, zr, mu, w0, w2, a0, a2, g2, k_k, k_a, r_k, seg)


def _scan_kernel(r_ref, v_ref, a_ref, lw_ref, k_ref, b_ref, tri_ref, strict_ref, incl_ref, lvl_ref, eye_ref, bd_ref,
                 y_ref, h_ref, *, tb):
    d = pl.program_id(1)
    i = pl.program_id(2)
    nc = tb // CHUNK
    ngrp = RWKV_WIDTH // GROUP_W

    @pl.when(i == 0)
    def _():
        h_ref[...] = jnp.zeros(h_ref.shape, F32)

    tri = tri_ref[...]
    strict = strict_ref[...] > 0.5
    incl = incl_ref[...] > 0.5
    eye = eye_ref[...]
    bdm = bd_ref[...] > 0.5
    ones8 = jnp.full((8, CHUNK), 0.125, F32)
    ones8w = jnp.ones((8, GROUP_W), F32)

    def bd(x):
        return jnp.where(bdm, jnp.concatenate([x] * GROUP, axis=0), 0.0).astype(BF16)

    def local(sl, cols):
        lw = lw_ref[sl, cols]
        r = r_ref[sl, cols]
        v = v_ref[sl, cols]
        a = a_ref[sl, cols]
        k = k_ref[sl, cols]
        b = b_ref[sl, cols]
        lcum = _dot_f32(tri, lw)
        ltot8 = _dot_f32(ones8, lw) * 8.0
        ltot = ltot8[0:1]
        at = a * jnp.exp(lcum - lw)
        rt = r * jnp.exp(lcum)
        einv = jnp.exp(-lcum)
        bt = b * einv
        kt = k * einv
        eend = jnp.exp(ltot - lcum)
        x = jnp.concatenate([at, rt], axis=0).astype(BF16)
        ybk = jnp.concatenate([bd(bt), bd(kt)], axis=0)
        amat = _dot_nt(x, ybk)
        aab = jnp.where(strict, amat[:CHUNK, :GROUP_W], 0.0)
        aak = jnp.where(strict, amat[:CHUNK, GROUP_W:], 0.0)
        arb = jnp.where(incl, amat[CHUNK:, :GROUP_W], 0.0)
        ark = jnp.where(incl, amat[CHUNK:, GROUP_W:], 0.0)
        p = eye + jnp.where(lvl_ref[0] > 0.5, aab, 0.0)
        for lvl in range(1, N_LEVELS):
            e = jnp.where(lvl_ref[lvl] > 0.5, aab, 0.0)
            g = _dot(p.astype(BF16), bd(e))
            p = p + _dot(g.astype(BF16), bd(p))
        vbd = bd(v)
        av = _dot(aak.astype(BF16), vbd)
        wcol = lax.dot_general(jnp.exp(ltot8) * 0.125, ones8w, (((0,), (0,)), ((), ())),
                               preferred_element_type=F32, precision=HIGHEST)
        return dict(x=x, p=p.astype(BF16), av=av, vbd=vbd, v=v.astype(BF16),
                    ar=jnp.concatenate([arb, ark], axis=1).astype(BF16),
                    bkw=jnp.concatenate([b * eend, k * eend], axis=0).astype(BF16), wcol=wcol)

    slices = []
    for cc in range(nc):
        ci = cc + d * (nc - 1 - 2 * cc)
        slices.append(pl.ds(pl.multiple_of(ci * CHUNK, CHUNK), CHUNK))
    groups = [slice(g * GROUP_W, (g + 1) * GROUP_W) for g in range(ngrp)]
    loc = [[local(sl, cols) for cols in groups] for sl in slices]

    for cc in range(nc):
        for g in range(ngrp):
            c = loc[cc][g]
            h = h_ref[g]
            hx = _dot(c["x"], h.astype(BF16))
            u = _dot(c["p"], bd(hx[:CHUNK] + c["av"]))
            y = hx[CHUNK:] + _dot(c["ar"], jnp.concatenate([bd(u), c["vbd"]], axis=0))
            y_ref[slices[cc], groups[g]] = y
            upd = _dot_tn(c["bkw"], jnp.concatenate([u.astype(BF16), c["v"]], axis=0))
            h_ref[g] = c["wcol"] * h + jnp.where(bdm, upd, 0.0)


def rwkv_scan(r, v, a, lw, kd, bb, consts, batch, seq, tb):
    n = batch * seq
    nblk = seq // tb
    ngrp = RWKV_WIDTH // GROUP_W
    tri, strict, incl, lvl, eye, bdm = consts

    def tmap(b, d, i):
        return (b * nblk + i + d * (nblk - 1 - 2 * i), 0)

    def tmap_d(b, d, i):
        return (d, b * nblk + i + d * (nblk - 1 - 2 * i), 0)

    tok = pl.BlockSpec((tb, RWKV_WIDTH), tmap)
    tok_d = pl.BlockSpec((None, tb, RWKV_WIDTH), tmap_d)
    per_d = lambda shape: pl.BlockSpec((None,) + shape, lambda b, d, i: (d,) + (0,) * len(shape))
    full = lambda shape: pl.BlockSpec(shape, lambda b, d, i: (0, 0))
    return pl.pallas_call(
        functools.partial(_scan_kernel, tb=tb),
        grid=(batch, 2, nblk),
        in_specs=[tok, tok, tok, tok_d, tok_d, tok_d,
                  per_d((CHUNK, CHUNK)), per_d((CHUNK, GROUP * CHUNK)), per_d((CHUNK, GROUP * CHUNK)),
                  per_d((N_LEVELS, CHUNK, GROUP * CHUNK)),
                  full((CHUNK, GROUP * CHUNK)), full((GROUP_W, GROUP_W))],
        out_specs=tok_d,
        out_shape=jax.ShapeDtypeStruct((2, n, RWKV_WIDTH), F32),
        scratch_shapes=[pltpu.VMEM((ngrp, GROUP_W, GROUP_W), F32)],
        compiler_params=_cparams(("parallel", "parallel", "arbitrary")),
        name="rwkv_scan",
    )(r, v, a, lw, kd, bb, tri, strict, incl, lvl, eye, bdm)


def scan_constants():
    t = jnp.arange(CHUNK)
    before_eq = (t[None, :] <= t[:, None])
    tri = jnp.stack([before_eq, before_eq.T]).astype(F32)
    incl = jnp.tile(tri, (1, 1, GROUP))
    eye = jnp.tile(jnp.eye(CHUNK, dtype=F32), (1, GROUP))
    strict = incl - eye[None]
    levels = []
    for lv in range(N_LEVELS):
        half = t // (1 << lv)
        fwd = ((half[:, None] // 2 == half[None, :] // 2) & (half[:, None] % 2 == 1) & (half[None, :] % 2 == 0))
        levels.append(jnp.stack([fwd, fwd.T]))
    lvl = jnp.tile(jnp.stack(levels, axis=1).astype(F32), (1, 1, 1, GROUP))
    blk = jnp.arange(GROUP_W) // RWKV_HEAD
    bdm = (blk[:, None] == blk[None, :]).astype(F32)
    return tri, strict, incl, lvl, eye, bdm


def _post_kernel(yf_ref, yb_ref, bonus_ref, gate_ref, oa_ref, x_ref, lw_ref, lb_ref, seg_ref,
                 wo_ref, g2_ref, wr_ref, xo_ref, h_ref, aff_ref):
    y = yf_ref[...] + yb_ref[...]
    seg = seg_ref[...]
    mean = _dot_f32(y, seg) * (1.0 / RWKV_HEAD)
    yc = y - mean
    var = _dot_f32(yc * yc, seg) * (1.0 / RWKV_HEAD)
    yn = yc * lax.rsqrt(var + GN_EPS) * lw_ref[...] + lb_ref[...]
    orw = ((yn + bonus_ref[...]) * gate_ref[...]).astype(BF16)
    x = (x_ref[...] + _dot(oa_ref[...], wo_ref[:ATTN_WIDTH, :]) + _dot(orw, wo_ref[ATTN_WIDTH:, :]))
    xo_ref[...] = x
    ms = jnp.mean(x * x, axis=-1, keepdims=True)
    h = x * lax.rsqrt(ms + NORM_EPS) * g2_ref[...]
    h_ref[...] = h.astype(BF16)
    logits = lax.dot_general(wr_ref[...], h, (((1,), (1,)), ((), ())),
                             preferred_element_type=F32, precision=HIGHEST)
    logits = logits - jnp.max(logits, axis=0, keepdims=True)
    e = jnp.exp(logits)
    aff_ref[...] = e / jnp.sum(e, axis=0, keepdims=True)


def post_mix(y2, bonus, gate, oa, x, lnx_w, lnx_b, seg, w_out, norm2_g, w_router_t, tm):
    n = x.shape[0]
    full = lambda shape: pl.BlockSpec(shape, lambda i: (0,) * len(shape))
    tok = lambda w: pl.BlockSpec((tm, w), lambda i: (i, 0))
    return pl.pallas_call(
        _post_kernel,
        grid=(n // tm,),
        in_specs=[
            pl.BlockSpec((None, tm, RWKV_WIDTH), lambda i: (0, i, 0)),
            pl.BlockSpec((None, tm, RWKV_WIDTH), lambda i: (1, i, 0)),
            tok(RWKV_WIDTH), tok(RWKV_WIDTH), tok(ATTN_WIDTH), tok(D_MODEL),
            full((1, RWKV_WIDTH)), full((1, RWKV_WIDTH)), full((RWKV_WIDTH, RWKV_WIDTH)),
            full((D_MODEL, D_MODEL)), full((1, D_MODEL)), full((N_EXPERTS, D_MODEL)),
        ],
        out_specs=[tok(D_MODEL), tok(D_MODEL), pl.BlockSpec((N_EXPERTS, tm), lambda i: (0, i))],
        out_shape=[
            jax.ShapeDtypeStruct((n, D_MODEL), F32),
            jax.ShapeDtypeStruct((n, D_MODEL), BF16),
            jax.ShapeDtypeStruct((N_EXPERTS, n), F32),
        ],
        compiler_params=_cparams(("parallel",)),
        name="post_mix",
    )(y2, y2, bonus, gate, oa, x, lnx_w, lnx_b, seg, w_out, norm2_g, w_router_t)


def _ffn_kernel(x_ref, gate_ref, wg_ref, wu_ref, wd_ref, o_ref):
    x = x_ref[...]
    hg = _dot(x, wg_ref[...])
    hu = _dot(x, wu_ref[...])
    act = (hg * _sigmoid(hg) * hu).astype(BF16)
    o_ref[...] = _dot(act, wd_ref[...]) * gate_ref[...]


def expert_ffn(xe, gates, wg, wu, wd, tm):
    e, cap, _ = xe.shape
    return pl.pallas_call(
        _ffn_kernel,
        grid=(e, cap // tm),
        in_specs=[
            pl.BlockSpec((None, tm, D_MODEL), lambda e, i: (e, i, 0)),
            pl.BlockSpec((None, tm, 1), lambda e, i: (e, i, 0)),
            pl.BlockSpec((None, D_MODEL, EXPERT_FF), lambda e, i: (e, 0, 0)),
            pl.BlockSpec((None, D_MODEL, EXPERT_FF), lambda e, i: (e, 0, 0)),
            pl.BlockSpec((None, EXPERT_FF, D_MODEL), lambda e, i: (e, 0, 0)),
        ],
        out_specs=pl.BlockSpec((None, tm, D_MODEL), lambda e, i: (e, i, 0)),
        out_shape=jax.ShapeDtypeStruct((e, cap, D_MODEL), F32),
        compiler_params=_cparams(("parallel", "parallel")),
        name="expert_ffn",
    )(xe, gates, wg, wu, wd)


def _rmsnorm_kernel(x_ref, g_ref, o_ref):
    x = x_ref[...]
    ms = jnp.mean(x * x, axis=-1, keepdims=True)
    o_ref[...] = x * lax.rsqrt(ms + NORM_EPS) * g_ref[...]


def final_norm(x, g, tm):
    n = x.shape[0]
    return pl.pallas_call(
        _rmsnorm_kernel,
        grid=(n // tm,),
        in_specs=[pl.BlockSpec((tm, D_MODEL), lambda i: (i, 0)),
                  pl.BlockSpec((1, D_MODEL), lambda i: (0, 0))],
        out_specs=pl.BlockSpec((tm, D_MODEL), lambda i: (i, 0)),
        out_shape=jax.ShapeDtypeStruct((n, D_MODEL), F32),
        compiler_params=_cparams(("parallel",)),
        name="final_norm",
    )(x, g)


def _pad_cols(w, width):
    return jnp.pad(w, ((0, 0),) * (w.ndim - 1) + ((0, width - w.shape[-1]),))


def _pad_lora_cols(z):
    o = RWKV_MAIN
    parts = [z[..., :o]]
    for width in (DECAY_LORA, DECAY_LORA, AAA_LORA, AAA_LORA, GATE_LORA):
        parts.append(_pad_cols(z[..., o:o + width], LORA_PAD))
        o += width
    return jnp.concatenate(parts, axis=-1)


def _pad_rows(w, rows):
    return jnp.pad(w, ((0, 0),) * (w.ndim - 2) + ((0, rows - w.shape[-2]), (0, 0)))


def prepare_weights(norm1_g, w_in, lambda_qk, subln_g, shift_mu, w0, w2, a0, a2, g2, k_k, k_a, r_k,
                    lnx_w, lnx_b, w_out, norm2_g, w_router, w_gate, w_up, w_down, final_g):
    scale = ATTN_QK_DIM ** -0.5 * math.log2(math.e)
    w_attn = jnp.concatenate([w_in[..., :ATTN_WIDTH] * scale, w_in[..., ATTN_WIDTH:ATTN_COLS]], axis=-1)
    w_in_p = jnp.concatenate([w_attn, _pad_lora_cols(w_in[..., ATTN_COLS:])], axis=-1).astype(BF16)
    blk = jnp.arange(RWKV_WIDTH) // RWKV_HEAD
    return dict(
        norm1_g=norm1_g[:, None, :], w_in=w_in_p, lambda_qk=lambda_qk, subln_g=subln_g[:, None, :],
        mu=_pad_lora_cols(shift_mu), w0=w0, w2=_pad_rows(w2, LORA_PAD).astype(BF16),
        a0=a0, a2=_pad_rows(a2, LORA_PAD).astype(BF16), g2=_pad_rows(g2, LORA_PAD).astype(BF16),
        k_k=k_k[:, None, :], k_a=k_a[:, None, :], r_k=r_k.reshape(DEPTH, 1, RWKV_WIDTH),
        lnx_w=lnx_w[:, None, :], lnx_b=lnx_b[:, None, :], w_out=w_out.astype(BF16),
        norm2_g=norm2_g[:, None, :], w_router_t=jnp.swapaxes(w_router, 1, 2),
        w_gate=w_gate.astype(BF16), w_up=w_up.astype(BF16), w_down=w_down.astype(BF16),
        final_g=final_g[None, :],
        seg=(blk[:, None] == blk[None, :]).astype(F32),
        scan_consts=scan_constants(),
    )


def route(aff_t, h, cap):
    gates, idx = lax.top_k(aff_t, cap)
    return gates, idx, h[idx]


def trunk(x, p, tiles):
    batch, seq, _ = x.shape
    n = batch * seq
    cap = CAPACITY_FACTOR * n // N_EXPERTS
    x = x.reshape(n, D_MODEL)
    attn_consts = attention_constants(tiles["tq"])
    for l in range(DEPTH):
        lambda_init = 0.8 - 0.6 * math.exp(-0.3 * l)
        za, zr = in_proj(x, p["norm1_g"][l], p["w_in"][l], tiles["tm"])
        oa = diff_attention(za, attn_consts, p["lambda_qk"][l], p["subln_g"][l], batch, seq, lambda_init,
                            tiles["tq"])
        r, v, a, lw, kd, bb, bonus, gate = rwkv_prep(
            zr, p["mu"][l], p["w0"][l], p["w2"][l], p["a0"][l], p["a2"][l], p["g2"][l],
            p["k_k"][l], p["k_a"][l], p["r_k"][l], p["seg"], seq, tiles["tm"])
        y2 = rwkv_scan(r, v, a, lw, kd, bb, p["scan_consts"], batch, seq, tiles["tb"])
        x, h, aff_t = post_mix(y2, bonus, gate, oa, x, p["lnx_w"][l], p["lnx_b"][l], p["seg"],
                               p["w_out"][l], p["norm2_g"][l], p["w_router_t"][l], tiles["tm"])
        gates, idx, xe = route(aff_t, h, cap)
        ye = expert_ffn(xe, gates[..., None], p["w_gate"][l], p["w_up"][l], p["w_down"][l],
                        min(tiles["tf"], cap))
        x = x.at[idx.reshape(-1)].add(ye.reshape(-1, D_MODEL))
    return final_norm(x, p["final_g"], tiles["tm"]).reshape(batch, seq, D_MODEL)


def _tiles(seq):
    return dict(tm=min(512, seq), tq=min(512, seq), tb=min(256, seq), tf=512)


def kernel(x_prompt, x_sample, norm1_g, w_in, lambda_qk, subln_g, shift_mu, w0, w2, a0, a2, g2, k_k, k_a, r_k,
           lnx_w, lnx_b, w_out, norm2_g, w_router, w_gate, w_up, w_down, final_g):
    p = prepare_weights(norm1_g, w_in, lambda_qk, subln_g, shift_mu, w0, w2, a0, a2, g2, k_k, k_a, r_k,
                        lnx_w, lnx_b, w_out, norm2_g, w_router, w_gate, w_up, w_down, final_g)
    y_prompt = trunk(x_prompt, p, _tiles(x_prompt.shape[1]))
    y_sample = trunk(x_sample, p, _tiles(x_sample.shape[1]))
    return (y_prompt, y_sample)
```

```python
import functools
import math

import jax
import jax.numpy as jnp
from jax import lax
from jax.experimental import pallas as pl
from jax.experimental.pallas import tpu as pltpu

F32 = jnp.float32
BF16 = jnp.bfloat16
HIGHEST = lax.Precision.HIGHEST

D_MODEL = 1024
DEPTH = 4
ATTN_WIDTH = 512
ATTN_HEADS = 4
ATTN_QK_DIM = 64
ATTN_V_DIM = 128
RWKV_WIDTH = 512
RWKV_HEAD = 64
RWKV_HEADS = 8
DECAY_LORA = 32
AAA_LORA = 32
GATE_LORA = 96
LORA_PAD = 128
ATTN_COLS = 3 * ATTN_WIDTH
RWKV_MAIN = 3 * RWKV_WIDTH
RWKV_COLS_PAD = RWKV_MAIN + 5 * LORA_PAD
N_EXPERTS = 16
EXPERT_FF = 2 * D_MODEL
CAPACITY_FACTOR = 2
NORM_EPS = 1e-6
GN_EPS = 64e-5

CHUNK = 64
GROUP = 4
GROUP_W = GROUP * RWKV_HEAD
N_LEVELS = CHUNK.bit_length() - 1
VMEM_LIMIT = 56 * 1024 * 1024


def _cparams(sem):
    return pltpu.CompilerParams(dimension_semantics=sem, vmem_limit_bytes=VMEM_LIMIT)


def _dot(a, b):
    return jnp.dot(a, b, preferred_element_type=F32)


def _dot_nt(a, b):
    return lax.dot_general(a, b, (((1,), (1,)), ((), ())), preferred_element_type=F32)


def _dot_tn(a, b):
    return lax.dot_general(a, b, (((0,), (0,)), ((), ())), preferred_element_type=F32)


def _dot_f32(a, b):
    return jnp.dot(a, b, preferred_element_type=F32, precision=HIGHEST)


def _in_proj_kernel(x_ref, g_ref, w_ref, za_ref, zr_ref):
    x = x_ref[...]
    ms = jnp.mean(x * x, axis=-1, keepdims=True)
    h = (x * lax.rsqrt(ms + NORM_EPS) * g_ref[...]).astype(BF16)
    za_ref[...] = _dot(h, w_ref[:, :ATTN_COLS]).astype(BF16)
    zr_ref[...] = _dot(h, w_ref[:, ATTN_COLS:])


def in_proj(x, g, w, tm):
    n = x.shape[0]
    cols = w.shape[1]
    return pl.pallas_call(
        _in_proj_kernel,
        grid=(n // tm,),
        in_specs=[
            pl.BlockSpec((tm, D_MODEL), lambda i: (i, 0)),
            pl.BlockSpec((1, D_MODEL), lambda i: (0, 0)),
            pl.BlockSpec((D_MODEL, cols), lambda i: (0, 0)),
        ],
        out_specs=[
            pl.BlockSpec((tm, ATTN_COLS), lambda i: (i, 0)),
            pl.BlockSpec((tm, RWKV_COLS_PAD), lambda i: (i, 0)),
        ],
        out_shape=[
            jax.ShapeDtypeStruct((n, ATTN_COLS), BF16),
            jax.ShapeDtypeStruct((n, RWKV_COLS_PAD), F32),
        ],
        compiler_params=_cparams(("parallel",)),
        name="in_proj",
    )(x, g, w)


ROW_BLOCK = 64
POS_BITS = 9
M_INIT = -1e30


def _attn_kernel(kap_ref, lq_ref, q_ref, k_ref, v_ref, g_ref, qx_ref, kx_ref, dbias_ref, o_ref,
                 q2_ref, s_ref, p_ref, m_ref, al_ref, sh_ref, acc_ref, *, seq, tq, lambda_init):
    h = pl.program_id(1)
    i = pl.program_id(2)
    nk = seq // tq
    kap = kap_ref[h]
    q = q_ref[...]
    lane = lax.broadcasted_iota(jnp.int32, q.shape, 1)
    zero = jnp.zeros_like(q)
    q2 = jnp.concatenate([jnp.where(lane < ATTN_QK_DIM, q, zero),
                          jnp.where(lane >= ATTN_QK_DIM, q, zero)], axis=0)
    qx = qx_ref[...]
    qx2 = jnp.concatenate([qx, qx], axis=0)
    q2_ref[0] = jnp.concatenate([q2, qx2], axis=1)
    q2_ref[1] = jnp.concatenate([q2, -qx2], axis=1)
    kx = kx_ref[...]
    ones = jnp.ones((tq, ATTN_V_DIM), BF16)
    m_ref[...] = jnp.full(m_ref.shape, M_INIT, F32)
    acc_ref[...] = jnp.zeros(acc_ref.shape, F32)

    def tile_of(step):
        jj = step - 1
        return jnp.where(step == 0, i, jj + (jj >= i).astype(jnp.int32))

    def scores(step, buf):
        j = tile_of(step)
        start = pl.multiple_of(j * tq, tq)
        k_aug = jnp.concatenate([k_ref[pl.ds(start, tq), :], kx], axis=1)
        s_ref[buf] = _dot_nt(q2_ref[(j > i).astype(jnp.int32)], k_aug)

    def softmax_and_values(step, buf):
        j = tile_of(step)
        kappa = -kap * jnp.abs(i - j).astype(F32)
        start = pl.multiple_of(j * tq, tq)
        v_aug = jnp.concatenate([v_ref[pl.ds(start, tq), :], ones], axis=1)
        for rb in range(2 * tq // ROW_BLOCK):
            rows = slice(rb * ROW_BLOCK, (rb + 1) * ROW_BLOCK)
            m_old = m_ref[rows, :]
            m_new = jnp.maximum(m_old, jnp.max(s_ref[buf, rows, :], axis=-1, keepdims=True) + kappa)
            al_ref[rows, :] = jnp.exp2(m_old - m_new)
            sh_ref[rows, :] = m_new - kappa
            m_ref[rows, :] = m_new
        for rb in range(2 * tq // ROW_BLOCK):
            rows = slice(rb * ROW_BLOCK, (rb + 1) * ROW_BLOCK)
            p_ref[rows, :] = jnp.exp2(s_ref[buf, rows, :] - sh_ref[rows, :]).astype(BF16)
        acc_ref[...] = al_ref[...] * acc_ref[...] + _dot(p_ref[...], v_aug)

    dstart = pl.multiple_of(i * tq, tq)
    s_diag = _dot_nt(q2, k_ref[pl.ds(dstart, tq), :])
    s_ref[0] = (s_diag.reshape(2, tq, tq) + dbias_ref[...][None]).reshape(2 * tq, tq)

    def pair(t, carry):
        scores(2 * t + 1, 1)
        softmax_and_values(2 * t, 0)
        scores(2 * t + 2, 0)
        softmax_and_values(2 * t + 1, 1)
        return carry

    lax.fori_loop(0, nk // 2 - 1, pair, 0)
    scores(nk - 1, 1)
    softmax_and_values(nk - 2, 0)
    softmax_and_values(nk - 1, 1)

    lq = lq_ref[...]
    lam = (jnp.exp(jnp.sum(lq[0:1] * lq[1:2], axis=-1, keepdims=True))
           - jnp.exp(jnp.sum(lq[2:3] * lq[3:4], axis=-1, keepdims=True)) + lambda_init)
    acc = acc_ref[...]
    o_all = acc[:, :ATTN_V_DIM] / acc[:, ATTN_V_DIM:]
    o = o_all[:tq] - lam * o_all[tq:]
    ms = jnp.mean(o * o, axis=-1, keepdims=True)
    o = o * lax.rsqrt(ms + NORM_EPS) * g_ref[...] * (1.0 - lambda_init)
    o_ref[...] = o.astype(BF16)


def attention_constants(tq):
    assert tq <= (1 << POS_BITS)
    slopes = 2.0 ** (-8.0 * jnp.arange(1, ATTN_HEADS + 1, dtype=F32) / ATTN_HEADS)
    c = slopes * math.log2(math.e)
    cb = c[:, None] * (2.0 ** jnp.arange(POS_BITS, dtype=F32))[None, :]
    hi = cb.astype(BF16).astype(F32)
    lo = (cb - hi).astype(BF16).astype(F32)
    pos = jnp.arange(tq)
    bits = ((pos[:, None] >> jnp.arange(POS_BITS)[None, :]) & 1).astype(F32)
    hb = jnp.broadcast_to(bits[None], (ATTN_HEADS, tq, POS_BITS))
    const = lambda x: jnp.broadcast_to(x[:, None, :], (ATTN_HEADS, tq, POS_BITS))
    pad = jnp.zeros((ATTN_HEADS, tq, ATTN_V_DIM - 4 * POS_BITS), F32)
    qx = jnp.concatenate([const(hi), const(lo), hb, hb, pad], axis=-1).astype(BF16)
    kx = jnp.concatenate([hb, hb, const(-hi), const(-lo), pad], axis=-1).astype(BF16)
    dist = jnp.abs(pos[:, None] - pos[None, :]).astype(F32)
    dbias = -c[:, None, None] * dist[None]
    kap = c * tq
    return kap, qx, kx, dbias


def diff_attention(za, consts, lq, subln_g, batch, seq, lambda_init, tq):
    n = batch * seq
    nq = seq // tq
    kap, qx, kx, dbias = consts
    assert nq % 2 == 0, "key tiles are processed in pairs"
    kern = functools.partial(_attn_kernel, seq=seq, tq=tq, lambda_init=lambda_init)
    per_head = lambda w: pl.BlockSpec((None, tq, w), lambda b, h, i: (h, 0, 0))
    return pl.pallas_call(
        kern,
        grid=(batch, ATTN_HEADS, nq),
        in_specs=[
            pl.BlockSpec(memory_space=pltpu.SMEM),
            pl.BlockSpec((4, ATTN_QK_DIM), lambda b, h, i: (0, 0)),
            pl.BlockSpec((tq, ATTN_V_DIM), lambda b, h, i: (b * nq + i, h)),
            pl.BlockSpec((seq, ATTN_V_DIM), lambda b, h, i: (b, ATTN_HEADS + h)),
            pl.BlockSpec((seq, ATTN_V_DIM), lambda b, h, i: (b, 2 * ATTN_HEADS + h)),
            pl.BlockSpec((1, ATTN_V_DIM), lambda b, h, i: (0, 0)),
            per_head(ATTN_V_DIM), per_head(ATTN_V_DIM), per_head(tq),
        ],
        out_specs=pl.BlockSpec((tq, ATTN_V_DIM), lambda b, h, i: (b * nq + i, h)),
        out_shape=jax.ShapeDtypeStruct((n, ATTN_WIDTH), BF16),
        scratch_shapes=[
            pltpu.VMEM((2, 2 * tq, 2 * ATTN_V_DIM), BF16),
            pltpu.VMEM((2, 2 * tq, tq), F32),
            pltpu.VMEM((2 * tq, tq), BF16),
            pltpu.VMEM((2 * tq, 1), F32),
            pltpu.VMEM((2 * tq, 1), F32),
            pltpu.VMEM((2 * tq, 1), F32),
            pltpu.VMEM((2 * tq, 2 * ATTN_V_DIM), F32),
        ],
        compiler_params=_cparams(("parallel", "parallel", "parallel")),
        name="diff_attention",
    )(kap, lq, za, za, za, subln_g, qx, kx, dbias)


def _softplus(x):
    return jnp.maximum(x, 0.0) + jnp.log(1.0 + jnp.exp(-jnp.abs(x)))


def _sigmoid(x):
    return 1.0 / (1.0 + jnp.exp(-x))


def _prep_kernel(z_ref, zp_ref, zn_ref, mu_ref, w0_ref, w2_ref, a0_ref, a2_ref, g2_ref,
                 kk_ref, ka_ref, rk_ref, seg_ref,
                 r_ref, v_ref, a_ref, lw_ref, kd_ref, bb_ref, bonus_ref, gate_ref, *, seq, tm):
    i = pl.program_id(0)
    z = z_ref[...]
    row = lax.broadcasted_iota(jnp.int32, z.shape, 0)
    first = (i * tm) % seq == 0
    last = ((i + 1) * tm) % seq == 0
    zp = jnp.where(first, 0.0, zp_ref[7:8, :])
    zn = jnp.where(last, 0.0, zn_ref[0:1, :])
    prev = jnp.where(row == 0, zp, pltpu.roll(z, 1, 0))
    nxt = jnp.where(row == tm - 1, zn, pltpu.roll(z, tm - 1, 0))
    z = z + mu_ref[0:1, :] * (prev - z) + mu_ref[1:2, :] * (nxt - z)

    w = RWKV_WIDTH
    r = z[:, 0:w]
    k = z[:, w:2 * w]
    v = z[:, 2 * w:3 * w]
    seg = seg_ref[...]
    kk = k * kk_ref[...]
    ss = _dot_f32(kk * kk, seg)
    kk = kk / jnp.maximum(jnp.sqrt(ss), 1e-12)
    r_ref[...] = r
    v_ref[...] = v
    a_ref[...] = -kk
    for d in range(2):
        xw = z[:, RWKV_MAIN + d * LORA_PAD:RWKV_MAIN + (d + 1) * LORA_PAD]
        xa = z[:, RWKV_MAIN + (2 + d) * LORA_PAD:RWKV_MAIN + (3 + d) * LORA_PAD]
        wlog = -_softplus(-(w0_ref[d:d + 1, :] + _dot(jnp.tanh(xw).astype(BF16), w2_ref[d]))) - 0.5
        lw_ref[d] = -jnp.exp(wlog)
        iclr = _sigmoid(a0_ref[d:d + 1, :] + _dot(xa.astype(BF16), a2_ref[d]))
        kd_ref[d] = k * (1.0 + (iclr - 1.0) * ka_ref[...])
        bb_ref[d] = kk * iclr
    xg = z[:, RWKV_MAIN + 4 * LORA_PAD:RWKV_MAIN + 5 * LORA_PAD]
    gate_ref[...] = _dot(_sigmoid(xg).astype(BF16), g2_ref[...])
    bonus_ref[...] = _dot_f32(r * k * rk_ref[...], seg) * v


def rwkv_prep(zr, mu, w0, w2, a0, a2, g2, k_k, k_a, r_k, seg, seq, tm):
    n, cols = zr.shape
    nb8 = n // 8
    tb8 = tm // 8
    full = lambda shape: pl.BlockSpec(shape, lambda i: (0,) * len(shape))
    tok = pl.BlockSpec((tm, RWKV_WIDTH), lambda i: (i, 0))
    tok2 = pl.BlockSpec((2, tm, RWKV_WIDTH), lambda i: (0, i, 0))
    sds = jax.ShapeDtypeStruct((n, RWKV_WIDTH), F32)
    sds2 = jax.ShapeDtypeStruct((2, n, RWKV_WIDTH), F32)
    return pl.pallas_call(
        functools.partial(_prep_kernel, seq=seq, tm=tm),
        grid=(n // tm,),
        in_specs=[
            pl.BlockSpec((tm, cols), lambda i: (i, 0)),
            pl.BlockSpec((8, cols), lambda i: (jnp.maximum(i * tb8 - 1, 0), 0)),
            pl.BlockSpec((8, cols), lambda i: (jnp.minimum((i + 1) * tb8, nb8 - 1), 0)),
            full((2, cols)), full((2, RWKV_WIDTH)), full((2, LORA_PAD, RWKV_WIDTH)),
            full((2, RWKV_WIDTH)), full((2, LORA_PAD, RWKV_WIDTH)), full((LORA_PAD, RWKV_WIDTH)),
            full((1, RWKV_WIDTH)), full((1, RWKV_WIDTH)), full((1, RWKV_WIDTH)),
            full((RWKV_WIDTH, RWKV_WIDTH)),
        ],
        out_specs=[tok, tok, tok, tok2, tok2, tok2, tok, tok],
        out_shape=[sds, sds, sds, sds2, sds2, sds2, sds, sds],
        compiler_params=_cparams(("parallel",)),
        name="rwkv_prep",
    )(zr, zr, zr, mu, w0, w2, a0, a2, g2, k_k, k_a, r_k, seg)


def _scan_kernel(r_ref, v_ref, a_ref, lw_ref, k_ref, b_ref, tri_ref, strict_ref, incl_ref, lvl_ref, eye_ref, bd_ref,
                 y_ref, h_ref, *, tb):
    d = pl.program_id(1)
    i = pl.program_id(2)
    nc = tb // CHUNK
    ngrp = RWKV_WIDTH // GROUP_W

    @pl.when(i == 0)
    def _():
        h_ref[...] = jnp.zeros(h_ref.shape, F32)

    tri = tri_ref[...]
    strict = strict_ref[...] > 0.5
    incl = incl_ref[...] > 0.5
    eye = eye_ref[...]
    bdm = bd_ref[...] > 0.5
    ones16 = jnp.ones((16, GROUP_W), BF16)
    row16 = lax.broadcasted_iota(jnp.int32, (16, GROUP_W), 0)

    def bd(x):
        return jnp.where(bdm, jnp.concatenate([x] * GROUP, axis=0), 0.0).astype(BF16)

    def split(x):
        hi = x.astype(BF16)
        return hi, (x - hi.astype(F32)).astype(BF16)

    slices = []
    for cc in range(nc):
        ci = cc + d * (nc - 1 - 2 * cc)
        slices.append(pl.ds(pl.multiple_of(ci * CHUNK, CHUNK), CHUNK))
    groups = [slice(g * GROUP_W, (g + 1) * GROUP_W) for g in range(ngrp)]
    units = [dict(sl=sl, cols=cols) for sl in slices for cols in groups]

    for c in units:
        lw = lw_ref[c["sl"], c["cols"]]
        r = r_ref[c["sl"], c["cols"]]
        v = v_ref[c["sl"], c["cols"]]
        a = a_ref[c["sl"], c["cols"]]
        k = k_ref[c["sl"], c["cols"]]
        b = b_ref[c["sl"], c["cols"]]
        hi, lo = split(lw)
        cs = _dot(tri, jnp.concatenate([hi, lo], axis=1))
        lcum = cs[:, :GROUP_W] + cs[:, GROUP_W:]
        ltot = jnp.where(d == 0, lcum[CHUNK - 1:CHUNK], lcum[0:1])
        at = a * jnp.exp(lcum - lw)
        rt = r * jnp.exp(lcum)
        einv = jnp.exp(-lcum)
        eend = jnp.exp(ltot - lcum)
        c["x"] = jnp.concatenate([at, rt], axis=0).astype(BF16)
        c["ybk"] = jnp.concatenate([bd(b * einv), bd(k * einv)], axis=0)
        c["bkw"] = jnp.concatenate([b * eend, k * eend], axis=0).astype(BF16)
        c["vbd"] = bd(v)
        c["v"] = v.astype(BF16)
        wtot = jnp.exp(ltot)
        whi = wtot.astype(BF16).astype(F32)
        c["whl"] = jnp.where(row16 == 0, whi, jnp.where(row16 == 1, wtot - whi, 0.0)).astype(BF16)
    for c in units:
        amat = _dot_nt(c["x"], c.pop("ybk"))
        c["aab"] = jnp.where(strict, amat[:CHUNK, :GROUP_W], 0.0)
        c["aak"] = jnp.where(strict, amat[:CHUNK, GROUP_W:], 0.0).astype(BF16)
        c["ar"] = jnp.concatenate([jnp.where(incl, amat[CHUNK:, :GROUP_W], 0.0),
                                   jnp.where(incl, amat[CHUNK:, GROUP_W:], 0.0)], axis=1).astype(BF16)
        c["p"] = eye + jnp.where(lvl_ref[0] > 0.5, c["aab"], 0.0)
    for lvl in range(1, N_LEVELS):
        for c in units:
            e = jnp.where(lvl_ref[lvl] > 0.5, c["aab"], 0.0)
            c["g"] = _dot(c["p"].astype(BF16), bd(e)).astype(BF16)
        for c in units:
            c["p"] = c["p"] + _dot(c.pop("g"), bd(c["p"]))
    for c in units:
        c["p"] = c["p"].astype(BF16)
        c["av"] = _dot(c.pop("aak"), c["vbd"])
        c["wcol"] = _dot_tn(c.pop("whl"), ones16)

    for cc in range(nc):
        cu = units[cc * ngrp:(cc + 1) * ngrp]
        hs = [h_ref[g] for g in range(ngrp)]
        hx = [_dot(c["x"], h.astype(BF16)) for c, h in zip(cu, hs)]
        us = [_dot(c["p"], bd(x[:CHUNK] + c["av"])) for c, x in zip(cu, hx)]
        for g, (c, h, x, u) in enumerate(zip(cu, hs, hx, us)):
            y_ref[c["sl"], c["cols"]] = x[CHUNK:] + _dot(c["ar"], jnp.concatenate([bd(u), c["vbd"]], axis=0))
            upd = _dot_tn(c["bkw"], jnp.concatenate([u.astype(BF16), c["v"]], axis=0))
            h_ref[g] = c["wcol"] * h + jnp.where(bdm, upd, 0.0)


def rwkv_scan(r, v, a, lw, kd, bb, consts, batch, seq, tb):
    n = batch * seq
    nblk = seq // tb
    ngrp = RWKV_WIDTH // GROUP_W
    tri, strict, incl, lvl, eye, bdm = consts

    def tmap(b, d, i):
        return (b * nblk + i + d * (nblk - 1 - 2 * i), 0)

    def tmap_d(b, d, i):
        return (d, b * nblk + i + d * (nblk - 1 - 2 * i), 0)

    tok = pl.BlockSpec((tb, RWKV_WIDTH), tmap)
    tok_d = pl.BlockSpec((None, tb, RWKV_WIDTH), tmap_d)
    per_d = lambda shape: pl.BlockSpec((None,) + shape, lambda b, d, i: (d,) + (0,) * len(shape))
    full = lambda shape: pl.BlockSpec(shape, lambda b, d, i: (0, 0))
    return pl.pallas_call(
        functools.partial(_scan_kernel, tb=tb),
        grid=(batch, 2, nblk),
        in_specs=[tok, tok, tok, tok_d, tok_d, tok_d,
                  per_d((CHUNK, CHUNK)), per_d((CHUNK, GROUP * CHUNK)), per_d((CHUNK, GROUP * CHUNK)),
                  per_d((N_LEVELS, CHUNK, GROUP * CHUNK)),
                  full((CHUNK, GROUP * CHUNK)), full((GROUP_W, GROUP_W))],
        out_specs=tok_d,
        out_shape=jax.ShapeDtypeStruct((2, n, RWKV_WIDTH), F32),
        scratch_shapes=[pltpu.VMEM((ngrp, GROUP_W, GROUP_W), F32)],
        compiler_params=_cparams(("parallel", "parallel", "arbitrary")),
        name="rwkv_scan",
    )(r, v, a, lw, kd, bb, tri, strict, incl, lvl, eye, bdm)


def scan_constants():
    t = jnp.arange(CHUNK)
    before_eq = (t[None, :] <= t[:, None])
    tri = jnp.stack([before_eq, before_eq.T]).astype(F32)
    incl = jnp.tile(tri, (1, 1, GROUP))
    tri = tri.astype(BF16)
    eye = jnp.tile(jnp.eye(CHUNK, dtype=F32), (1, GROUP))
    strict = incl - eye[None]
    levels = []
    for lv in range(N_LEVELS):
        half = t // (1 << lv)
        fwd = ((half[:, None] // 2 == half[None, :] // 2) & (half[:, None] % 2 == 1) & (half[None, :] % 2 == 0))
        levels.append(jnp.stack([fwd, fwd.T]))
    lvl = jnp.tile(jnp.stack(levels, axis=1).astype(F32), (1, 1, 1, GROUP))
    blk = jnp.arange(GROUP_W) // RWKV_HEAD
    bdm = (blk[:, None] == blk[None, :]).astype(F32)
    return tri, strict, incl, lvl, eye, bdm


def _post_kernel(yf_ref, yb_ref, bonus_ref, gate_ref, oa_ref, x_ref, lw_ref, lb_ref, seg_ref,
                 wo_ref, g2_ref, wr_ref, xo_ref, h_ref, aff_ref):
    y = yf_ref[...] + yb_ref[...]
    seg = seg_ref[...]
    mean = _dot_f32(y, seg) * (1.0 / RWKV_HEAD)
    yc = y - mean
    var = _dot_f32(yc * yc, seg) * (1.0 / RWKV_HEAD)
    yn = yc * lax.rsqrt(var + GN_EPS) * lw_ref[...] + lb_ref[...]
    orw = ((yn + bonus_ref[...]) * gate_ref[...]).astype(BF16)
    x = (x_ref[...] + _dot(oa_ref[...], wo_ref[:ATTN_WIDTH, :]) + _dot(orw, wo_ref[ATTN_WIDTH:, :]))
    xo_ref[...] = x
    ms = jnp.mean(x * x, axis=-1, keepdims=True)
    h = x * lax.rsqrt(ms + NORM_EPS) * g2_ref[...]
    h_ref[...] = h.astype(BF16)
    logits = lax.dot_general(wr_ref[...], h, (((1,), (1,)), ((), ())),
                             preferred_element_type=F32, precision=HIGHEST)
    logits = logits - jnp.max(logits, axis=0, keepdims=True)
    e = jnp.exp(logits)
    aff_ref[...] = e / jnp.sum(e, axis=0, keepdims=True)


def post_mix(y2, bonus, gate, oa, x, lnx_w, lnx_b, seg, w_out, norm2_g, w_router_t, tm):
    n = x.shape[0]
    full = lambda shape: pl.BlockSpec(shape, lambda i: (0,) * len(shape))
    tok = lambda w: pl.BlockSpec((tm, w), lambda i: (i, 0))
    return pl.pallas_call(
        _post_kernel,
        grid=(n // tm,),
        in_specs=[
            pl.BlockSpec((None, tm, RWKV_WIDTH), lambda i: (0, i, 0)),
            pl.BlockSpec((None, tm, RWKV_WIDTH), lambda i: (1, i, 0)),
            tok(RWKV_WIDTH), tok(RWKV_WIDTH), tok(ATTN_WIDTH), tok(D_MODEL),
            full((1, RWKV_WIDTH)), full((1, RWKV_WIDTH)), full((RWKV_WIDTH, RWKV_WIDTH)),
            full((D_MODEL, D_MODEL)), full((1, D_MODEL)), full((N_EXPERTS, D_MODEL)),
        ],
        out_specs=[tok(D_MODEL), tok(D_MODEL), pl.BlockSpec((N_EXPERTS, tm), lambda i: (0, i))],
        out_shape=[
            jax.ShapeDtypeStruct((n, D_MODEL), F32),
            jax.ShapeDtypeStruct((n, D_MODEL), BF16),
            jax.ShapeDtypeStruct((N_EXPERTS, n), F32),
        ],
        compiler_params=_cparams(("parallel",)),
        name="post_mix",
    )(y2, y2, bonus, gate, oa, x, lnx_w, lnx_b, seg, w_out, norm2_g, w_router_t)


def _thresh_kernel(aff_ref, thr_ref, need_ref, *, cap):
    bits = pltpu.bitcast(aff_ref[...], jnp.int32)
    e = bits.shape[0]

    def body(it, thr):
        cand = thr | jnp.left_shift(jnp.int32(1), 30 - it)
        cnt = jnp.sum(jnp.where(bits >= cand, 1.0, 0.0), axis=1, keepdims=True)
        return jnp.where(cnt >= cap, cand, thr)

    thr = lax.fori_loop(0, 31, body, jnp.zeros((e, 1), jnp.int32))
    n_gt = jnp.sum(jnp.where(bits > thr, 1.0, 0.0), axis=1, keepdims=True)
    thr_ref[...] = jnp.broadcast_to(thr, thr_ref.shape)
    need_ref[...] = jnp.broadcast_to(cap - n_gt, need_ref.shape)


def _prefix_kernel(aff_ref, thr_ref, need_ref, u_ref, pos_ref, cend_ref, carry_ref):
    @pl.when(pl.program_id(0) == 0)
    def _():
        carry_ref[...] = jnp.zeros(carry_ref.shape, F32)

    bits = pltpu.bitcast(aff_ref[...], jnp.int32)
    e, tr = bits.shape
    thr = thr_ref[:, 0:1]
    need = need_ref[:, 0:1]
    gt = bits > thr
    eq = bits == thr
    marks = jnp.concatenate([jnp.where(gt, 1.0, 0.0), jnp.where(eq, 1.0, 0.0)], axis=0).astype(BF16)
    pre = _dot(marks, u_ref[...]) + carry_ref[...]
    pgt = pre[:e]
    peq = pre[e:]
    sel = jnp.where(gt, 1.0, jnp.where(eq, jnp.where(peq <= need, 1.0, 0.0), 0.0))
    rank = pgt + jnp.minimum(peq, need) - 1.0
    pos_ref[...] = jnp.where(sel > 0.5, rank, -1.0).astype(jnp.int32)
    carry_ref[...] = pre[:, tr - 1:tr]
    cend_ref[...] = jnp.broadcast_to(rank[:, tr - 1:tr] + 1.0, cend_ref.shape)


def select_tokens(aff_t, cap, tr):
    e, n = aff_t.shape
    nb = n // tr
    thr, need = pl.pallas_call(
        functools.partial(_thresh_kernel, cap=float(cap)),
        out_shape=[jax.ShapeDtypeStruct((e, 128), jnp.int32), jax.ShapeDtypeStruct((e, 128), F32)],
        compiler_params=pltpu.CompilerParams(vmem_limit_bytes=VMEM_LIMIT),
        name="route_threshold",
    )(aff_t)
    upper = (jnp.arange(tr)[:, None] <= jnp.arange(tr)[None, :]).astype(BF16)
    pos, cend = pl.pallas_call(
        _prefix_kernel,
        grid=(nb,),
        in_specs=[pl.BlockSpec((e, tr), lambda i: (0, i)),
                  pl.BlockSpec((e, 128), lambda i: (0, 0)),
                  pl.BlockSpec((e, 128), lambda i: (0, 0)),
                  pl.BlockSpec((tr, tr), lambda i: (0, 0))],
        out_specs=[pl.BlockSpec((e, tr), lambda i: (0, i)),
                   pl.BlockSpec((None, e, 128), lambda i: (i, 0, 0))],
        out_shape=[jax.ShapeDtypeStruct((e, n), jnp.int32), jax.ShapeDtypeStruct((nb, e, 128), F32)],
        scratch_shapes=[pltpu.VMEM((2 * e, 1), F32)],
        compiler_params=_cparams(("arbitrary",)),
        name="route_prefix",
    )(aff_t, thr, need, upper)
    return pos, cend[:, :, 0].T.astype(jnp.int32)


def routing_tables(cend, tr, n_ob):
    e, nb = cend.shape
    cstart = jnp.concatenate([jnp.zeros((e, 1), jnp.int32), cend[:, :-1]], axis=1)
    count = cend - cstart
    ob_first = jnp.minimum(cstart // tr, n_ob - 1)
    ob_last = jnp.where(count > 0, (cend - 1) // tr, ob_first)
    nitems = jnp.where(count > 0, ob_last - ob_first + 1, 0)
    cum = jnp.cumsum(nitems, axis=1)
    length = nb + n_ob
    w = jnp.arange(length, dtype=jnp.int32)
    tb = jax.vmap(lambda c: jnp.searchsorted(c, w, side="right"))(cum).astype(jnp.int32)
    tb = jnp.minimum(tb, nb - 1)
    before = jnp.take_along_axis(cum - nitems, tb, axis=1)
    ob = jnp.take_along_axis(ob_first, tb, axis=1) + (w[None, :] - before)
    valid = w[None, :] < cum[:, -1:]
    ob = jnp.where(valid, ob, n_ob - 1)
    tb = jnp.where(valid, tb, nb - 1)
    prev_ob = jnp.concatenate([jnp.full((e, 1), -1, jnp.int32), ob[:, :-1]], axis=1)
    next_ob = jnp.concatenate([ob[:, 1:], jnp.full((e, 1), -1, jnp.int32)], axis=1)
    next_valid = jnp.concatenate([valid[:, 1:], jnp.zeros((e, 1), bool)], axis=1)
    first = valid & (ob != prev_ob)
    last = valid & ((ob != next_ob) | ~next_valid)
    flags = first.astype(jnp.int32) + 2 * last.astype(jnp.int32) + 4 * valid.astype(jnp.int32)
    gather = (ob.reshape(-1), tb.reshape(-1), flags.reshape(-1))
    scatter = (ob_first.T.reshape(-1), (ob_last > ob_first).astype(jnp.int32).T.reshape(-1))
    return gather, scatter


def _gather_ffn_kernel(ob_ref, tb_ref, fl_ref, pos_ref, aff_ref, h_ref, wg_ref, wu_ref, wd_ref, o_ref,
                       x_acc, g_acc, *, length, tr):
    idx = pl.program_id(0) * length + pl.program_id(1)
    flags = fl_ref[idx]
    ob = ob_ref[idx]

    @pl.when((flags & 1) != 0)
    def _():
        x_acc[...] = jnp.zeros(x_acc.shape, F32)
        g_acc[...] = jnp.zeros(g_acc.shape, F32)

    @pl.when((flags & 4) != 0)
    def _():
        rank = ob * tr + lax.broadcasted_iota(jnp.int32, (tr, tr), 0)
        match = pos_ref[...] == rank
        x_acc[...] += _dot(jnp.where(match, 1.0, 0.0).astype(BF16), h_ref[...])
        g_acc[...] += jnp.sum(jnp.where(match, aff_ref[...], 0.0), axis=1, keepdims=True)

    @pl.when((flags & 2) != 0)
    def _():
        x = x_acc[...].astype(BF16)
        hg = _dot(x, wg_ref[...])
        hu = _dot(x, wu_ref[...])
        act = (hg * _sigmoid(hg) * hu).astype(BF16)
        o_ref[...] = (_dot(act, wd_ref[...]) * g_acc[...]).astype(BF16)


def gather_ffn(tables, pos3, aff3, h, wg, wu, wd, cap, tr):
    e = pos3.shape[0]
    n = h.shape[0]
    length = n // tr + cap // tr
    ob_of, tb_of, flags = tables
    tok = pl.BlockSpec((None, 1, tr), lambda e, w, ob, tb, fl: (e, 0, tb[e * length + w]))
    wspec = lambda r, c: pl.BlockSpec((None, r, c), lambda e, w, ob, tb, fl: (e, 0, 0))
    return pl.pallas_call(
        functools.partial(_gather_ffn_kernel, length=length, tr=tr),
        grid_spec=pltpu.PrefetchScalarGridSpec(
            num_scalar_prefetch=3,
            grid=(e, length),
            in_specs=[tok, tok,
                      pl.BlockSpec((tr, D_MODEL), lambda e, w, ob, tb, fl: (tb[e * length + w], 0)),
                      wspec(D_MODEL, EXPERT_FF), wspec(D_MODEL, EXPERT_FF), wspec(EXPERT_FF, D_MODEL)],
            out_specs=pl.BlockSpec((None, tr, D_MODEL), lambda e, w, ob, tb, fl: (e, ob[e * length + w], 0)),
            scratch_shapes=[pltpu.VMEM((tr, D_MODEL), F32), pltpu.VMEM((tr, 1), F32)],
        ),
        out_shape=jax.ShapeDtypeStruct((e, cap, D_MODEL), BF16),
        compiler_params=_cparams(("parallel", "arbitrary")),
        name="gather_ffn",
    )(ob_of, tb_of, flags, pos3, aff3, h, wg, wu, wd)


def _combine_kernel(obf_ref, span_ref, pos_ref, ye_ref, x_ref, g_ref, o_ref, *, tr, n_experts, final):
    t = pl.program_id(0)
    e = pl.program_id(1)
    s = pl.program_id(2)
    idx = t * n_experts + e

    @pl.when((e == 0) & (s == 0))
    def _():
        o_ref[...] = x_ref[...]

    @pl.when((s == 0) | (span_ref[idx] != 0))
    def _():
        rank = (obf_ref[idx] + s) * tr + lax.broadcasted_iota(jnp.int32, (tr, tr), 0)
        onehot = jnp.where(pos_ref[...] == rank, 1.0, 0.0).astype(BF16)
        o_ref[...] += _dot_tn(onehot, ye_ref[...])

    if final:
        @pl.when((e == n_experts - 1) & (s == 1))
        def _():
            x = o_ref[...]
            ms = jnp.mean(x * x, axis=-1, keepdims=True)
            o_ref[...] = x * lax.rsqrt(ms + NORM_EPS) * g_ref[...]


def combine(tables, pos3, ye, x, final_g, tr, final):
    e, cap, _ = ye.shape
    n = x.shape[0]
    nb = n // tr
    n_ob = cap // tr
    obf, span = tables

    def ye_map(t, e, s, obf, span):
        idx = t * N_EXPERTS + e
        return (e, jnp.minimum(obf[idx] + s * span[idx], n_ob - 1), 0)

    return pl.pallas_call(
        functools.partial(_combine_kernel, tr=tr, n_experts=e, final=final),
        grid_spec=pltpu.PrefetchScalarGridSpec(
            num_scalar_prefetch=2,
            grid=(nb, e, 2),
            in_specs=[pl.BlockSpec((None, 1, tr), lambda t, e, s, obf, span: (e, 0, t)),
                      pl.BlockSpec((None, tr, D_MODEL), ye_map),
                      pl.BlockSpec((tr, D_MODEL), lambda t, e, s, obf, span: (t, 0)),
                      pl.BlockSpec((1, D_MODEL), lambda t, e, s, obf, span: (0, 0))],
            out_specs=pl.BlockSpec((tr, D_MODEL), lambda t, e, s, obf, span: (t, 0)),
        ),
        out_shape=jax.ShapeDtypeStruct((n, D_MODEL), F32),
        compiler_params=_cparams(("parallel", "arbitrary", "arbitrary")),
        name="combine",
    )(obf, span, pos3, ye, x, final_g)


def expert_choice_ffn(x, h, aff_t, wg, wu, wd, final_g, tr, final):
    e, n = aff_t.shape
    cap = CAPACITY_FACTOR * n // e
    pos, cend = select_tokens(aff_t, cap, tr)
    gather_tables, scatter_tables = routing_tables(cend, tr, cap // tr)
    pos3 = pos.reshape(e, 1, n)
    ye = gather_ffn(gather_tables, pos3, aff_t.reshape(e, 1, n), h, wg, wu, wd, cap, tr)
    return combine(scatter_tables, pos3, ye, x, final_g, tr, final)


def _pad_cols(w, width):
    return jnp.pad(w, ((0, 0),) * (w.ndim - 1) + ((0, width - w.shape[-1]),))


def _pad_lora_cols(z):
    o = RWKV_MAIN
    parts = [z[..., :o]]
    for width in (DECAY_LORA, DECAY_LORA, AAA_LORA, AAA_LORA, GATE_LORA):
        parts.append(_pad_cols(z[..., o:o + width], LORA_PAD))
        o += width
    return jnp.concatenate(parts, axis=-1)


def _pad_rows(w, rows):
    return jnp.pad(w, ((0, 0),) * (w.ndim - 2) + ((0, rows - w.shape[-2]), (0, 0)))


def prepare_weights(norm1_g, w_in, lambda_qk, subln_g, shift_mu, w0, w2, a0, a2, g2, k_k, k_a, r_k,
                    lnx_w, lnx_b, w_out, norm2_g, w_router, w_gate, w_up, w_down, final_g):
    scale = ATTN_QK_DIM ** -0.5 * math.log2(math.e)
    w_attn = jnp.concatenate([w_in[..., :ATTN_WIDTH] * scale, w_in[..., ATTN_WIDTH:ATTN_COLS]], axis=-1)
    w_in_p = jnp.concatenate([w_attn, _pad_lora_cols(w_in[..., ATTN_COLS:])], axis=-1).astype(BF16)
    blk = jnp.arange(RWKV_WIDTH) // RWKV_HEAD
    return dict(
        norm1_g=norm1_g[:, None, :], w_in=w_in_p, lambda_qk=lambda_qk, subln_g=subln_g[:, None, :],
        mu=_pad_lora_cols(shift_mu), w0=w0, w2=_pad_rows(w2, LORA_PAD).astype(BF16),
        a0=a0, a2=_pad_rows(a2, LORA_PAD).astype(BF16), g2=_pad_rows(g2, LORA_PAD).astype(BF16),
        k_k=k_k[:, None, :], k_a=k_a[:, None, :], r_k=r_k.reshape(DEPTH, 1, RWKV_WIDTH),
        lnx_w=lnx_w[:, None, :], lnx_b=lnx_b[:, None, :], w_out=w_out.astype(BF16),
        norm2_g=norm2_g[:, None, :], w_router_t=jnp.swapaxes(w_router, 1, 2),
        w_gate=w_gate.astype(BF16), w_up=w_up.astype(BF16), w_down=w_down.astype(BF16),
        final_g=final_g[None, :],
        seg=(blk[:, None] == blk[None, :]).astype(F32),
        scan_consts=scan_constants(),
    )


def trunk(x, p, tiles):
    batch, seq, _ = x.shape
    n = batch * seq
    x = x.reshape(n, D_MODEL)
    attn_consts = attention_constants(tiles["tq"])
    for l in range(DEPTH):
        lambda_init = 0.8 - 0.6 * math.exp(-0.3 * l)
        za, zr = in_proj(x, p["norm1_g"][l], p["w_in"][l], tiles["tm"])
        oa = diff_attention(za, attn_consts, p["lambda_qk"][l], p["subln_g"][l], batch, seq, lambda_init,
                            tiles["tq"])
        r, v, a, lw, kd, bb, bonus, gate = rwkv_prep(
            zr, p["mu"][l], p["w0"][l], p["w2"][l], p["a0"][l], p["a2"][l], p["g2"][l],
            p["k_k"][l], p["k_a"][l], p["r_k"][l], p["seg"], seq, tiles["tm"])
        y2 = rwkv_scan(r, v, a, lw, kd, bb, p["scan_consts"], batch, seq, tiles["tb"])
        x, h, aff_t = post_mix(y2, bonus, gate, oa, x, p["lnx_w"][l], p["lnx_b"][l], p["seg"],
                               p["w_out"][l], p["norm2_g"][l], p["w_router_t"][l], tiles["tm"])
        x = expert_choice_ffn(x, h, aff_t, p["w_gate"][l], p["w_up"][l], p["w_down"][l], p["final_g"],
                              tiles["tr"], final=(l == DEPTH - 1))
    return x.reshape(batch, seq, D_MODEL)


def _tiles(seq):
    return dict(tm=min(512, seq), tq=min(512, seq), tb=min(256, seq), tr=512)


def kernel(x_prompt, x_sample, norm1_g, w_in, lambda_qk, subln_g, shift_mu, w0, w2, a0, a2, g2, k_k, k_a, r_k,
           lnx_w, lnx_b, w_out, norm2_g, w_router, w_gate, w_up, w_down, final_g):
    p = prepare_weights(norm1_g, w_in, lambda_qk, subln_g, shift_mu, w0, w2, a0, a2, g2, k_k, k_a, r_k,
                        lnx_w, lnx_b, w_out, norm2_g, w_router, w_gate, w_up, w_down, final_g)
    y_prompt = trunk(x_prompt, p, _tiles(x_prompt.shape[1]))
    y_sample = trunk(x_sample, p, _tiles(x_sample.shape[1]))
    return (y_prompt, y_sample)
```

```python
import functools
import math

import jax
import jax.numpy as jnp
from jax import lax
from jax.experimental import pallas as pl
from jax.experimental.pallas import tpu as pltpu

F32 = jnp.float32
BF16 = jnp.bfloat16
HIGHEST = lax.Precision.HIGHEST

D_MODEL = 1024
DEPTH = 4
ATTN_WIDTH = 512
ATTN_HEADS = 4
ATTN_QK_DIM = 64
ATTN_V_DIM = 128
RWKV_WIDTH = 512
RWKV_HEAD = 64
RWKV_HEADS = 8
DECAY_LORA = 32
AAA_LORA = 32
GATE_LORA = 96
LORA_PAD = 128
ATTN_COLS = 3 * ATTN_WIDTH
RWKV_MAIN = 3 * RWKV_WIDTH
RWKV_COLS_PAD = RWKV_MAIN + 5 * LORA_PAD
N_EXPERTS = 16
EXPERT_FF = 2 * D_MODEL
CAPACITY_FACTOR = 2
NORM_EPS = 1e-6
GN_EPS = 64e-5

CHUNK = 64
GROUP = 4
GROUP_W = GROUP * RWKV_HEAD
N_LEVELS = CHUNK.bit_length() - 1
VMEM_LIMIT = 56 * 1024 * 1024


def _cparams(sem):
    return pltpu.CompilerParams(dimension_semantics=sem, vmem_limit_bytes=VMEM_LIMIT)


def _dot(a, b):
    return jnp.dot(a, b, preferred_element_type=F32)


def _dot_nt(a, b):
    return lax.dot_general(a, b, (((1,), (1,)), ((), ())), preferred_element_type=F32)


def _dot_tn(a, b):
    return lax.dot_general(a, b, (((0,), (0,)), ((), ())), preferred_element_type=F32)


def _dot_f32(a, b):
    return jnp.dot(a, b, preferred_element_type=F32, precision=HIGHEST)


def _in_proj_kernel(x_ref, g_ref, w_ref, za_ref, zr_ref):
    x = x_ref[...]
    ms = jnp.mean(x * x, axis=-1, keepdims=True)
    h = (x * lax.rsqrt(ms + NORM_EPS) * g_ref[...]).astype(BF16)
    za = _dot(h, w_ref[:, :ATTN_COLS]).astype(BF16)
    for j in range(ATTN_COLS // ATTN_V_DIM):
        za_ref[j] = za[:, j * ATTN_V_DIM:(j + 1) * ATTN_V_DIM]
    zr_ref[...] = _dot(h, w_ref[:, ATTN_COLS:])


def in_proj(x, g, w, tm):
    n = x.shape[0]
    cols = w.shape[1]
    return pl.pallas_call(
        _in_proj_kernel,
        grid=(n // tm,),
        in_specs=[
            pl.BlockSpec((tm, D_MODEL), lambda i: (i, 0)),
            pl.BlockSpec((1, D_MODEL), lambda i: (0, 0)),
            pl.BlockSpec((D_MODEL, cols), lambda i: (0, 0)),
        ],
        out_specs=[
            pl.BlockSpec((ATTN_COLS // ATTN_V_DIM, tm, ATTN_V_DIM), lambda i: (0, i, 0)),
            pl.BlockSpec((tm, RWKV_COLS_PAD), lambda i: (i, 0)),
        ],
        out_shape=[
            jax.ShapeDtypeStruct((ATTN_COLS // ATTN_V_DIM, n, ATTN_V_DIM), BF16),
            jax.ShapeDtypeStruct((n, RWKV_COLS_PAD), F32),
        ],
        compiler_params=_cparams(("parallel",)),
        name="in_proj",
    )(x, g, w)


ROW_BLOCK = 64
POS_BITS = 9
M_INIT = -1e30


def _attn_kernel(kap_ref, lq_ref, q_ref, k_ref, v_ref, g_ref, qx_ref, kx_ref, dbias_ref, o_ref,
                 q2_ref, s_ref, p_ref, m_ref, al_ref, sh_ref, acc_ref, *, seq, tq, lambda_init):
    h = pl.program_id(1)
    i = pl.program_id(2)
    nk = seq // tq
    kap = kap_ref[h]
    q = q_ref[...]
    lane = lax.broadcasted_iota(jnp.int32, q.shape, 1)
    zero = jnp.zeros_like(q)
    q2 = jnp.concatenate([jnp.where(lane < ATTN_QK_DIM, q, zero),
                          jnp.where(lane >= ATTN_QK_DIM, q, zero)], axis=0)
    qx = qx_ref[...]
    qx2 = jnp.concatenate([qx, qx], axis=0)
    q2_ref[0] = jnp.concatenate([q2, qx2], axis=1)
    q2_ref[1] = jnp.concatenate([q2, -qx2], axis=1)
    kx = kx_ref[...]
    ones = jnp.ones((tq, ATTN_V_DIM), BF16)
    m_ref[...] = jnp.full(m_ref.shape, M_INIT, F32)
    acc_ref[...] = jnp.zeros(acc_ref.shape, F32)

    def tile_of(step):
        jj = step - 1
        return jnp.where(step == 0, i, jj + (jj >= i).astype(jnp.int32))

    def scores(step, buf):
        j = tile_of(step)
        start = pl.multiple_of(j * tq, tq)
        k_aug = jnp.concatenate([k_ref[pl.ds(start, tq), :], kx], axis=1)
        s_ref[buf] = _dot_nt(q2_ref[(j > i).astype(jnp.int32)], k_aug)

    def softmax_and_values(step, buf):
        j = tile_of(step)
        kappa = -kap * jnp.abs(i - j).astype(F32)
        start = pl.multiple_of(j * tq, tq)
        v_aug = jnp.concatenate([v_ref[pl.ds(start, tq), :], ones], axis=1)
        for rb in range(2 * tq // ROW_BLOCK):
            rows = slice(rb * ROW_BLOCK, (rb + 1) * ROW_BLOCK)
            m_old = m_ref[rows, :]
            m_new = jnp.maximum(m_old, jnp.max(s_ref[buf, rows, :], axis=-1, keepdims=True) + kappa)
            al_ref[rows, :] = jnp.exp2(m_old - m_new)
            sh_ref[rows, :] = m_new - kappa
            m_ref[rows, :] = m_new
        for rb in range(2 * tq // ROW_BLOCK):
            rows = slice(rb * ROW_BLOCK, (rb + 1) * ROW_BLOCK)
            p_ref[rows, :] = jnp.exp2(s_ref[buf, rows, :] - sh_ref[rows, :]).astype(BF16)
        acc_ref[...] = al_ref[...] * acc_ref[...] + _dot(p_ref[...], v_aug)

    dstart = pl.multiple_of(i * tq, tq)
    s_diag = _dot_nt(q2, k_ref[pl.ds(dstart, tq), :])
    s_ref[0] = (s_diag.reshape(2, tq, tq) + dbias_ref[...][None]).reshape(2 * tq, tq)

    def pair(t, carry):
        scores(2 * t + 1, 1)
        softmax_and_values(2 * t, 0)
        scores(2 * t + 2, 0)
        softmax_and_values(2 * t + 1, 1)
        return carry

    lax.fori_loop(0, nk // 2 - 1, pair, 0)
    scores(nk - 1, 1)
    softmax_and_values(nk - 2, 0)
    softmax_and_values(nk - 1, 1)

    lq = lq_ref[...]
    lam = (jnp.exp(jnp.sum(lq[0:1] * lq[1:2], axis=-1, keepdims=True))
           - jnp.exp(jnp.sum(lq[2:3] * lq[3:4], axis=-1, keepdims=True)) + lambda_init)
    acc = acc_ref[...]
    o_all = acc[:, :ATTN_V_DIM] / acc[:, ATTN_V_DIM:]
    o = o_all[:tq] - lam * o_all[tq:]
    ms = jnp.mean(o * o, axis=-1, keepdims=True)
    o = o * lax.rsqrt(ms + NORM_EPS) * g_ref[...] * (1.0 - lambda_init)
    o_ref[...] = o.astype(BF16)


def attention_constants(tq):
    assert tq <= (1 << POS_BITS)
    slopes = 2.0 ** (-8.0 * jnp.arange(1, ATTN_HEADS + 1, dtype=F32) / ATTN_HEADS)
    c = slopes * math.log2(math.e)
    cb = c[:, None] * (2.0 ** jnp.arange(POS_BITS, dtype=F32))[None, :]
    hi = cb.astype(BF16).astype(F32)
    lo = (cb - hi).astype(BF16).astype(F32)
    pos = jnp.arange(tq)
    bits = ((pos[:, None] >> jnp.arange(POS_BITS)[None, :]) & 1).astype(F32)
    hb = jnp.broadcast_to(bits[None], (ATTN_HEADS, tq, POS_BITS))
    const = lambda x: jnp.broadcast_to(x[:, None, :], (ATTN_HEADS, tq, POS_BITS))
    pad = jnp.zeros((ATTN_HEADS, tq, ATTN_V_DIM - 4 * POS_BITS), F32)
    qx = jnp.concatenate([const(hi), const(lo), hb, hb, pad], axis=-1).astype(BF16)
    kx = jnp.concatenate([hb, hb, const(-hi), const(-lo), pad], axis=-1).astype(BF16)
    dist = jnp.abs(pos[:, None] - pos[None, :]).astype(F32)
    dbias = -c[:, None, None] * dist[None]
    kap = c * tq
    return kap, qx, kx, dbias


def diff_attention(za, consts, lq, subln_g, batch, seq, lambda_init, tq):
    n = batch * seq
    nq = seq // tq
    kap, qx, kx, dbias = consts
    assert nq % 2 == 0, "key tiles are processed in pairs"
    kern = functools.partial(_attn_kernel, seq=seq, tq=tq, lambda_init=lambda_init)
    per_head = lambda w: pl.BlockSpec((None, tq, w), lambda b, h, i: (h, 0, 0))
    return pl.pallas_call(
        kern,
        grid=(batch, ATTN_HEADS, nq),
        in_specs=[
            pl.BlockSpec(memory_space=pltpu.SMEM),
            pl.BlockSpec((4, ATTN_QK_DIM), lambda b, h, i: (0, 0)),
            pl.BlockSpec((None, tq, ATTN_V_DIM), lambda b, h, i: (h, b * nq + i, 0)),
            pl.BlockSpec((None, seq, ATTN_V_DIM), lambda b, h, i: (ATTN_HEADS + h, b, 0)),
            pl.BlockSpec((None, seq, ATTN_V_DIM), lambda b, h, i: (2 * ATTN_HEADS + h, b, 0)),
            pl.BlockSpec((1, ATTN_V_DIM), lambda b, h, i: (0, 0)),
            per_head(ATTN_V_DIM), per_head(ATTN_V_DIM), per_head(tq),
        ],
        out_specs=pl.BlockSpec((None, tq, ATTN_V_DIM), lambda b, h, i: (h, b * nq + i, 0)),
        out_shape=jax.ShapeDtypeStruct((ATTN_HEADS, n, ATTN_V_DIM), BF16),
        scratch_shapes=[
            pltpu.VMEM((2, 2 * tq, 2 * ATTN_V_DIM), BF16),
            pltpu.VMEM((2, 2 * tq, tq), F32),
            pltpu.VMEM((2 * tq, tq), BF16),
            pltpu.VMEM((2 * tq, 1), F32),
            pltpu.VMEM((2 * tq, 1), F32),
            pltpu.VMEM((2 * tq, 1), F32),
            pltpu.VMEM((2 * tq, 2 * ATTN_V_DIM), F32),
        ],
        compiler_params=_cparams(("parallel", "parallel", "parallel")),
        name="diff_attention",
    )(kap, lq, za, za, za, subln_g, qx, kx, dbias)


def _softplus(x):
    return jnp.maximum(x, 0.0) + jnp.log(1.0 + jnp.exp(-jnp.abs(x)))


def _sigmoid(x):
    return 1.0 / (1.0 + jnp.exp(-x))


def _prep_kernel(z_ref, zp_ref, zn_ref, mu_ref, w0_ref, w2_ref, a0_ref, a2_ref, g2_ref,
                 kk_ref, ka_ref, rk_ref, seg_ref,
                 r_ref, v_ref, a_ref, lw_ref, kd_ref, bb_ref, bonus_ref, gate_ref, *, seq, tm):
    i = pl.program_id(0)
    z = z_ref[...]
    row = lax.broadcasted_iota(jnp.int32, z.shape, 0)
    first = (i * tm) % seq == 0
    last = ((i + 1) * tm) % seq == 0
    zp = jnp.where(first, 0.0, zp_ref[7:8, :])
    zn = jnp.where(last, 0.0, zn_ref[0:1, :])
    prev = jnp.where(row == 0, zp, pltpu.roll(z, 1, 0))
    nxt = jnp.where(row == tm - 1, zn, pltpu.roll(z, tm - 1, 0))
    z = z + mu_ref[0:1, :] * (prev - z) + mu_ref[1:2, :] * (nxt - z)

    w = RWKV_WIDTH
    r = z[:, 0:w]
    k = z[:, w:2 * w]
    v = z[:, 2 * w:3 * w]
    seg = seg_ref[...]
    kk = k * kk_ref[...]
    ss = _dot_f32(kk * kk, seg)
    kk = kk / jnp.maximum(jnp.sqrt(ss), 1e-12)
    r_ref[...] = r
    v_ref[...] = v
    a_ref[...] = -kk
    for d in range(2):
        xw = z[:, RWKV_MAIN + d * LORA_PAD:RWKV_MAIN + (d + 1) * LORA_PAD]
        xa = z[:, RWKV_MAIN + (2 + d) * LORA_PAD:RWKV_MAIN + (3 + d) * LORA_PAD]
        wlog = -_softplus(-(w0_ref[d:d + 1, :] + _dot(jnp.tanh(xw).astype(BF16), w2_ref[d]))) - 0.5
        lw_ref[d] = -jnp.exp(wlog)
        iclr = _sigmoid(a0_ref[d:d + 1, :] + _dot(xa.astype(BF16), a2_ref[d]))
        kd_ref[d] = k * (1.0 + (iclr - 1.0) * ka_ref[...])
        bb_ref[d] = kk * iclr
    xg = z[:, RWKV_MAIN + 4 * LORA_PAD:RWKV_MAIN + 5 * LORA_PAD]
    gate_ref[...] = _dot(_sigmoid(xg).astype(BF16), g2_ref[...])
    bonus_ref[...] = _dot_f32(r * k * rk_ref[...], seg) * v


def rwkv_prep(zr, mu, w0, w2, a0, a2, g2, k_k, k_a, r_k, seg, seq, tm):
    n, cols = zr.shape
    nb8 = n // 8
    tb8 = tm // 8
    full = lambda shape: pl.BlockSpec(shape, lambda i: (0,) * len(shape))
    tok = pl.BlockSpec((tm, RWKV_WIDTH), lambda i: (i, 0))
    tok2 = pl.BlockSpec((2, tm, RWKV_WIDTH), lambda i: (0, i, 0))
    sds = jax.ShapeDtypeStruct((n, RWKV_WIDTH), F32)
    sds2 = jax.ShapeDtypeStruct((2, n, RWKV_WIDTH), F32)
    return pl.pallas_call(
        functools.partial(_prep_kernel, seq=seq, tm=tm),
        grid=(n // tm,),
        in_specs=[
            pl.BlockSpec((tm, cols), lambda i: (i, 0)),
            pl.BlockSpec((8, cols), lambda i: (jnp.maximum(i * tb8 - 1, 0), 0)),
            pl.BlockSpec((8, cols), lambda i: (jnp.minimum((i + 1) * tb8, nb8 - 1), 0)),
            full((2, cols)), full((2, RWKV_WIDTH)), full((2, LORA_PAD, RWKV_WIDTH)),
            full((2, RWKV_WIDTH)), full((2, LORA_PAD, RWKV_WIDTH)), full((LORA_PAD, RWKV_WIDTH)),
            full((1, RWKV_WIDTH)), full((1, RWKV_WIDTH)), full((1, RWKV_WIDTH)),
            full((RWKV_WIDTH, RWKV_WIDTH)),
        ],
        out_specs=[tok, tok, tok, tok2, tok2, tok2, tok, tok],
        out_shape=[sds, sds, sds, sds2, sds2, sds2, sds, sds],
        compiler_params=_cparams(("parallel",)),
        name="rwkv_prep",
    )(zr, zr, zr, mu, w0, w2, a0, a2, g2, k_k, k_a, r_k, seg)


def _scan_kernel(r_ref, v_ref, a_ref, lw_ref, k_ref, b_ref, tri_ref, strict_ref, incl_ref, lvl_ref, eye_ref, bd_ref,
                 y_ref, h_ref, *, tb):
    d = pl.program_id(1)
    i = pl.program_id(2)
    nc = tb // CHUNK
    ngrp = RWKV_WIDTH // GROUP_W

    @pl.when(i == 0)
    def _():
        h_ref[...] = jnp.zeros(h_ref.shape, F32)

    tri = tri_ref[...]
    strict = strict_ref[...] > 0.5
    incl = incl_ref[...] > 0.5
    eye = eye_ref[...]
    bdm = bd_ref[...] > 0.5
    ones16 = jnp.ones((16, GROUP_W), BF16)
    row16 = lax.broadcasted_iota(jnp.int32, (16, GROUP_W), 0)

    def bd(x):
        return jnp.where(bdm, jnp.concatenate([x] * GROUP, axis=0), 0.0).astype(BF16)

    def split(x):
        hi = x.astype(BF16)
        return hi, (x - hi.astype(F32)).astype(BF16)

    slices = []
    for cc in range(nc):
        ci = cc + d * (nc - 1 - 2 * cc)
        slices.append(pl.ds(pl.multiple_of(ci * CHUNK, CHUNK), CHUNK))
    groups = [slice(g * GROUP_W, (g + 1) * GROUP_W) for g in range(ngrp)]
    units = [dict(sl=sl, cols=cols) for sl in slices for cols in groups]

    for c in units:
        lw = lw_ref[c["sl"], c["cols"]]
        r = r_ref[c["sl"], c["cols"]]
        v = v_ref[c["sl"], c["cols"]]
        a = a_ref[c["sl"], c["cols"]]
        k = k_ref[c["sl"], c["cols"]]
        b = b_ref[c["sl"], c["cols"]]
        hi, lo = split(lw)
        cs = _dot(tri, jnp.concatenate([hi, lo], axis=1))
        lcum = cs[:, :GROUP_W] + cs[:, GROUP_W:]
        ltot = jnp.where(d == 0, lcum[CHUNK - 1:CHUNK], lcum[0:1])
        at = a * jnp.exp(lcum - lw)
        rt = r * jnp.exp(lcum)
        einv = jnp.exp(-lcum)
        eend = jnp.exp(ltot - lcum)
        c["x"] = jnp.concatenate([at, rt], axis=0).astype(BF16)
        c["ybk"] = jnp.concatenate([bd(b * einv), bd(k * einv)], axis=0)
        c["bkw"] = jnp.concatenate([b * eend, k * eend], axis=0).astype(BF16)
        c["vbd"] = bd(v)
        c["v"] = v.astype(BF16)
        wtot = jnp.exp(ltot)
        whi = wtot.astype(BF16).astype(F32)
        c["whl"] = jnp.where(row16 == 0, whi, jnp.where(row16 == 1, wtot - whi, 0.0)).astype(BF16)
    for c in units:
        amat = _dot_nt(c["x"], c.pop("ybk"))
        c["aab"] = jnp.where(strict, amat[:CHUNK, :GROUP_W], 0.0)
        c["aak"] = jnp.where(strict, amat[:CHUNK, GROUP_W:], 0.0).astype(BF16)
        c["ar"] = jnp.concatenate([jnp.where(incl, amat[CHUNK:, :GROUP_W], 0.0),
                                   jnp.where(incl, amat[CHUNK:, GROUP_W:], 0.0)], axis=1).astype(BF16)
        c["p"] = eye + jnp.where(lvl_ref[0] > 0.5, c["aab"], 0.0)
    for lvl in range(1, N_LEVELS):
        for c in units:
            e = jnp.where(lvl_ref[lvl] > 0.5, c["aab"], 0.0)
            c["g"] = _dot(c["p"].astype(BF16), bd(e)).astype(BF16)
        for c in units:
            c["p"] = c["p"] + _dot(c.pop("g"), bd(c["p"]))
    for c in units:
        c["p"] = c["p"].astype(BF16)
        c["av"] = _dot(c.pop("aak"), c["vbd"])
        c["wcol"] = _dot_tn(c.pop("whl"), ones16)

    for cc in range(nc):
        cu = units[cc * ngrp:(cc + 1) * ngrp]
        hs = [h_ref[g] for g in range(ngrp)]
        hx = [_dot(c["x"], h.astype(BF16)) for c, h in zip(cu, hs)]
        us = [_dot(c["p"], bd(x[:CHUNK] + c["av"])) for c, x in zip(cu, hx)]
        for g, (c, h, x, u) in enumerate(zip(cu, hs, hx, us)):
            y_ref[c["sl"], c["cols"]] = x[CHUNK:] + _dot(c["ar"], jnp.concatenate([bd(u), c["vbd"]], axis=0))
            upd = _dot_tn(c["bkw"], jnp.concatenate([u.astype(BF16), c["v"]], axis=0))
            h_ref[g] = c["wcol"] * h + jnp.where(bdm, upd, 0.0)


def rwkv_scan(r, v, a, lw, kd, bb, consts, batch, seq, tb):
    n = batch * seq
    nblk = seq // tb
    ngrp = RWKV_WIDTH // GROUP_W
    tri, strict, incl, lvl, eye, bdm = consts

    def tmap(b, d, i):
        return (b * nblk + i + d * (nblk - 1 - 2 * i), 0)

    def tmap_d(b, d, i):
        return (d, b * nblk + i + d * (nblk - 1 - 2 * i), 0)

    tok = pl.BlockSpec((tb, RWKV_WIDTH), tmap)
    tok_d = pl.BlockSpec((None, tb, RWKV_WIDTH), tmap_d)
    per_d = lambda shape: pl.BlockSpec((None,) + shape, lambda b, d, i: (d,) + (0,) * len(shape))
    full = lambda shape: pl.BlockSpec(shape, lambda b, d, i: (0, 0))
    return pl.pallas_call(
        functools.partial(_scan_kernel, tb=tb),
        grid=(batch, 2, nblk),
        in_specs=[tok, tok, tok, tok_d, tok_d, tok_d,
                  per_d((CHUNK, CHUNK)), per_d((CHUNK, GROUP * CHUNK)), per_d((CHUNK, GROUP * CHUNK)),
                  per_d((N_LEVELS, CHUNK, GROUP * CHUNK)),
                  full((CHUNK, GROUP * CHUNK)), full((GROUP_W, GROUP_W))],
        out_specs=tok_d,
        out_shape=jax.ShapeDtypeStruct((2, n, RWKV_WIDTH), F32),
        scratch_shapes=[pltpu.VMEM((ngrp, GROUP_W, GROUP_W), F32)],
        compiler_params=_cparams(("parallel", "parallel", "arbitrary")),
        name="rwkv_scan",
    )(r, v, a, lw, kd, bb, tri, strict, incl, lvl, eye, bdm)


def scan_constants():
    t = jnp.arange(CHUNK)
    before_eq = (t[None, :] <= t[:, None])
    tri = jnp.stack([before_eq, before_eq.T]).astype(F32)
    incl = jnp.tile(tri, (1, 1, GROUP))
    tri = tri.astype(BF16)
    eye = jnp.tile(jnp.eye(CHUNK, dtype=F32), (1, GROUP))
    strict = incl - eye[None]
    levels = []
    for lv in range(N_LEVELS):
        half = t // (1 << lv)
        fwd = ((half[:, None] // 2 == half[None, :] // 2) & (half[:, None] % 2 == 1) & (half[None, :] % 2 == 0))
        levels.append(jnp.stack([fwd, fwd.T]))
    lvl = jnp.tile(jnp.stack(levels, axis=1).astype(F32), (1, 1, 1, GROUP))
    blk = jnp.arange(GROUP_W) // RWKV_HEAD
    bdm = (blk[:, None] == blk[None, :]).astype(F32)
    return tri, strict, incl, lvl, eye, bdm


def _post_kernel(yf_ref, yb_ref, bonus_ref, gate_ref, oa_ref, x_ref, lw_ref, lb_ref, seg_ref,
                 wo_ref, g2_ref, wr_ref, xo_ref, h_ref, aff_ref):
    y = yf_ref[...] + yb_ref[...]
    seg = seg_ref[...]
    mean = _dot_f32(y, seg) * (1.0 / RWKV_HEAD)
    yc = y - mean
    var = _dot_f32(yc * yc, seg) * (1.0 / RWKV_HEAD)
    yn = yc * lax.rsqrt(var + GN_EPS) * lw_ref[...] + lb_ref[...]
    orw = ((yn + bonus_ref[...]) * gate_ref[...]).astype(BF16)
    oa = jnp.concatenate([oa_ref[j] for j in range(ATTN_HEADS)], axis=1)
    x = (x_ref[...] + _dot(oa, wo_ref[:ATTN_WIDTH, :]) + _dot(orw, wo_ref[ATTN_WIDTH:, :]))
    xo_ref[...] = x
    ms = jnp.mean(x * x, axis=-1, keepdims=True)
    h = x * lax.rsqrt(ms + NORM_EPS) * g2_ref[...]
    h_ref[...] = h.astype(BF16)
    logits = lax.dot_general(wr_ref[...], h, (((1,), (1,)), ((), ())),
                             preferred_element_type=F32, precision=HIGHEST)
    logits = logits - jnp.max(logits, axis=0, keepdims=True)
    e = jnp.exp(logits)
    aff_ref[...] = e / jnp.sum(e, axis=0, keepdims=True)


def post_mix(y2, bonus, gate, oa, x, lnx_w, lnx_b, seg, w_out, norm2_g, w_router_t, tm):
    n = x.shape[0]
    full = lambda shape: pl.BlockSpec(shape, lambda i: (0,) * len(shape))
    tok = lambda w: pl.BlockSpec((tm, w), lambda i: (i, 0))
    return pl.pallas_call(
        _post_kernel,
        grid=(n // tm,),
        in_specs=[
            pl.BlockSpec((None, tm, RWKV_WIDTH), lambda i: (0, i, 0)),
            pl.BlockSpec((None, tm, RWKV_WIDTH), lambda i: (1, i, 0)),
            tok(RWKV_WIDTH), tok(RWKV_WIDTH),
            pl.BlockSpec((ATTN_HEADS, tm, ATTN_V_DIM), lambda i: (0, i, 0)), tok(D_MODEL),
            full((1, RWKV_WIDTH)), full((1, RWKV_WIDTH)), full((RWKV_WIDTH, RWKV_WIDTH)),
            full((D_MODEL, D_MODEL)), full((1, D_MODEL)), full((N_EXPERTS, D_MODEL)),
        ],
        out_specs=[tok(D_MODEL), tok(D_MODEL), pl.BlockSpec((N_EXPERTS, tm), lambda i: (0, i))],
        out_shape=[
            jax.ShapeDtypeStruct((n, D_MODEL), F32),
            jax.ShapeDtypeStruct((n, D_MODEL), BF16),
            jax.ShapeDtypeStruct((N_EXPERTS, n), F32),
        ],
        compiler_params=_cparams(("parallel",)),
        name="post_mix",
    )(y2, y2, bonus, gate, oa, x, lnx_w, lnx_b, seg, w_out, norm2_g, w_router_t)


COMBINE_RANKS = 256


def _thresh_kernel(aff_ref, thr_ref, need_ref, *, cap):
    bits = pltpu.bitcast(aff_ref[...], jnp.int32)
    e = bits.shape[0]

    def body(it, thr):
        cand = thr | jnp.left_shift(jnp.int32(1), 30 - it)
        cnt = jnp.sum(jnp.where(bits >= cand, 1.0, 0.0), axis=1, keepdims=True)
        return jnp.where(cnt >= cap, cand, thr)

    thr = lax.fori_loop(0, 31, body, jnp.zeros((e, 1), jnp.int32))
    n_gt = jnp.sum(jnp.where(bits > thr, 1.0, 0.0), axis=1, keepdims=True)
    thr_ref[...] = jnp.broadcast_to(thr, thr_ref.shape)
    need_ref[...] = jnp.broadcast_to(cap - n_gt, need_ref.shape)


def _prefix_kernel(aff_ref, thr_ref, need_ref, u_ref, pos_ref, cend_ref, carry_ref):
    @pl.when(pl.program_id(0) == 0)
    def _():
        carry_ref[...] = jnp.zeros(carry_ref.shape, F32)

    bits = pltpu.bitcast(aff_ref[...], jnp.int32)
    e, tr = bits.shape
    thr = thr_ref[:, 0:1]
    need = need_ref[:, 0:1]
    gt = bits > thr
    eq = bits == thr
    marks = jnp.concatenate([jnp.where(gt, 1.0, 0.0), jnp.where(eq, 1.0, 0.0)], axis=0).astype(BF16)
    pre = _dot(marks, u_ref[...]) + carry_ref[...]
    pgt = pre[:e]
    peq = pre[e:]
    sel = jnp.where(gt, 1.0, jnp.where(eq, jnp.where(peq <= need, 1.0, 0.0), 0.0))
    rank = pgt + jnp.minimum(peq, need) - 1.0
    pos_ref[...] = jnp.where(sel > 0.5, rank, -1.0).astype(jnp.int32)
    carry_ref[...] = pre[:, tr - 1:tr]
    cend_ref[...] = jnp.broadcast_to(rank[:, tr - 1:tr] + 1.0, cend_ref.shape)


def select_tokens(aff_t, cap, tr):
    e, n = aff_t.shape
    nb = n // tr
    thr, need = pl.pallas_call(
        functools.partial(_thresh_kernel, cap=float(cap)),
        out_shape=[jax.ShapeDtypeStruct((e, 128), jnp.int32), jax.ShapeDtypeStruct((e, 128), F32)],
        compiler_params=pltpu.CompilerParams(vmem_limit_bytes=VMEM_LIMIT),
        name="route_threshold",
    )(aff_t)
    upper = (jnp.arange(tr)[:, None] <= jnp.arange(tr)[None, :]).astype(BF16)
    pos, cend = pl.pallas_call(
        _prefix_kernel,
        grid=(nb,),
        in_specs=[pl.BlockSpec((e, tr), lambda i: (0, i)),
                  pl.BlockSpec((e, 128), lambda i: (0, 0)),
                  pl.BlockSpec((e, 128), lambda i: (0, 0)),
                  pl.BlockSpec((tr, tr), lambda i: (0, 0))],
        out_specs=[pl.BlockSpec((e, tr), lambda i: (0, i)),
                   pl.BlockSpec((None, e, 128), lambda i: (i, 0, 0))],
        out_shape=[jax.ShapeDtypeStruct((e, n), jnp.int32), jax.ShapeDtypeStruct((nb, e, 128), F32)],
        scratch_shapes=[pltpu.VMEM((2 * e, 1), F32)],
        compiler_params=_cparams(("arbitrary",)),
        name="route_prefix",
    )(aff_t, thr, need, upper)
    return pos, cend[:, :, 0].T.astype(jnp.int32)


def routing_tables(cend, tr, n_ob):
    e, nb = cend.shape
    cstart = jnp.concatenate([jnp.zeros((e, 1), jnp.int32), cend[:, :-1]], axis=1)
    count = cend - cstart
    ob_first = jnp.minimum(cstart // tr, n_ob - 1)
    ob_last = jnp.where(count > 0, (cend - 1) // tr, ob_first)
    nitems = jnp.where(count > 0, ob_last - ob_first + 1, 0)
    cum = jnp.cumsum(nitems, axis=1)
    length = nb + n_ob
    w = jnp.arange(length, dtype=jnp.int32)
    tb = jax.vmap(lambda c: jnp.searchsorted(c, w, side="right"))(cum).astype(jnp.int32)
    tb = jnp.minimum(tb, nb - 1)
    before = jnp.take_along_axis(cum - nitems, tb, axis=1)
    ob = jnp.take_along_axis(ob_first, tb, axis=1) + (w[None, :] - before)
    valid = w[None, :] < cum[:, -1:]
    ob = jnp.where(valid, ob, n_ob - 1)
    tb = jnp.where(valid, tb, nb - 1)
    prev_ob = jnp.concatenate([jnp.full((e, 1), -1, jnp.int32), ob[:, :-1]], axis=1)
    next_ob = jnp.concatenate([ob[:, 1:], jnp.full((e, 1), -1, jnp.int32)], axis=1)
    next_valid = jnp.concatenate([valid[:, 1:], jnp.zeros((e, 1), bool)], axis=1)
    first = valid & (ob != prev_ob)
    last = valid & ((ob != next_ob) | ~next_valid)
    flags = first.astype(jnp.int32) + 2 * last.astype(jnp.int32) + 4 * valid.astype(jnp.int32)
    return ob.reshape(-1), tb.reshape(-1), flags.reshape(-1)


def combine_table(cend, rb, n_rb):
    e, nb = cend.shape
    cstart = jnp.concatenate([jnp.zeros((e, 1), jnp.int32), cend[:, :-1]], axis=1)
    ob_first = jnp.minimum(cstart // rb, n_rb - 1)
    ob_last = jnp.where(cend > cstart, (cend - 1) // rb, ob_first)
    obf = ob_first.T.reshape(-1)
    nitems = (ob_last - ob_first + 1).T.reshape(-1)
    cum = jnp.cumsum(nitems)
    start = cum - nitems
    length = nb * e + e * n_rb
    w = jnp.arange(length, dtype=jnp.int32)
    valid = w < cum[-1]
    pair = jnp.minimum(jnp.searchsorted(cum, w, side="right").astype(jnp.int32), nb * e - 1)
    ob = jnp.where(valid, obf[pair] + (w - start[pair]), obf[-1] + nitems[-1] - 1)
    ex = pair % e
    first = valid & (w == start[pair]) & (ex == 0)
    last = valid & (w == cum[pair] - 1) & (ex == e - 1)
    flags = first.astype(jnp.int32) + 2 * last.astype(jnp.int32) + 4 * valid.astype(jnp.int32)
    return pair // e, ex, ob, flags


def _gather_ffn_kernel(ob_ref, tb_ref, fl_ref, pos_ref, aff_ref, h_ref, wg_ref, wu_ref, wd_ref, o_ref,
                       x_acc, g_acc, *, length, tr):
    idx = pl.program_id(0) * length + pl.program_id(1)
    flags = fl_ref[idx]
    ob = ob_ref[idx]

    @pl.when((flags & 1) != 0)
    def _():
        x_acc[...] = jnp.zeros(x_acc.shape, F32)
        g_acc[...] = jnp.zeros(g_acc.shape, F32)

    @pl.when((flags & 4) != 0)
    def _():
        rank = ob * tr + lax.broadcasted_iota(jnp.int32, (tr, tr), 0)
        match = pos_ref[...] == rank
        x_acc[...] += _dot(jnp.where(match, 1.0, 0.0).astype(BF16), h_ref[...])
        g_acc[...] += jnp.sum(jnp.where(match, aff_ref[...], 0.0), axis=1, keepdims=True)

    @pl.when((flags & 2) != 0)
    def _():
        x = x_acc[...].astype(BF16)
        hg = _dot(x, wg_ref[...])
        hu = _dot(x, wu_ref[...])
        act = (hg * _sigmoid(hg) * hu).astype(BF16)
        o_ref[...] = (_dot(act, wd_ref[...]) * g_acc[...]).astype(BF16)


def gather_ffn(tables, pos3, aff3, h, wg, wu, wd, cap, tr):
    e = pos3.shape[0]
    n = h.shape[0]
    length = n // tr + cap // tr
    ob_of, tb_of, flags = tables
    tok = pl.BlockSpec((None, 1, tr), lambda e, w, ob, tb, fl: (e, 0, tb[e * length + w]))
    wspec = lambda r, c: pl.BlockSpec((None, r, c), lambda e, w, ob, tb, fl: (e, 0, 0))
    return pl.pallas_call(
        functools.partial(_gather_ffn_kernel, length=length, tr=tr),
        grid_spec=pltpu.PrefetchScalarGridSpec(
            num_scalar_prefetch=3,
            grid=(e, length),
            in_specs=[tok, tok,
                      pl.BlockSpec((tr, D_MODEL), lambda e, w, ob, tb, fl: (tb[e * length + w], 0)),
                      wspec(D_MODEL, EXPERT_FF), wspec(D_MODEL, EXPERT_FF), wspec(EXPERT_FF, D_MODEL)],
            out_specs=pl.BlockSpec((None, tr, D_MODEL), lambda e, w, ob, tb, fl: (e, ob[e * length + w], 0)),
            scratch_shapes=[pltpu.VMEM((tr, D_MODEL), F32), pltpu.VMEM((tr, 1), F32)],
        ),
        out_shape=jax.ShapeDtypeStruct((e, cap, D_MODEL), BF16),
        compiler_params=_cparams(("parallel", "arbitrary")),
        name="gather_ffn",
    )(ob_of, tb_of, flags, pos3, aff3, h, wg, wu, wd)


def _combine_kernel(tb_ref, ex_ref, ob_ref, fl_ref, pos_ref, ye_ref, x_ref, g_ref, o_ref, *, rb, tr, final):
    w = pl.program_id(0)
    flags = fl_ref[w]

    @pl.when((flags & 1) != 0)
    def _():
        o_ref[...] = x_ref[...]

    @pl.when((flags & 4) != 0)
    def _():
        rank = ob_ref[w] * rb + lax.broadcasted_iota(jnp.int32, (rb, tr), 0)
        onehot = jnp.where(pos_ref[...] == rank, 1.0, 0.0).astype(BF16)
        o_ref[...] += _dot_tn(onehot, ye_ref[...])

    if final:
        @pl.when((flags & 2) != 0)
        def _():
            x = o_ref[...]
            ms = jnp.mean(x * x, axis=-1, keepdims=True)
            o_ref[...] = x * lax.rsqrt(ms + NORM_EPS) * g_ref[...]


def combine(table, pos3, ye, x, final_g, tr, rb, final):
    n = x.shape[0]
    tb_of, ex_of, ob_of, flags = table
    return pl.pallas_call(
        functools.partial(_combine_kernel, rb=rb, tr=tr, final=final),
        grid_spec=pltpu.PrefetchScalarGridSpec(
            num_scalar_prefetch=4,
            grid=(tb_of.shape[0],),
            in_specs=[pl.BlockSpec((None, 1, tr), lambda w, tb, ex, ob, fl: (ex[w], 0, tb[w])),
                      pl.BlockSpec((None, rb, D_MODEL), lambda w, tb, ex, ob, fl: (ex[w], ob[w], 0)),
                      pl.BlockSpec((tr, D_MODEL), lambda w, tb, ex, ob, fl: (tb[w], 0)),
                      pl.BlockSpec((1, D_MODEL), lambda w, tb, ex, ob, fl: (0, 0))],
            out_specs=pl.BlockSpec((tr, D_MODEL), lambda w, tb, ex, ob, fl: (tb[w], 0)),
        ),
        out_shape=jax.ShapeDtypeStruct((n, D_MODEL), F32),
        compiler_params=_cparams(("arbitrary",)),
        name="combine",
    )(tb_of, ex_of, ob_of, flags, pos3, ye, x, final_g)


def expert_choice_ffn(x, h, aff_t, wg, wu, wd, final_g, tr, final):
    e, n = aff_t.shape
    cap = CAPACITY_FACTOR * n // e
    pos, cend = select_tokens(aff_t, cap, tr)
    pos3 = pos.reshape(e, 1, n)
    ye = gather_ffn(routing_tables(cend, tr, cap // tr), pos3, aff_t.reshape(e, 1, n), h, wg, wu, wd, cap, tr)
    rb = min(COMBINE_RANKS, tr)
    return combine(combine_table(cend, rb, cap // rb), pos3, ye, x, final_g, tr, rb, final)


def _pad_cols(w, width):
    return jnp.pad(w, ((0, 0),) * (w.ndim - 1) + ((0, width - w.shape[-1]),))


def _pad_lora_cols(z):
    o = RWKV_MAIN
    parts = [z[..., :o]]
    for width in (DECAY_LORA, DECAY_LORA, AAA_LORA, AAA_LORA, GATE_LORA):
        parts.append(_pad_cols(z[..., o:o + width], LORA_PAD))
        o += width
    return jnp.concatenate(parts, axis=-1)


def _pad_rows(w, rows):
    return jnp.pad(w, ((0, 0),) * (w.ndim - 2) + ((0, rows - w.shape[-2]), (0, 0)))


def prepare_weights(norm1_g, w_in, lambda_qk, subln_g, shift_mu, w0, w2, a0, a2, g2, k_k, k_a, r_k,
                    lnx_w, lnx_b, w_out, norm2_g, w_router, w_gate, w_up, w_down, final_g):
    scale = ATTN_QK_DIM ** -0.5 * math.log2(math.e)
    w_attn = jnp.concatenate([w_in[..., :ATTN_WIDTH] * scale, w_in[..., ATTN_WIDTH:ATTN_COLS]], axis=-1)
    w_in_p = jnp.concatenate([w_attn, _pad_lora_cols(w_in[..., ATTN_COLS:])], axis=-1).astype(BF16)
    blk = jnp.arange(RWKV_WIDTH) // RWKV_HEAD
    return dict(
        norm1_g=norm1_g[:, None, :], w_in=w_in_p, lambda_qk=lambda_qk, subln_g=subln_g[:, None, :],
        mu=_pad_lora_cols(shift_mu), w0=w0, w2=_pad_rows(w2, LORA_PAD).astype(BF16),
        a0=a0, a2=_pad_rows(a2, LORA_PAD).astype(BF16), g2=_pad_rows(g2, LORA_PAD).astype(BF16),
        k_k=k_k[:, None, :], k_a=k_a[:, None, :], r_k=r_k.reshape(DEPTH, 1, RWKV_WIDTH),
        lnx_w=lnx_w[:, None, :], lnx_b=lnx_b[:, None, :], w_out=w_out.astype(BF16),
        norm2_g=norm2_g[:, None, :], w_router_t=jnp.swapaxes(w_router, 1, 2),
        w_gate=w_gate.astype(BF16), w_up=w_up.astype(BF16), w_down=w_down.astype(BF16),
        final_g=final_g[None, :],
        seg=(blk[:, None] == blk[None, :]).astype(F32),
        scan_consts=scan_constants(),
    )


def trunk(x, p, tiles):
    batch, seq, _ = x.shape
    n = batch * seq
    x = x.reshape(n, D_MODEL)
    attn_consts = attention_constants(tiles["tq"])
    for l in range(DEPTH):
        lambda_init = 0.8 - 0.6 * math.exp(-0.3 * l)
        za, zr = in_proj(x, p["norm1_g"][l], p["w_in"][l], tiles["tm"])
        oa = diff_attention(za, attn_consts, p["lambda_qk"][l], p["subln_g"][l], batch, seq, lambda_init,
                            tiles["tq"])
        r, v, a, lw, kd, bb, bonus, gate = rwkv_prep(
            zr, p["mu"][l], p["w0"][l], p["w2"][l], p["a0"][l], p["a2"][l], p["g2"][l],
            p["k_k"][l], p["k_a"][l], p["r_k"][l], p["seg"], seq, tiles["tm"])
        y2 = rwkv_scan(r, v, a, lw, kd, bb, p["scan_consts"], batch, seq, tiles["tb"])
        x, h, aff_t = post_mix(y2, bonus, gate, oa, x, p["lnx_w"][l], p["lnx_b"][l], p["seg"],
                               p["w_out"][l], p["norm2_g"][l], p["w_router_t"][l], tiles["tm"])
        x = expert_choice_ffn(x, h, aff_t, p["w_gate"][l], p["w_up"][l], p["w_down"][l], p["final_g"],
                              tiles["tr"], final=(l == DEPTH - 1))
    return x.reshape(batch, seq, D_MODEL)


def _tiles(seq):
    return dict(tm=min(512, seq), tq=min(512, seq), tb=min(256, seq), tr=512)


def kernel(x_prompt, x_sample, norm1_g, w_in, lambda_qk, subln_g, shift_mu, w0, w2, a0, a2, g2, k_k, k_a, r_k,
           lnx_w, lnx_b, w_out, norm2_g, w_router, w_gate, w_up, w_down, final_g):
    p = prepare_weights(norm1_g, w_in, lambda_qk, subln_g, shift_mu, w0, w2, a0, a2, g2, k_k, k_a, r_k,
                        lnx_w, lnx_b, w_out, norm2_g, w_router, w_gate, w_up, w_down, final_g)
    y_prompt = trunk(x_prompt, p, _tiles(x_prompt.shape[1]))
    y_sample = trunk(x_sample, p, _tiles(x_sample.shape[1]))
    return (y_prompt, y_sample)
```

```python
import functools
import math

import jax
import jax.numpy as jnp
from jax import lax
from jax.experimental import pallas as pl
from jax.experimental.pallas import tpu as pltpu

F32 = jnp.float32
BF16 = jnp.bfloat16
HIGHEST = lax.Precision.HIGHEST

D_MODEL = 1024
DEPTH = 4
ATTN_WIDTH = 512
ATTN_HEADS = 4
ATTN_QK_DIM = 64
ATTN_V_DIM = 128
RWKV_WIDTH = 512
RWKV_HEAD = 64
RWKV_HEADS = 8
DECAY_LORA = 32
AAA_LORA = 32
GATE_LORA = 96
LORA_PAD = 128
ATTN_COLS = 3 * ATTN_WIDTH
RWKV_MAIN = 3 * RWKV_WIDTH
RWKV_COLS_PAD = RWKV_MAIN + 5 * LORA_PAD
N_EXPERTS = 16
EXPERT_FF = 2 * D_MODEL
CAPACITY_FACTOR = 2
NORM_EPS = 1e-6
GN_EPS = 64e-5

CHUNK = 64
GROUP = 4
GROUP_W = GROUP * RWKV_HEAD
N_LEVELS = CHUNK.bit_length() - 1
SCAN_BLOCKS_PER_STEP = 2
VMEM_LIMIT = 56 * 1024 * 1024


def _cparams(sem):
    return pltpu.CompilerParams(dimension_semantics=sem, vmem_limit_bytes=VMEM_LIMIT)


def _dot(a, b):
    return jnp.dot(a, b, preferred_element_type=F32)


def _dot_nt(a, b):
    return lax.dot_general(a, b, (((1,), (1,)), ((), ())), preferred_element_type=F32)


def _dot_tn(a, b):
    return lax.dot_general(a, b, (((0,), (0,)), ((), ())), preferred_element_type=F32)


def _dot_f32(a, b):
    return jnp.dot(a, b, preferred_element_type=F32, precision=HIGHEST)


def _in_proj_kernel(x_ref, g_ref, w_ref, za_ref, zr_ref):
    x = x_ref[...]
    ms = jnp.mean(x * x, axis=-1, keepdims=True)
    h = (x * lax.rsqrt(ms + NORM_EPS) * g_ref[...]).astype(BF16)
    za = _dot(h, w_ref[:, :ATTN_COLS]).astype(BF16)
    for j in range(ATTN_COLS // ATTN_V_DIM):
        za_ref[j] = za[:, j * ATTN_V_DIM:(j + 1) * ATTN_V_DIM]
    zr_ref[...] = _dot(h, w_ref[:, ATTN_COLS:])


def in_proj(x, g, w, tm):
    n = x.shape[0]
    cols = w.shape[1]
    return pl.pallas_call(
        _in_proj_kernel,
        grid=(n // tm,),
        in_specs=[
            pl.BlockSpec((tm, D_MODEL), lambda i: (i, 0)),
            pl.BlockSpec((1, D_MODEL), lambda i: (0, 0)),
            pl.BlockSpec((D_MODEL, cols), lambda i: (0, 0)),
        ],
        out_specs=[
            pl.BlockSpec((ATTN_COLS // ATTN_V_DIM, tm, ATTN_V_DIM), lambda i: (0, i, 0)),
            pl.BlockSpec((tm, RWKV_COLS_PAD), lambda i: (i, 0)),
        ],
        out_shape=[
            jax.ShapeDtypeStruct((ATTN_COLS // ATTN_V_DIM, n, ATTN_V_DIM), BF16),
            jax.ShapeDtypeStruct((n, RWKV_COLS_PAD), F32),
        ],
        compiler_params=_cparams(("parallel",)),
        name="in_proj",
    )(x, g, w)


ROW_BLOCK = 64
POS_BITS = 9
M_INIT = -1e30
QUERY_TILES_PER_STEP = 2


def _attn_kernel(kap_ref, lq_ref, q_ref, k_ref, v_ref, g_ref, qx_ref, kx_ref, dbias_ref, o_ref,
                 q2_ref, s_ref, p_ref, m_ref, al_ref, sh_ref, acc_ref, *, seq, tq, qsub, lambda_init):
    h = pl.program_id(1)
    nk = seq // tq
    kap = kap_ref[h]
    lax.fori_loop(0, qsub, functools.partial(
        _attn_query_tile, pl.program_id(2) * qsub, kap, lq_ref, q_ref, k_ref, v_ref, g_ref, qx_ref, kx_ref,
        dbias_ref, o_ref, q2_ref, s_ref, p_ref, m_ref, al_ref, sh_ref, acc_ref, nk, tq, lambda_init), 0)


def _attn_query_tile(i0, kap, lq_ref, q_ref, k_ref, v_ref, g_ref, qx_ref, kx_ref, dbias_ref, o_ref,
                     q2_ref, s_ref, p_ref, m_ref, al_ref, sh_ref, acc_ref, nk, tq, lambda_init, sub, carry):
    i = i0 + sub
    qrows = pl.ds(pl.multiple_of(sub * tq, tq), tq)
    q = q_ref[qrows, :]
    lane = lax.broadcasted_iota(jnp.int32, q.shape, 1)
    zero = jnp.zeros_like(q)
    q2 = jnp.concatenate([jnp.where(lane < ATTN_QK_DIM, q, zero),
                          jnp.where(lane >= ATTN_QK_DIM, q, zero)], axis=0)
    qx = qx_ref[...]
    qx2 = jnp.concatenate([qx, qx], axis=0)
    q2_ref[0] = jnp.concatenate([q2, qx2], axis=1)
    q2_ref[1] = jnp.concatenate([q2, -qx2], axis=1)
    kx = kx_ref[...]
    ones = jnp.ones((tq, ATTN_V_DIM), BF16)
    m_ref[...] = jnp.full(m_ref.shape, M_INIT, F32)
    acc_ref[...] = jnp.zeros(acc_ref.shape, F32)

    def tile_of(step):
        jj = step - 1
        return jnp.where(step == 0, i, jj + (jj >= i).astype(jnp.int32))

    def scores(step, buf):
        j = tile_of(step)
        start = pl.multiple_of(j * tq, tq)
        k_aug = jnp.concatenate([k_ref[pl.ds(start, tq), :], kx], axis=1)
        s_ref[buf] = _dot_nt(q2_ref[(j > i).astype(jnp.int32)], k_aug)

    def softmax_and_values(step, buf):
        j = tile_of(step)
        kappa = -kap * jnp.abs(i - j).astype(F32)
        start = pl.multiple_of(j * tq, tq)
        v_aug = jnp.concatenate([v_ref[pl.ds(start, tq), :], ones], axis=1)
        for rb in range(2 * tq // ROW_BLOCK):
            rows = slice(rb * ROW_BLOCK, (rb + 1) * ROW_BLOCK)
            m_old = m_ref[rows, :]
            m_new = jnp.maximum(m_old, jnp.max(s_ref[buf, rows, :], axis=-1, keepdims=True) + kappa)
            al_ref[rows, :] = jnp.exp2(m_old - m_new)
            sh_ref[rows, :] = m_new - kappa
            m_ref[rows, :] = m_new
        for rb in range(2 * tq // ROW_BLOCK):
            rows = slice(rb * ROW_BLOCK, (rb + 1) * ROW_BLOCK)
            p_ref[rows, :] = jnp.exp2(s_ref[buf, rows, :] - sh_ref[rows, :]).astype(BF16)
        acc_ref[...] = al_ref[...] * acc_ref[...] + _dot(p_ref[...], v_aug)

    dstart = pl.multiple_of(i * tq, tq)
    s_diag = _dot_nt(q2, k_ref[pl.ds(dstart, tq), :])
    s_ref[0] = (s_diag.reshape(2, tq, tq) + dbias_ref[...][None]).reshape(2 * tq, tq)

    def pair(t, carry):
        scores(2 * t + 1, 1)
        softmax_and_values(2 * t, 0)
        scores(2 * t + 2, 0)
        softmax_and_values(2 * t + 1, 1)
        return carry

    lax.fori_loop(0, nk // 2 - 1, pair, 0)
    scores(nk - 1, 1)
    softmax_and_values(nk - 2, 0)
    softmax_and_values(nk - 1, 1)

    lq = lq_ref[...]
    lam = (jnp.exp(jnp.sum(lq[0:1] * lq[1:2], axis=-1, keepdims=True))
           - jnp.exp(jnp.sum(lq[2:3] * lq[3:4], axis=-1, keepdims=True)) + lambda_init)
    acc = acc_ref[...]
    o_all = acc[:, :ATTN_V_DIM] / acc[:, ATTN_V_DIM:]
    o = o_all[:tq] - lam * o_all[tq:]
    ms = jnp.mean(o * o, axis=-1, keepdims=True)
    o = o * lax.rsqrt(ms + NORM_EPS) * g_ref[...] * (1.0 - lambda_init)
    o_ref[qrows, :] = o.astype(BF16)
    return carry


def attention_constants(tq):
    assert tq <= (1 << POS_BITS)
    slopes = 2.0 ** (-8.0 * jnp.arange(1, ATTN_HEADS + 1, dtype=F32) / ATTN_HEADS)
    c = slopes * math.log2(math.e)
    cb = c[:, None] * (2.0 ** jnp.arange(POS_BITS, dtype=F32))[None, :]
    hi = cb.astype(BF16).astype(F32)
    lo = (cb - hi).astype(BF16).astype(F32)
    pos = jnp.arange(tq)
    bits = ((pos[:, None] >> jnp.arange(POS_BITS)[None, :]) & 1).astype(F32)
    hb = jnp.broadcast_to(bits[None], (ATTN_HEADS, tq, POS_BITS))
    const = lambda x: jnp.broadcast_to(x[:, None, :], (ATTN_HEADS, tq, POS_BITS))
    pad = jnp.zeros((ATTN_HEADS, tq, ATTN_V_DIM - 4 * POS_BITS), F32)
    qx = jnp.concatenate([const(hi), const(lo), hb, hb, pad], axis=-1).astype(BF16)
    kx = jnp.concatenate([hb, hb, const(-hi), const(-lo), pad], axis=-1).astype(BF16)
    dist = jnp.abs(pos[:, None] - pos[None, :]).astype(F32)
    dbias = -c[:, None, None] * dist[None]
    kap = c * tq
    return kap, qx, kx, dbias


def diff_attention(za, consts, lq, subln_g, batch, seq, lambda_init, tq):
    n = batch * seq
    nq = seq // tq
    kap, qx, kx, dbias = consts
    assert nq % 2 == 0, "key tiles are processed in pairs"
    qsub = QUERY_TILES_PER_STEP if nq % QUERY_TILES_PER_STEP == 0 else 1
    nq //= qsub
    kern = functools.partial(_attn_kernel, seq=seq, tq=tq, qsub=qsub, lambda_init=lambda_init)
    per_head = lambda w: pl.BlockSpec((None, tq, w), lambda b, h, i: (h, 0, 0))
    return pl.pallas_call(
        kern,
        grid=(batch, ATTN_HEADS, nq),
        in_specs=[
            pl.BlockSpec(memory_space=pltpu.SMEM),
            pl.BlockSpec((4, ATTN_QK_DIM), lambda b, h, i: (0, 0)),
            pl.BlockSpec((None, qsub * tq, ATTN_V_DIM), lambda b, h, i: (h, b * nq + i, 0)),
            pl.BlockSpec((None, seq, ATTN_V_DIM), lambda b, h, i: (ATTN_HEADS + h, b, 0)),
            pl.BlockSpec((None, seq, ATTN_V_DIM), lambda b, h, i: (2 * ATTN_HEADS + h, b, 0)),
            pl.BlockSpec((1, ATTN_V_DIM), lambda b, h, i: (0, 0)),
            per_head(ATTN_V_DIM), per_head(ATTN_V_DIM), per_head(tq),
        ],
        out_specs=pl.BlockSpec((None, qsub * tq, ATTN_V_DIM), lambda b, h, i: (h, b * nq + i, 0)),
        out_shape=jax.ShapeDtypeStruct((ATTN_HEADS, n, ATTN_V_DIM), BF16),
        scratch_shapes=[
            pltpu.VMEM((2, 2 * tq, 2 * ATTN_V_DIM), BF16),
            pltpu.VMEM((2, 2 * tq, tq), F32),
            pltpu.VMEM((2 * tq, tq), BF16),
            pltpu.VMEM((2 * tq, 1), F32),
            pltpu.VMEM((2 * tq, 1), F32),
            pltpu.VMEM((2 * tq, 1), F32),
            pltpu.VMEM((2 * tq, 2 * ATTN_V_DIM), F32),
        ],
        compiler_params=_cparams(("parallel", "parallel", "parallel")),
        name="diff_attention",
    )(kap, lq, za, za, za, subln_g, qx, kx, dbias)


def _softplus(x):
    return jnp.maximum(x, 0.0) + jnp.log(1.0 + jnp.exp(-jnp.abs(x)))


def _sigmoid(x):
    return 1.0 / (1.0 + jnp.exp(-x))


def _prep_kernel(z_ref, zp_ref, zn_ref, mu_ref, w0_ref, w2_ref, a0_ref, a2_ref, g2_ref,
                 kk_ref, ka_ref, rk_ref, seg_ref,
                 r_ref, v_ref, a_ref, lw_ref, kd_ref, bb_ref, bonus_ref, gate_ref, *, seq, tm):
    i = pl.program_id(0)
    z = z_ref[...]
    row = lax.broadcasted_iota(jnp.int32, z.shape, 0)
    first = (i * tm) % seq == 0
    last = ((i + 1) * tm) % seq == 0
    zp = jnp.where(first, 0.0, zp_ref[7:8, :])
    zn = jnp.where(last, 0.0, zn_ref[0:1, :])
    prev = jnp.where(row == 0, zp, pltpu.roll(z, 1, 0))
    nxt = jnp.where(row == tm - 1, zn, pltpu.roll(z, tm - 1, 0))
    z = z + mu_ref[0:1, :] * (prev - z) + mu_ref[1:2, :] * (nxt - z)

    w = RWKV_WIDTH
    r = z[:, 0:w]
    k = z[:, w:2 * w]
    v = z[:, 2 * w:3 * w]
    seg = seg_ref[...]
    kk = k * kk_ref[...]
    ss = _dot_f32(kk * kk, seg)
    kk = kk / jnp.maximum(jnp.sqrt(ss), 1e-12)
    r_ref[...] = r
    v_ref[...] = v
    a_ref[...] = -kk
    for d in range(2):
        xw = z[:, RWKV_MAIN + d * LORA_PAD:RWKV_MAIN + (d + 1) * LORA_PAD]
        xa = z[:, RWKV_MAIN + (2 + d) * LORA_PAD:RWKV_MAIN + (3 + d) * LORA_PAD]
        wlog = -_softplus(-(w0_ref[d:d + 1, :] + _dot(jnp.tanh(xw).astype(BF16), w2_ref[d]))) - 0.5
        lw_ref[d] = -jnp.exp(wlog)
        iclr = _sigmoid(a0_ref[d:d + 1, :] + _dot(xa.astype(BF16), a2_ref[d]))
        kd_ref[d] = k * (1.0 + (iclr - 1.0) * ka_ref[...])
        bb_ref[d] = kk * iclr
    xg = z[:, RWKV_MAIN + 4 * LORA_PAD:RWKV_MAIN + 5 * LORA_PAD]
    gate_ref[...] = _dot(_sigmoid(xg).astype(BF16), g2_ref[...])
    bonus_ref[...] = _dot_f32(r * k * rk_ref[...], seg) * v


def rwkv_prep(zr, mu, w0, w2, a0, a2, g2, k_k, k_a, r_k, seg, seq, tm):
    n, cols = zr.shape
    nb8 = n // 8
    tb8 = tm // 8
    full = lambda shape: pl.BlockSpec(shape, lambda i: (0,) * len(shape))
    tok = pl.BlockSpec((tm, RWKV_WIDTH), lambda i: (i, 0))
    tok2 = pl.BlockSpec((2, tm, RWKV_WIDTH), lambda i: (0, i, 0))
    sds = jax.ShapeDtypeStruct((n, RWKV_WIDTH), F32)
    sds2 = jax.ShapeDtypeStruct((2, n, RWKV_WIDTH), F32)
    return pl.pallas_call(
        functools.partial(_prep_kernel, seq=seq, tm=tm),
        grid=(n // tm,),
        in_specs=[
            pl.BlockSpec((tm, cols), lambda i: (i, 0)),
            pl.BlockSpec((8, cols), lambda i: (jnp.maximum(i * tb8 - 1, 0), 0)),
            pl.BlockSpec((8, cols), lambda i: (jnp.minimum((i + 1) * tb8, nb8 - 1), 0)),
            full((2, cols)), full((2, RWKV_WIDTH)), full((2, LORA_PAD, RWKV_WIDTH)),
            full((2, RWKV_WIDTH)), full((2, LORA_PAD, RWKV_WIDTH)), full((LORA_PAD, RWKV_WIDTH)),
            full((1, RWKV_WIDTH)), full((1, RWKV_WIDTH)), full((1, RWKV_WIDTH)),
            full((RWKV_WIDTH, RWKV_WIDTH)),
        ],
        out_specs=[tok, tok, tok, tok2, tok2, tok2, tok, tok],
        out_shape=[sds, sds, sds, sds2, sds2, sds2, sds, sds],
        compiler_params=_cparams(("parallel",)),
        name="rwkv_prep",
    )(zr, zr, zr, mu, w0, w2, a0, a2, g2, k_k, k_a, r_k, seg)


def _scan_kernel(r_ref, v_ref, a_ref, lw_ref, k_ref, b_ref, tri_ref, strict_ref, incl_ref, lvl_ref, eye_ref, bd_ref,
                 y_ref, h_ref, *, tb, nsub):
    d = pl.program_id(1)
    i = pl.program_id(2)
    nc = tb // CHUNK
    ngrp = RWKV_WIDTH // GROUP_W

    @pl.when(i == 0)
    def _():
        h_ref[...] = jnp.zeros(h_ref.shape, F32)

    tri = tri_ref[...]
    strict = strict_ref[...] > 0.5
    incl = incl_ref[...] > 0.5
    eye = eye_ref[...]
    bdm = bd_ref[...] > 0.5
    ones16 = jnp.ones((16, GROUP_W), BF16)
    row16 = lax.broadcasted_iota(jnp.int32, (16, GROUP_W), 0)

    def bd(x):
        return jnp.where(bdm, jnp.concatenate([x] * GROUP, axis=0), 0.0).astype(BF16)

    def split(x):
        hi = x.astype(BF16)
        return hi, (x - hi.astype(F32)).astype(BF16)

    lax.fori_loop(0, nsub, functools.partial(
        _scan_block, d, nsub, tb, nc, ngrp, bd, split, tri, strict, incl, eye, bdm, ones16, row16,
        r_ref, v_ref, a_ref, lw_ref, k_ref, b_ref, lvl_ref, y_ref, h_ref), 0)


def _scan_block(d, nsub, tb, nc, ngrp, bd, split, tri, strict, incl, eye, bdm, ones16, row16,
                r_ref, v_ref, a_ref, lw_ref, k_ref, b_ref, lvl_ref, y_ref, h_ref, sub, carry):
    base = (sub + d * (nsub - 1 - 2 * sub)) * tb
    slices = []
    for cc in range(nc):
        ci = cc + d * (nc - 1 - 2 * cc)
        slices.append(pl.ds(pl.multiple_of(base + ci * CHUNK, CHUNK), CHUNK))
    groups = [slice(g * GROUP_W, (g + 1) * GROUP_W) for g in range(ngrp)]
    units = [dict(sl=sl, cols=cols) for sl in slices for cols in groups]

    for c in units:
        lw = lw_ref[c["sl"], c["cols"]]
        r = r_ref[c["sl"], c["cols"]]
        v = v_ref[c["sl"], c["cols"]]
        a = a_ref[c["sl"], c["cols"]]
        k = k_ref[c["sl"], c["cols"]]
        b = b_ref[c["sl"], c["cols"]]
        hi, lo = split(lw)
        cs = _dot(tri, jnp.concatenate([hi, lo], axis=1))
        lcum = cs[:, :GROUP_W] + cs[:, GROUP_W:]
        ltot = jnp.where(d == 0, lcum[CHUNK - 1:CHUNK], lcum[0:1])
        at = a * jnp.exp(lcum - lw)
        rt = r * jnp.exp(lcum)
        einv = jnp.exp(-lcum)
        eend = jnp.exp(ltot - lcum)
        c["x"] = jnp.concatenate([at, rt], axis=0).astype(BF16)
        c["ybk"] = jnp.concatenate([bd(b * einv), bd(k * einv)], axis=0)
        c["bkw"] = jnp.concatenate([b * eend, k * eend], axis=0).astype(BF16)
        c["vbd"] = bd(v)
        c["v"] = v.astype(BF16)
        wtot = jnp.exp(ltot)
        whi = wtot.astype(BF16).astype(F32)
        c["whl"] = jnp.where(row16 == 0, whi, jnp.where(row16 == 1, wtot - whi, 0.0)).astype(BF16)
    for c in units:
        amat = _dot_nt(c["x"], c.pop("ybk"))
        c["aab"] = jnp.where(strict, amat[:CHUNK, :GROUP_W], 0.0)
        c["aak"] = jnp.where(strict, amat[:CHUNK, GROUP_W:], 0.0).astype(BF16)
        c["ar"] = jnp.concatenate([jnp.where(incl, amat[CHUNK:, :GROUP_W], 0.0),
                                   jnp.where(incl, amat[CHUNK:, GROUP_W:], 0.0)], axis=1).astype(BF16)
        c["p"] = eye + jnp.where(lvl_ref[0] > 0.5, c["aab"], 0.0)
    for lvl in range(1, N_LEVELS):
        for c in units:
            e = jnp.where(lvl_ref[lvl] > 0.5, c["aab"], 0.0)
            c["g"] = _dot(c["p"].astype(BF16), bd(e)).astype(BF16)
        for c in units:
            c["p"] = c["p"] + _dot(c.pop("g"), bd(c["p"]))
    for c in units:
        c["p"] = c["p"].astype(BF16)
        c["av"] = _dot(c.pop("aak"), c["vbd"])
        c["wcol"] = _dot_tn(c.pop("whl"), ones16)

    for cc in range(nc):
        cu = units[cc * ngrp:(cc + 1) * ngrp]
        hs = [h_ref[g] for g in range(ngrp)]
        hx = [_dot(c["x"], h.astype(BF16)) for c, h in zip(cu, hs)]
        us = [_dot(c["p"], bd(x[:CHUNK] + c["av"])) for c, x in zip(cu, hx)]
        for g, (c, h, x, u) in enumerate(zip(cu, hs, hx, us)):
            y_ref[c["sl"], c["cols"]] = x[CHUNK:] + _dot(c["ar"], jnp.concatenate([bd(u), c["vbd"]], axis=0))
            upd = _dot_tn(c["bkw"], jnp.concatenate([u.astype(BF16), c["v"]], axis=0))
            h_ref[g] = c["wcol"] * h + jnp.where(bdm, upd, 0.0)
    return carry


def rwkv_scan(r, v, a, lw, kd, bb, consts, batch, seq, tb):
    n = batch * seq
    nsub = SCAN_BLOCKS_PER_STEP if seq % (tb * SCAN_BLOCKS_PER_STEP) == 0 else 1
    nblk = seq // (tb * nsub)
    tri, strict, incl, lvl, eye, bdm = consts

    def tmap(b, d, i):
        return (b * nblk + i + d * (nblk - 1 - 2 * i), 0)

    def tmap_d(b, d, i):
        return (d, b * nblk + i + d * (nblk - 1 - 2 * i), 0)

    tok = pl.BlockSpec((nsub * tb, RWKV_WIDTH), tmap)
    tok_d = pl.BlockSpec((None, nsub * tb, RWKV_WIDTH), tmap_d)
    per_d = lambda shape: pl.BlockSpec((None,) + shape, lambda b, d, i: (d,) + (0,) * len(shape))
    full = lambda shape: pl.BlockSpec(shape, lambda b, d, i: (0, 0))
    return pl.pallas_call(
        functools.partial(_scan_kernel, tb=tb, nsub=nsub),
        grid=(batch, 2, nblk),
        in_specs=[tok, tok, tok, tok_d, tok_d, tok_d,
                  per_d((CHUNK, CHUNK)), per_d((CHUNK, GROUP * CHUNK)), per_d((CHUNK, GROUP * CHUNK)),
                  per_d((N_LEVELS, CHUNK, GROUP * CHUNK)),
                  full((CHUNK, GROUP * CHUNK)), full((GROUP_W, GROUP_W))],
        out_specs=tok_d,
        out_shape=jax.ShapeDtypeStruct((2, n, RWKV_WIDTH), F32),
        scratch_shapes=[pltpu.VMEM((RWKV_WIDTH // GROUP_W, GROUP_W, GROUP_W), F32)],
        compiler_params=_cparams(("parallel", "parallel", "arbitrary")),
        name="rwkv_scan",
    )(r, v, a, lw, kd, bb, tri, strict, incl, lvl, eye, bdm)


def scan_constants():
    t = jnp.arange(CHUNK)
    before_eq = (t[None, :] <= t[:, None])
    tri = jnp.stack([before_eq, before_eq.T]).astype(F32)
    incl = jnp.tile(tri, (1, 1, GROUP))
    tri = tri.astype(BF16)
    eye = jnp.tile(jnp.eye(CHUNK, dtype=F32), (1, GROUP))
    strict = incl - eye[None]
    levels = []
    for lv in range(N_LEVELS):
        half = t // (1 << lv)
        fwd = ((half[:, None] // 2 == half[None, :] // 2) & (half[:, None] % 2 == 1) & (half[None, :] % 2 == 0))
        levels.append(jnp.stack([fwd, fwd.T]))
    lvl = jnp.tile(jnp.stack(levels, axis=1).astype(F32), (1, 1, 1, GROUP))
    blk = jnp.arange(GROUP_W) // RWKV_HEAD
    bdm = (blk[:, None] == blk[None, :]).astype(F32)
    return tri, strict, incl, lvl, eye, bdm


def _post_kernel(yf_ref, yb_ref, bonus_ref, gate_ref, oa_ref, x_ref, lw_ref, lb_ref, seg_ref,
                 wo_ref, g2_ref, wr_ref, xo_ref, h_ref, aff_ref):
    y = yf_ref[...] + yb_ref[...]
    seg = seg_ref[...]
    mean = _dot_f32(y, seg) * (1.0 / RWKV_HEAD)
    yc = y - mean
    var = _dot_f32(yc * yc, seg) * (1.0 / RWKV_HEAD)
    yn = yc * lax.rsqrt(var + GN_EPS) * lw_ref[...] + lb_ref[...]
    orw = ((yn + bonus_ref[...]) * gate_ref[...]).astype(BF16)
    oa = jnp.concatenate([oa_ref[j] for j in range(ATTN_HEADS)], axis=1)
    x = (x_ref[...] + _dot(oa, wo_ref[:ATTN_WIDTH, :]) + _dot(orw, wo_ref[ATTN_WIDTH:, :]))
    xo_ref[...] = x
    ms = jnp.mean(x * x, axis=-1, keepdims=True)
    h = x * lax.rsqrt(ms + NORM_EPS) * g2_ref[...]
    h_ref[...] = h.astype(BF16)
    logits = lax.dot_general(wr_ref[...], h, (((1,), (1,)), ((), ())),
                             preferred_element_type=F32, precision=HIGHEST)
    logits = logits - jnp.max(logits, axis=0, keepdims=True)
    e = jnp.exp(logits)
    aff_ref[...] = e / jnp.sum(e, axis=0, keepdims=True)


def post_mix(y2, bonus, gate, oa, x, lnx_w, lnx_b, seg, w_out, norm2_g, w_router_t, tm):
    n = x.shape[0]
    full = lambda shape: pl.BlockSpec(shape, lambda i: (0,) * len(shape))
    tok = lambda w: pl.BlockSpec((tm, w), lambda i: (i, 0))
    return pl.pallas_call(
        _post_kernel,
        grid=(n // tm,),
        in_specs=[
            pl.BlockSpec((None, tm, RWKV_WIDTH), lambda i: (0, i, 0)),
            pl.BlockSpec((None, tm, RWKV_WIDTH), lambda i: (1, i, 0)),
            tok(RWKV_WIDTH), tok(RWKV_WIDTH),
            pl.BlockSpec((ATTN_HEADS, tm, ATTN_V_DIM), lambda i: (0, i, 0)), tok(D_MODEL),
            full((1, RWKV_WIDTH)), full((1, RWKV_WIDTH)), full((RWKV_WIDTH, RWKV_WIDTH)),
            full((D_MODEL, D_MODEL)), full((1, D_MODEL)), full((N_EXPERTS, D_MODEL)),
        ],
        out_specs=[tok(D_MODEL), tok(D_MODEL), pl.BlockSpec((N_EXPERTS, tm), lambda i: (0, i))],
        out_shape=[
            jax.ShapeDtypeStruct((n, D_MODEL), F32),
            jax.ShapeDtypeStruct((n, D_MODEL), BF16),
            jax.ShapeDtypeStruct((N_EXPERTS, n), F32),
        ],
        compiler_params=_cparams(("parallel",)),
        name="post_mix",
    )(y2, y2, bonus, gate, oa, x, lnx_w, lnx_b, seg, w_out, norm2_g, w_router_t)


COMBINE_RANKS = 256
GATHER_TOKENS = 1024


def _thresh_kernel(aff_ref, thr_ref, need_ref, *, cap):
    bits = pltpu.bitcast(aff_ref[...], jnp.int32)
    e = bits.shape[0]

    def body(it, thr):
        cand = thr | jnp.left_shift(jnp.int32(1), 30 - it)
        cnt = jnp.sum(jnp.where(bits >= cand, 1.0, 0.0), axis=1, keepdims=True)
        return jnp.where(cnt >= cap, cand, thr)

    thr = lax.fori_loop(0, 31, body, jnp.zeros((e, 1), jnp.int32))
    n_gt = jnp.sum(jnp.where(bits > thr, 1.0, 0.0), axis=1, keepdims=True)
    thr_ref[...] = jnp.broadcast_to(thr, thr_ref.shape)
    need_ref[...] = jnp.broadcast_to(cap - n_gt, need_ref.shape)


def _prefix_kernel(aff_ref, thr_ref, need_ref, u_ref, pos_ref, cend_ref, carry_ref):
    @pl.when(pl.program_id(0) == 0)
    def _():
        carry_ref[...] = jnp.zeros(carry_ref.shape, F32)

    bits = pltpu.bitcast(aff_ref[...], jnp.int32)
    e, tr = bits.shape
    thr = thr_ref[:, 0:1]
    need = need_ref[:, 0:1]
    gt = bits > thr
    eq = bits == thr
    marks = jnp.concatenate([jnp.where(gt, 1.0, 0.0), jnp.where(eq, 1.0, 0.0)], axis=0).astype(BF16)
    pre = _dot(marks, u_ref[...]) + carry_ref[...]
    pgt = pre[:e]
    peq = pre[e:]
    sel = jnp.where(gt, 1.0, jnp.where(eq, jnp.where(peq <= need, 1.0, 0.0), 0.0))
    rank = pgt + jnp.minimum(peq, need) - 1.0
    pos_ref[...] = jnp.where(sel > 0.5, rank, -1.0).astype(jnp.int32)
    carry_ref[...] = pre[:, tr - 1:tr]
    cend_ref[...] = jnp.broadcast_to(rank[:, tr - 1:tr] + 1.0, cend_ref.shape)


def select_tokens(aff_t, cap, tr):
    e, n = aff_t.shape
    nb = n // tr
    thr, need = pl.pallas_call(
        functools.partial(_thresh_kernel, cap=float(cap)),
        out_shape=[jax.ShapeDtypeStruct((e, 128), jnp.int32), jax.ShapeDtypeStruct((e, 128), F32)],
        compiler_params=pltpu.CompilerParams(vmem_limit_bytes=VMEM_LIMIT),
        name="route_threshold",
    )(aff_t)
    upper = (jnp.arange(tr)[:, None] <= jnp.arange(tr)[None, :]).astype(BF16)
    pos, cend = pl.pallas_call(
        _prefix_kernel,
        grid=(nb,),
        in_specs=[pl.BlockSpec((e, tr), lambda i: (0, i)),
                  pl.BlockSpec((e, 128), lambda i: (0, 0)),
                  pl.BlockSpec((e, 128), lambda i: (0, 0)),
                  pl.BlockSpec((tr, tr), lambda i: (0, 0))],
        out_specs=[pl.BlockSpec((e, tr), lambda i: (0, i)),
                   pl.BlockSpec((None, e, 128), lambda i: (i, 0, 0))],
        out_shape=[jax.ShapeDtypeStruct((e, n), jnp.int32), jax.ShapeDtypeStruct((nb, e, 128), F32)],
        scratch_shapes=[pltpu.VMEM((2 * e, 1), F32)],
        compiler_params=_cparams(("arbitrary",)),
        name="route_prefix",
    )(aff_t, thr, need, upper)
    return pos, cend[:, :, 0].T.astype(jnp.int32)


def _pick(onehot, values):
    return jnp.sum(jnp.where(onehot, values[None, :], 0), axis=1)


def gather_table(cend, rb, n_rb):
    nb = cend.shape[0]
    cstart = jnp.concatenate([jnp.zeros((1,), jnp.int32), cend[:-1]])
    count = cend - cstart
    ob_first = jnp.minimum(cstart // rb, n_rb - 1)
    ob_last = jnp.where(count > 0, (cend - 1) // rb, ob_first)
    nitems = jnp.where(count > 0, ob_last - ob_first + 1, 0)
    cum = jnp.cumsum(nitems)
    start = cum - nitems
    w = jnp.arange(nb + n_rb, dtype=jnp.int32)
    owner = (w[:, None] >= start[None, :]) & (w[:, None] < cum[None, :])
    valid = w < cum[-1]
    tb = jnp.where(valid, _pick(owner, jnp.arange(nb, dtype=jnp.int32)), nb - 1)
    ob = jnp.where(valid, _pick(owner, ob_first - start) + w, n_rb - 1)
    prev_ob = jnp.concatenate([jnp.full((1,), -1, jnp.int32), ob[:-1]])
    next_ob = jnp.concatenate([ob[1:], jnp.full((1,), -1, jnp.int32)])
    next_valid = jnp.concatenate([valid[1:], jnp.zeros((1,), bool)])
    first = valid & (ob != prev_ob)
    last = valid & ((ob != next_ob) | ~next_valid)
    flags = first.astype(jnp.int32) + 2 * last.astype(jnp.int32) + 4 * valid.astype(jnp.int32)
    return ob, tb, flags


def routing_tables(cend, rb, n_rb):
    ob, tb, flags = jax.vmap(lambda c: gather_table(c, rb, n_rb))(cend)
    return ob.reshape(-1), tb.reshape(-1), flags.reshape(-1)


def combine_table(cend, rb, n_rb):
    e, nb = cend.shape
    cstart = jnp.concatenate([jnp.zeros((e, 1), jnp.int32), cend[:, :-1]], axis=1)
    ob_first = jnp.minimum(cstart // rb, n_rb - 1)
    ob_last = jnp.where(cend > cstart, (cend - 1) // rb, ob_first)
    obf = ob_first.T.reshape(-1)
    nitems = (ob_last - ob_first + 1).T.reshape(-1)
    cum = jnp.cumsum(nitems)
    start = cum - nitems
    length = nb * e + e * n_rb
    w = jnp.arange(length, dtype=jnp.int32)
    owner = (w[:, None] >= start[None, :]) & (w[:, None] < cum[None, :])
    valid = w < cum[-1]
    pair = jnp.where(valid, _pick(owner, jnp.arange(nb * e, dtype=jnp.int32)), nb * e - 1)
    ob = jnp.where(valid, _pick(owner, obf - start) + w, obf[-1] + nitems[-1] - 1)
    ex = pair % e
    first = valid & (_pick(owner, start) == w) & (ex == 0)
    last = valid & (_pick(owner, cum) - 1 == w) & (ex == e - 1)
    flags = first.astype(jnp.int32) + 2 * last.astype(jnp.int32) + 4 * valid.astype(jnp.int32)
    return pair // e, ex, ob, flags


def _gather_ffn_kernel(ob_ref, tb_ref, fl_ref, pos_ref, aff_ref, h_ref, wg_ref, wu_ref, wd_ref, o_ref,
                       x_acc, g_acc, *, length, rb, tg):
    idx = pl.program_id(0) * length + pl.program_id(1)
    flags = fl_ref[idx]
    ob = ob_ref[idx]

    @pl.when((flags & 1) != 0)
    def _():
        x_acc[...] = jnp.zeros(x_acc.shape, F32)
        g_acc[...] = jnp.zeros(g_acc.shape, F32)

    @pl.when((flags & 4) != 0)
    def _():
        rank = ob * rb + lax.broadcasted_iota(jnp.int32, (rb, tg), 0)
        match = pos_ref[...] == rank
        x_acc[...] += _dot(jnp.where(match, 1.0, 0.0).astype(BF16), h_ref[...])
        g_acc[...] += jnp.sum(jnp.where(match, aff_ref[...], 0.0), axis=1, keepdims=True)

    @pl.when((flags & 2) != 0)
    def _():
        x = x_acc[...].astype(BF16)
        hg = _dot(x, wg_ref[...])
        hu = _dot(x, wu_ref[...])
        act = (hg * _sigmoid(hg) * hu).astype(BF16)
        o_ref[...] = (_dot(act, wd_ref[...]) * g_acc[...]).astype(BF16)


def gather_ffn(tables, pos3, aff3, h, wg, wu, wd, cap, rb, tg):
    e = pos3.shape[0]
    n = h.shape[0]
    length = n // tg + cap // rb
    ob_of, tb_of, flags = tables
    tok = pl.BlockSpec((None, 1, tg), lambda e, w, ob, tb, fl: (e, 0, tb[e * length + w]))
    wspec = lambda r, c: pl.BlockSpec((None, r, c), lambda e, w, ob, tb, fl: (e, 0, 0))
    return pl.pallas_call(
        functools.partial(_gather_ffn_kernel, length=length, rb=rb, tg=tg),
        grid_spec=pltpu.PrefetchScalarGridSpec(
            num_scalar_prefetch=3,
            grid=(e, length),
            in_specs=[tok, tok,
                      pl.BlockSpec((tg, D_MODEL), lambda e, w, ob, tb, fl: (tb[e * length + w], 0)),
                      wspec(D_MODEL, EXPERT_FF), wspec(D_MODEL, EXPERT_FF), wspec(EXPERT_FF, D_MODEL)],
            out_specs=pl.BlockSpec((None, rb, D_MODEL), lambda e, w, ob, tb, fl: (e, ob[e * length + w], 0)),
            scratch_shapes=[pltpu.VMEM((rb, D_MODEL), F32), pltpu.VMEM((rb, 1), F32)],
        ),
        out_shape=jax.ShapeDtypeStruct((e, cap, D_MODEL), BF16),
        compiler_params=_cparams(("parallel", "arbitrary")),
        name="gather_ffn",
    )(ob_of, tb_of, flags, pos3, aff3, h, wg, wu, wd)


def _combine_kernel(tb_ref, ex_ref, ob_ref, fl_ref, pos_ref, ye_ref, x_ref, g_ref, o_ref, *, rb, tr, final):
    w = pl.program_id(0)
    flags = fl_ref[w]

    @pl.when((flags & 1) != 0)
    def _():
        o_ref[...] = x_ref[...]

    @pl.when((flags & 4) != 0)
    def _():
        rank = ob_ref[w] * rb + lax.broadcasted_iota(jnp.int32, (rb, tr), 0)
        onehot = jnp.where(pos_ref[...] == rank, 1.0, 0.0).astype(BF16)
        o_ref[...] += _dot_tn(onehot, ye_ref[...])

    if final:
        @pl.when((flags & 2) != 0)
        def _():
            x = o_ref[...]
            ms = jnp.mean(x * x, axis=-1, keepdims=True)
            o_ref[...] = x * lax.rsqrt(ms + NORM_EPS) * g_ref[...]


def combine(table, pos3, ye, x, final_g, tr, rb, final):
    n = x.shape[0]
    tb_of, ex_of, ob_of, flags = table
    return pl.pallas_call(
        functools.partial(_combine_kernel, rb=rb, tr=tr, final=final),
        grid_spec=pltpu.PrefetchScalarGridSpec(
            num_scalar_prefetch=4,
            grid=(tb_of.shape[0],),
            in_specs=[pl.BlockSpec((None, 1, tr), lambda w, tb, ex, ob, fl: (ex[w], 0, tb[w])),
                      pl.BlockSpec((None, rb, D_MODEL), lambda w, tb, ex, ob, fl: (ex[w], ob[w], 0)),
                      pl.BlockSpec((tr, D_MODEL), lambda w, tb, ex, ob, fl: (tb[w], 0)),
                      pl.BlockSpec((1, D_MODEL), lambda w, tb, ex, ob, fl: (0, 0))],
            out_specs=pl.BlockSpec((tr, D_MODEL), lambda w, tb, ex, ob, fl: (tb[w], 0)),
        ),
        out_shape=jax.ShapeDtypeStruct((n, D_MODEL), F32),
        compiler_params=_cparams(("arbitrary",)),
        name="combine",
    )(tb_of, ex_of, ob_of, flags, pos3, ye, x, final_g)


def expert_choice_ffn(x, h, aff_t, wg, wu, wd, final_g, tr, final):
    e, n = aff_t.shape
    cap = CAPACITY_FACTOR * n // e
    pos, cend = select_tokens(aff_t, cap, tr)
    pos3 = pos.reshape(e, 1, n)
    gpb = min(GATHER_TOKENS, n) // tr
    ye = gather_ffn(routing_tables(cend[:, gpb - 1::gpb], tr, cap // tr), pos3, aff_t.reshape(e, 1, n), h,
                    wg, wu, wd, cap, tr, gpb * tr)
    rb = min(COMBINE_RANKS, tr)
    return combine(combine_table(cend, rb, cap // rb), pos3, ye, x, final_g, tr, rb, final)


def _pad_cols(w, width):
    return jnp.pad(w, ((0, 0),) * (w.ndim - 1) + ((0, width - w.shape[-1]),))


def _pad_lora_cols(z):
    o = RWKV_MAIN
    parts = [z[..., :o]]
    for width in (DECAY_LORA, DECAY_LORA, AAA_LORA, AAA_LORA, GATE_LORA):
        parts.append(_pad_cols(z[..., o:o + width], LORA_PAD))
        o += width
    return jnp.concatenate(parts, axis=-1)


def _pad_rows(w, rows):
    return jnp.pad(w, ((0, 0),) * (w.ndim - 2) + ((0, rows - w.shape[-2]), (0, 0)))


def prepare_weights(norm1_g, w_in, lambda_qk, subln_g, shift_mu, w0, w2, a0, a2, g2, k_k, k_a, r_k,
                    lnx_w, lnx_b, w_out, norm2_g, w_router, w_gate, w_up, w_down, final_g):
    scale = ATTN_QK_DIM ** -0.5 * math.log2(math.e)
    w_attn = jnp.concatenate([w_in[..., :ATTN_WIDTH] * scale, w_in[..., ATTN_WIDTH:ATTN_COLS]], axis=-1)
    w_in_p = jnp.concatenate([w_attn, _pad_lora_cols(w_in[..., ATTN_COLS:])], axis=-1).astype(BF16)
    blk = jnp.arange(RWKV_WIDTH) // RWKV_HEAD
    return dict(
        norm1_g=norm1_g[:, None, :], w_in=w_in_p, lambda_qk=lambda_qk, subln_g=subln_g[:, None, :],
        mu=_pad_lora_cols(shift_mu), w0=w0, w2=_pad_rows(w2, LORA_PAD).astype(BF16),
        a0=a0, a2=_pad_rows(a2, LORA_PAD).astype(BF16), g2=_pad_rows(g2, LORA_PAD).astype(BF16),
        k_k=k_k[:, None, :], k_a=k_a[:, None, :], r_k=r_k.reshape(DEPTH, 1, RWKV_WIDTH),
        lnx_w=lnx_w[:, None, :], lnx_b=lnx_b[:, None, :], w_out=w_out.astype(BF16),
        norm2_g=norm2_g[:, None, :], w_router_t=jnp.swapaxes(w_router, 1, 2),
        w_gate=w_gate.astype(BF16), w_up=w_up.astype(BF16), w_down=w_down.astype(BF16),
        final_g=final_g[None, :],
        seg=(blk[:, None] == blk[None, :]).astype(F32),
        scan_consts=scan_constants(),
    )


def trunk(x, p, tiles):
    batch, seq, _ = x.shape
    n = batch * seq
    x = x.reshape(n, D_MODEL)
    attn_consts = attention_constants(tiles["tq"])
    for l in range(DEPTH):
        lambda_init = 0.8 - 0.6 * math.exp(-0.3 * l)
        za, zr = in_proj(x, p["norm1_g"][l], p["w_in"][l], tiles["tm"])
        oa = diff_attention(za, attn_consts, p["lambda_qk"][l], p["subln_g"][l], batch, seq, lambda_init,
                            tiles["tq"])
        r, v, a, lw, kd, bb, bonus, gate = rwkv_prep(
            zr, p["mu"][l], p["w0"][l], p["w2"][l], p["a0"][l], p["a2"][l], p["g2"][l],
            p["k_k"][l], p["k_a"][l], p["r_k"][l], p["seg"], seq, tiles["tm"])
        y2 = rwkv_scan(r, v, a, lw, kd, bb, p["scan_consts"], batch, seq, tiles["tb"])
        x, h, aff_t = post_mix(y2, bonus, gate, oa, x, p["lnx_w"][l], p["lnx_b"][l], p["seg"],
                               p["w_out"][l], p["norm2_g"][l], p["w_router_t"][l], tiles["tm"])
        x = expert_choice_ffn(x, h, aff_t, p["w_gate"][l], p["w_up"][l], p["w_down"][l], p["final_g"],
                              tiles["tr"], final=(l == DEPTH - 1))
    return x.reshape(batch, seq, D_MODEL)


def _tiles(seq):
    return dict(tm=min(512, seq), tq=min(512, seq), tb=min(256, seq), tr=512)


def kernel(x_prompt, x_sample, norm1_g, w_in, lambda_qk, subln_g, shift_mu, w0, w2, a0, a2, g2, k_k, k_a, r_k,
           lnx_w, lnx_b, w_out, norm2_g, w_router, w_gate, w_up, w_down, final_g):
    p = prepare_weights(norm1_g, w_in, lambda_qk, subln_g, shift_mu, w0, w2, a0, a2, g2, k_k, k_a, r_k,
                        lnx_w, lnx_b, w_out, norm2_g, w_router, w_gate, w_up, w_down, final_g)
    y_prompt = trunk(x_prompt, p, _tiles(x_prompt.shape[1]))
    y_sample = trunk(x_sample, p, _tiles(x_sample.shape[1]))
    return (y_prompt, y_sample)
```

```python
import functools
import math

import jax
import jax.numpy as jnp
from jax import lax
from jax.experimental import pallas as pl
from jax.experimental.pallas import tpu as pltpu

F32 = jnp.float32
BF16 = jnp.bfloat16
HIGHEST = lax.Precision.HIGHEST

D_MODEL = 1024
DEPTH = 4
ATTN_WIDTH = 512
ATTN_HEADS = 4
ATTN_QK_DIM = 64
ATTN_V_DIM = 128
RWKV_WIDTH = 512
RWKV_HEAD = 64
RWKV_HEADS = 8
DECAY_LORA = 32
AAA_LORA = 32
GATE_LORA = 96
LORA_PAD = 128
ATTN_COLS = 3 * ATTN_WIDTH
RWKV_MAIN = 3 * RWKV_WIDTH
RWKV_COLS_PAD = RWKV_MAIN + 5 * LORA_PAD
N_EXPERTS = 16
EXPERT_FF = 2 * D_MODEL
CAPACITY_FACTOR = 2
NORM_EPS = 1e-6
GN_EPS = 64e-5

CHUNK = 64
GROUP = 4
GROUP_W = GROUP * RWKV_HEAD
N_LEVELS = CHUNK.bit_length() - 1
SCAN_BLOCKS_PER_STEP = 2
VMEM_LIMIT = 56 * 1024 * 1024


def _cparams(sem):
    return pltpu.CompilerParams(dimension_semantics=sem, vmem_limit_bytes=VMEM_LIMIT)


def _dot(a, b):
    return jnp.dot(a, b, preferred_element_type=F32)


def _dot_nt(a, b):
    return lax.dot_general(a, b, (((1,), (1,)), ((), ())), preferred_element_type=F32)


def _dot_tn(a, b):
    return lax.dot_general(a, b, (((0,), (0,)), ((), ())), preferred_element_type=F32)


def _dot_f32(a, b):
    return jnp.dot(a, b, preferred_element_type=F32, precision=HIGHEST)


def _in_proj_kernel(x_ref, g_ref, w_ref, za_ref, zr_ref):
    x = x_ref[...]
    ms = jnp.mean(x * x, axis=-1, keepdims=True)
    h = (x * lax.rsqrt(ms + NORM_EPS) * g_ref[...]).astype(BF16)
    za = _dot(h, w_ref[:, :ATTN_COLS]).astype(BF16)
    for j in range(ATTN_COLS // ATTN_V_DIM):
        za_ref[j] = za[:, j * ATTN_V_DIM:(j + 1) * ATTN_V_DIM]
    zr_ref[...] = _dot(h, w_ref[:, ATTN_COLS:])


def in_proj(x, g, w, tm):
    n = x.shape[0]
    cols = w.shape[1]
    return pl.pallas_call(
        _in_proj_kernel,
        grid=(n // tm,),
        in_specs=[
            pl.BlockSpec((tm, D_MODEL), lambda i: (i, 0)),
            pl.BlockSpec((1, D_MODEL), lambda i: (0, 0)),
            pl.BlockSpec((D_MODEL, cols), lambda i: (0, 0)),
        ],
        out_specs=[
            pl.BlockSpec((ATTN_COLS // ATTN_V_DIM, tm, ATTN_V_DIM), lambda i: (0, i, 0)),
            pl.BlockSpec((tm, RWKV_COLS_PAD), lambda i: (i, 0)),
        ],
        out_shape=[
            jax.ShapeDtypeStruct((ATTN_COLS // ATTN_V_DIM, n, ATTN_V_DIM), BF16),
            jax.ShapeDtypeStruct((n, RWKV_COLS_PAD), F32),
        ],
        compiler_params=_cparams(("parallel",)),
        name="in_proj",
    )(x, g, w)


ROW_BLOCK = 64
POS_BITS = 9
M_INIT = -1e30
QUERY_TILES_PER_STEP = 2


def _attn_kernel(kap_ref, lq_ref, q_ref, k_ref, v_ref, g_ref, qx_ref, kx_ref, dbias_ref, o_ref,
                 q2_ref, s_ref, p_ref, m_ref, al_ref, sh_ref, acc_ref, *, seq, tq, qsub, lambda_init):
    h = pl.program_id(1)
    nk = seq // tq
    kap = kap_ref[h]
    lax.fori_loop(0, qsub, functools.partial(
        _attn_query_tile, pl.program_id(2) * qsub, kap, lq_ref, q_ref, k_ref, v_ref, g_ref, qx_ref, kx_ref,
        dbias_ref, o_ref, q2_ref, s_ref, p_ref, m_ref, al_ref, sh_ref, acc_ref, nk, tq, lambda_init), 0)


def _attn_query_tile(i0, kap, lq_ref, q_ref, k_ref, v_ref, g_ref, qx_ref, kx_ref, dbias_ref, o_ref,
                     q2_ref, s_ref, p_ref, m_ref, al_ref, sh_ref, acc_ref, nk, tq, lambda_init, sub, carry):
    i = i0 + sub
    qrows = pl.ds(pl.multiple_of(sub * tq, tq), tq)
    q = q_ref[qrows, :]
    lane = lax.broadcasted_iota(jnp.int32, q.shape, 1)
    zero = jnp.zeros_like(q)
    q2 = jnp.concatenate([jnp.where(lane < ATTN_QK_DIM, q, zero),
                          jnp.where(lane >= ATTN_QK_DIM, q, zero)], axis=0)
    qx = qx_ref[...]
    qx2 = jnp.concatenate([qx, qx], axis=0)
    q2_ref[0] = jnp.concatenate([q2, qx2], axis=1)
    q2_ref[1] = jnp.concatenate([q2, -qx2], axis=1)
    kx = kx_ref[...]
    ones = jnp.ones((tq, ATTN_V_DIM), BF16)
    m_ref[...] = jnp.full(m_ref.shape, M_INIT, F32)
    acc_ref[...] = jnp.zeros(acc_ref.shape, F32)

    def tile_of(step):
        jj = step - 1
        return jnp.where(step == 0, i, jj + (jj >= i).astype(jnp.int32))

    def scores(step, buf):
        j = tile_of(step)
        start = pl.multiple_of(j * tq, tq)
        k_aug = jnp.concatenate([k_ref[pl.ds(start, tq), :], kx], axis=1)
        s_ref[buf] = _dot_nt(q2_ref[(j > i).astype(jnp.int32)], k_aug)

    def softmax_and_values(step, buf):
        j = tile_of(step)
        kappa = -kap * jnp.abs(i - j).astype(F32)
        start = pl.multiple_of(j * tq, tq)
        v_aug = jnp.concatenate([v_ref[pl.ds(start, tq), :], ones], axis=1)
        for rb in range(2 * tq // ROW_BLOCK):
            rows = slice(rb * ROW_BLOCK, (rb + 1) * ROW_BLOCK)
            m_old = m_ref[rows, :]
            m_new = jnp.maximum(m_old, jnp.max(s_ref[buf, rows, :], axis=-1, keepdims=True) + kappa)
            al_ref[rows, :] = jnp.exp2(m_old - m_new)
            sh_ref[rows, :] = m_new - kappa
            m_ref[rows, :] = m_new
        for rb in range(2 * tq // ROW_BLOCK):
            rows = slice(rb * ROW_BLOCK, (rb + 1) * ROW_BLOCK)
            p_ref[rows, :] = jnp.exp2(s_ref[buf, rows, :] - sh_ref[rows, :]).astype(BF16)
        acc_ref[...] = al_ref[...] * acc_ref[...] + _dot(p_ref[...], v_aug)

    dstart = pl.multiple_of(i * tq, tq)
    s_diag = _dot_nt(q2, k_ref[pl.ds(dstart, tq), :])
    s_ref[0] = (s_diag.reshape(2, tq, tq) + dbias_ref[...][None]).reshape(2 * tq, tq)

    def pair(t, carry):
        scores(2 * t + 1, 1)
        softmax_and_values(2 * t, 0)
        scores(jnp.minimum(2 * t + 2, nk - 1), 0)
        softmax_and_values(2 * t + 1, 1)
        return carry

    lax.fori_loop(0, nk // 2, pair, 0)

    lq = lq_ref[...]
    lam = (jnp.exp(jnp.sum(lq[0:1] * lq[1:2], axis=-1, keepdims=True))
           - jnp.exp(jnp.sum(lq[2:3] * lq[3:4], axis=-1, keepdims=True)) + lambda_init)
    acc = acc_ref[...]
    o_all = acc[:, :ATTN_V_DIM] / acc[:, ATTN_V_DIM:]
    o = o_all[:tq] - lam * o_all[tq:]
    ms = jnp.mean(o * o, axis=-1, keepdims=True)
    o = o * lax.rsqrt(ms + NORM_EPS) * g_ref[...] * (1.0 - lambda_init)
    o_ref[qrows, :] = o.astype(BF16)
    return carry


def attention_constants(tq):
    assert tq <= (1 << POS_BITS)
    slopes = 2.0 ** (-8.0 * jnp.arange(1, ATTN_HEADS + 1, dtype=F32) / ATTN_HEADS)
    c = slopes * math.log2(math.e)
    cb = c[:, None] * (2.0 ** jnp.arange(POS_BITS, dtype=F32))[None, :]
    hi = cb.astype(BF16).astype(F32)
    lo = (cb - hi).astype(BF16).astype(F32)
    pos = jnp.arange(tq)
    bits = ((pos[:, None] >> jnp.arange(POS_BITS)[None, :]) & 1).astype(F32)
    hb = jnp.broadcast_to(bits[None], (ATTN_HEADS, tq, POS_BITS))
    const = lambda x: jnp.broadcast_to(x[:, None, :], (ATTN_HEADS, tq, POS_BITS))
    pad = jnp.zeros((ATTN_HEADS, tq, ATTN_V_DIM - 4 * POS_BITS), F32)
    qx = jnp.concatenate([const(hi), const(lo), hb, hb, pad], axis=-1).astype(BF16)
    kx = jnp.concatenate([hb, hb, const(-hi), const(-lo), pad], axis=-1).astype(BF16)
    dist = jnp.abs(pos[:, None] - pos[None, :]).astype(F32)
    dbias = -c[:, None, None] * dist[None]
    kap = c * tq
    return kap, qx, kx, dbias


def diff_attention(za, consts, lq, subln_g, batch, seq, lambda_init, tq):
    n = batch * seq
    nq = seq // tq
    kap, qx, kx, dbias = consts
    assert nq % 2 == 0, "key tiles are processed in pairs"
    qsub = QUERY_TILES_PER_STEP if nq % QUERY_TILES_PER_STEP == 0 else 1
    nq //= qsub
    kern = functools.partial(_attn_kernel, seq=seq, tq=tq, qsub=qsub, lambda_init=lambda_init)
    per_head = lambda w: pl.BlockSpec((None, tq, w), lambda b, h, i: (h, 0, 0))
    return pl.pallas_call(
        kern,
        grid=(batch, ATTN_HEADS, nq),
        in_specs=[
            pl.BlockSpec(memory_space=pltpu.SMEM),
            pl.BlockSpec((4, ATTN_QK_DIM), lambda b, h, i: (0, 0)),
            pl.BlockSpec((None, qsub * tq, ATTN_V_DIM), lambda b, h, i: (h, b * nq + i, 0)),
            pl.BlockSpec((None, seq, ATTN_V_DIM), lambda b, h, i: (ATTN_HEADS + h, b, 0)),
            pl.BlockSpec((None, seq, ATTN_V_DIM), lambda b, h, i: (2 * ATTN_HEADS + h, b, 0)),
            pl.BlockSpec((1, ATTN_V_DIM), lambda b, h, i: (0, 0)),
            per_head(ATTN_V_DIM), per_head(ATTN_V_DIM), per_head(tq),
        ],
        out_specs=pl.BlockSpec((None, qsub * tq, ATTN_V_DIM), lambda b, h, i: (h, b * nq + i, 0)),
        out_shape=jax.ShapeDtypeStruct((ATTN_HEADS, n, ATTN_V_DIM), BF16),
        scratch_shapes=[
            pltpu.VMEM((2, 2 * tq, 2 * ATTN_V_DIM), BF16),
            pltpu.VMEM((2, 2 * tq, tq), F32),
            pltpu.VMEM((2 * tq, tq), BF16),
            pltpu.VMEM((2 * tq, 1), F32),
            pltpu.VMEM((2 * tq, 1), F32),
            pltpu.VMEM((2 * tq, 1), F32),
            pltpu.VMEM((2 * tq, 2 * ATTN_V_DIM), F32),
        ],
        compiler_params=_cparams(("parallel", "parallel", "parallel")),
        name="diff_attention",
    )(kap, lq, za, za, za, subln_g, qx, kx, dbias)


def _softplus(x):
    return jnp.maximum(x, 0.0) + jnp.log(1.0 + jnp.exp(-jnp.abs(x)))


def _sigmoid(x):
    return 1.0 / (1.0 + jnp.exp(-x))


def _prep_kernel(z_ref, zp_ref, zn_ref, mu_ref, w0_ref, w2_ref, a0_ref, a2_ref, g2_ref,
                 kk_ref, ka_ref, rk_ref, seg_ref,
                 r_ref, v_ref, a_ref, lw_ref, kd_ref, bb_ref, bonus_ref, gate_ref, *, seq, tm):
    i = pl.program_id(0)
    z = z_ref[...]
    row = lax.broadcasted_iota(jnp.int32, z.shape, 0)
    first = (i * tm) % seq == 0
    last = ((i + 1) * tm) % seq == 0
    zp = jnp.where(first, 0.0, zp_ref[7:8, :])
    zn = jnp.where(last, 0.0, zn_ref[0:1, :])
    prev = jnp.where(row == 0, zp, pltpu.roll(z, 1, 0))
    nxt = jnp.where(row == tm - 1, zn, pltpu.roll(z, tm - 1, 0))
    z = z + mu_ref[0:1, :] * (prev - z) + mu_ref[1:2, :] * (nxt - z)

    w = RWKV_WIDTH
    r = z[:, 0:w]
    k = z[:, w:2 * w]
    v = z[:, 2 * w:3 * w]
    seg = seg_ref[...]
    kk = k * kk_ref[...]
    ss = _dot_f32(kk * kk, seg)
    kk = kk / jnp.maximum(jnp.sqrt(ss), 1e-12)
    r_ref[...] = r.astype(BF16)
    v_ref[...] = v.astype(BF16)
    a_ref[...] = (-kk).astype(BF16)
    for d in range(2):
        xw = z[:, RWKV_MAIN + d * LORA_PAD:RWKV_MAIN + (d + 1) * LORA_PAD]
        xa = z[:, RWKV_MAIN + (2 + d) * LORA_PAD:RWKV_MAIN + (3 + d) * LORA_PAD]
        wlog = -_softplus(-(w0_ref[d:d + 1, :] + _dot(jnp.tanh(xw).astype(BF16), w2_ref[d]))) - 0.5
        lw_ref[d] = -jnp.exp(wlog)
        iclr = _sigmoid(a0_ref[d:d + 1, :] + _dot(xa.astype(BF16), a2_ref[d]))
        kd_ref[d] = (k * (1.0 + (iclr - 1.0) * ka_ref[...])).astype(BF16)
        bb_ref[d] = (kk * iclr).astype(BF16)
    xg = z[:, RWKV_MAIN + 4 * LORA_PAD:RWKV_MAIN + 5 * LORA_PAD]
    gate_ref[...] = _dot(_sigmoid(xg).astype(BF16), g2_ref[...]).astype(BF16)
    bonus_ref[...] = (_dot_f32(r * k * rk_ref[...], seg) * v).astype(BF16)


def rwkv_prep(zr, mu, w0, w2, a0, a2, g2, k_k, k_a, r_k, seg, seq, tm):
    n, cols = zr.shape
    nb8 = n // 8
    tb8 = tm // 8
    full = lambda shape: pl.BlockSpec(shape, lambda i: (0,) * len(shape))
    tok = pl.BlockSpec((tm, RWKV_WIDTH), lambda i: (i, 0))
    tok2 = pl.BlockSpec((2, tm, RWKV_WIDTH), lambda i: (0, i, 0))
    sds = jax.ShapeDtypeStruct((n, RWKV_WIDTH), BF16)
    sds2 = jax.ShapeDtypeStruct((2, n, RWKV_WIDTH), BF16)
    sds2_f32 = jax.ShapeDtypeStruct((2, n, RWKV_WIDTH), F32)
    return pl.pallas_call(
        functools.partial(_prep_kernel, seq=seq, tm=tm),
        grid=(n // tm,),
        in_specs=[
            pl.BlockSpec((tm, cols), lambda i: (i, 0)),
            pl.BlockSpec((8, cols), lambda i: (jnp.maximum(i * tb8 - 1, 0), 0)),
            pl.BlockSpec((8, cols), lambda i: (jnp.minimum((i + 1) * tb8, nb8 - 1), 0)),
            full((2, cols)), full((2, RWKV_WIDTH)), full((2, LORA_PAD, RWKV_WIDTH)),
            full((2, RWKV_WIDTH)), full((2, LORA_PAD, RWKV_WIDTH)), full((LORA_PAD, RWKV_WIDTH)),
            full((1, RWKV_WIDTH)), full((1, RWKV_WIDTH)), full((1, RWKV_WIDTH)),
            full((RWKV_WIDTH, RWKV_WIDTH)),
        ],
        out_specs=[tok, tok, tok, tok2, tok2, tok2, tok, tok],
        out_shape=[sds, sds, sds, sds2_f32, sds2, sds2, sds, sds],
        compiler_params=_cparams(("parallel",)),
        name="rwkv_prep",
    )(zr, zr, zr, mu, w0, w2, a0, a2, g2, k_k, k_a, r_k, seg)


def _scan_kernel(r_ref, v_ref, a_ref, lw_ref, k_ref, b_ref, tri_ref, strict_ref, incl_ref, lvl_ref, eye_ref, bd_ref,
                 y_ref, h_ref, *, tb, nsub):
    d = pl.program_id(1)
    i = pl.program_id(2)
    nc = tb // CHUNK
    ngrp = RWKV_WIDTH // GROUP_W

    @pl.when(i == 0)
    def _():
        h_ref[...] = jnp.zeros(h_ref.shape, F32)

    tri = tri_ref[...]
    strict = strict_ref[...] > 0.5
    incl = incl_ref[...] > 0.5
    eye = eye_ref[...]
    bdm = bd_ref[...] > 0.5
    ones16 = jnp.ones((16, GROUP_W), BF16)
    row16 = lax.broadcasted_iota(jnp.int32, (16, GROUP_W), 0)

    def bd(x):
        return jnp.where(bdm, jnp.concatenate([x] * GROUP, axis=0), 0.0).astype(BF16)

    def split(x):
        hi = x.astype(BF16)
        return hi, (x - hi.astype(F32)).astype(BF16)

    lax.fori_loop(0, nsub, functools.partial(
        _scan_block, d, nsub, tb, nc, ngrp, bd, split, tri, strict, incl, eye, bdm, ones16, row16,
        r_ref, v_ref, a_ref, lw_ref, k_ref, b_ref, lvl_ref, y_ref, h_ref), 0)


def _scan_block(d, nsub, tb, nc, ngrp, bd, split, tri, strict, incl, eye, bdm, ones16, row16,
                r_ref, v_ref, a_ref, lw_ref, k_ref, b_ref, lvl_ref, y_ref, h_ref, sub, carry):
    base = (sub + d * (nsub - 1 - 2 * sub)) * tb
    slices = []
    for cc in range(nc):
        ci = cc + d * (nc - 1 - 2 * cc)
        slices.append(pl.ds(pl.multiple_of(base + ci * CHUNK, CHUNK), CHUNK))
    groups = [slice(g * GROUP_W, (g + 1) * GROUP_W) for g in range(ngrp)]
    units = [dict(sl=sl, cols=cols) for sl in slices for cols in groups]

    for c in units:
        lw = lw_ref[c["sl"], c["cols"]]
        r = r_ref[c["sl"], c["cols"]].astype(F32)
        v = v_ref[c["sl"], c["cols"]].astype(F32)
        a = a_ref[c["sl"], c["cols"]].astype(F32)
        k = k_ref[c["sl"], c["cols"]].astype(F32)
        b = b_ref[c["sl"], c["cols"]].astype(F32)
        hi, lo = split(lw)
        cs = _dot(tri, jnp.concatenate([hi, lo], axis=1))
        lcum = cs[:, :GROUP_W] + cs[:, GROUP_W:]
        ltot = jnp.where(d == 0, lcum[CHUNK - 1:CHUNK], lcum[0:1])
        at = a * jnp.exp(lcum - lw)
        rt = r * jnp.exp(lcum)
        einv = jnp.exp(-lcum)
        eend = jnp.exp(ltot - lcum)
        c["x"] = jnp.concatenate([at, rt], axis=0).astype(BF16)
        c["ybk"] = jnp.concatenate([bd(b * einv), bd(k * einv)], axis=0)
        c["bkw"] = jnp.concatenate([b * eend, k * eend], axis=0).astype(BF16)
        c["vbd"] = bd(v)
        c["v"] = v.astype(BF16)
        wtot = jnp.exp(ltot)
        whi = wtot.astype(BF16).astype(F32)
        c["whl"] = jnp.where(row16 == 0, whi, jnp.where(row16 == 1, wtot - whi, 0.0)).astype(BF16)
    for c in units:
        amat = _dot_nt(c["x"], c.pop("ybk"))
        c["aab"] = jnp.where(strict, amat[:CHUNK, :GROUP_W], 0.0)
        c["aak"] = jnp.where(strict, amat[:CHUNK, GROUP_W:], 0.0).astype(BF16)
        c["ar"] = jnp.concatenate([jnp.where(incl, amat[CHUNK:, :GROUP_W], 0.0),
                                   jnp.where(incl, amat[CHUNK:, GROUP_W:], 0.0)], axis=1).astype(BF16)
        c["p"] = eye + jnp.where(lvl_ref[0] > 0.5, c["aab"], 0.0)
    for lvl in range(1, N_LEVELS):
        for c in units:
            e = jnp.where(lvl_ref[lvl] > 0.5, c["aab"], 0.0)
            c["g"] = _dot(c["p"].astype(BF16), bd(e)).astype(BF16)
        for c in units:
            c["p"] = c["p"] + _dot(c.pop("g"), bd(c["p"]))
    for c in units:
        c["p"] = c["p"].astype(BF16)
        c["av"] = _dot(c.pop("aak"), c["vbd"])
        c["wcol"] = _dot_tn(c.pop("whl"), ones16)

    for cc in range(nc):
        cu = units[cc * ngrp:(cc + 1) * ngrp]
        hs = [h_ref[g] for g in range(ngrp)]
        hx = [_dot(c["x"], h.astype(BF16)) for c, h in zip(cu, hs)]
        us = [_dot(c["p"], bd(x[:CHUNK] + c["av"])) for c, x in zip(cu, hx)]
        for g, (c, h, x, u) in enumerate(zip(cu, hs, hx, us)):
            y_ref[c["sl"], c["cols"]] = x[CHUNK:] + _dot(c["ar"], jnp.concatenate([bd(u), c["vbd"]], axis=0))
            upd = _dot_tn(c["bkw"], jnp.concatenate([u.astype(BF16), c["v"]], axis=0))
            h_ref[g] = c["wcol"] * h + jnp.where(bdm, upd, 0.0)
    return carry


def rwkv_scan(r, v, a, lw, kd, bb, consts, batch, seq, tb):
    n = batch * seq
    nsub = SCAN_BLOCKS_PER_STEP if seq % (tb * SCAN_BLOCKS_PER_STEP) == 0 else 1
    nblk = seq // (tb * nsub)
    tri, strict, incl, lvl, eye, bdm = consts

    def tmap(b, d, i):
        return (b * nblk + i + d * (nblk - 1 - 2 * i), 0)

    def tmap_d(b, d, i):
        return (d, b * nblk + i + d * (nblk - 1 - 2 * i), 0)

    tok = pl.BlockSpec((nsub * tb, RWKV_WIDTH), tmap)
    tok_d = pl.BlockSpec((None, nsub * tb, RWKV_WIDTH), tmap_d)
    per_d = lambda shape: pl.BlockSpec((None,) + shape, lambda b, d, i: (d,) + (0,) * len(shape))
    full = lambda shape: pl.BlockSpec(shape, lambda b, d, i: (0, 0))
    return pl.pallas_call(
        functools.partial(_scan_kernel, tb=tb, nsub=nsub),
        grid=(batch, 2, nblk),
        in_specs=[tok, tok, tok, tok_d, tok_d, tok_d,
                  per_d((CHUNK, CHUNK)), per_d((CHUNK, GROUP * CHUNK)), per_d((CHUNK, GROUP * CHUNK)),
                  per_d((N_LEVELS, CHUNK, GROUP * CHUNK)),
                  full((CHUNK, GROUP * CHUNK)), full((GROUP_W, GROUP_W))],
        out_specs=tok_d,
        out_shape=jax.ShapeDtypeStruct((2, n, RWKV_WIDTH), F32),
        scratch_shapes=[pltpu.VMEM((RWKV_WIDTH // GROUP_W, GROUP_W, GROUP_W), F32)],
        compiler_params=_cparams(("parallel", "parallel", "arbitrary")),
        name="rwkv_scan",
    )(r, v, a, lw, kd, bb, tri, strict, incl, lvl, eye, bdm)


def scan_constants():
    t = jnp.arange(CHUNK)
    before_eq = (t[None, :] <= t[:, None])
    tri = jnp.stack([before_eq, before_eq.T]).astype(F32)
    incl = jnp.tile(tri, (1, 1, GROUP))
    tri = tri.astype(BF16)
    eye = jnp.tile(jnp.eye(CHUNK, dtype=F32), (1, GROUP))
    strict = incl - eye[None]
    levels = []
    for lv in range(N_LEVELS):
        half = t // (1 << lv)
        fwd = ((half[:, None] // 2 == half[None, :] // 2) & (half[:, None] % 2 == 1) & (half[None, :] % 2 == 0))
        levels.append(jnp.stack([fwd, fwd.T]))
    lvl = jnp.tile(jnp.stack(levels, axis=1).astype(F32), (1, 1, 1, GROUP))
    blk = jnp.arange(GROUP_W) // RWKV_HEAD
    bdm = (blk[:, None] == blk[None, :]).astype(F32)
    return tri, strict, incl, lvl, eye, bdm


def _post_kernel(yf_ref, yb_ref, bonus_ref, gate_ref, oa_ref, x_ref, lw_ref, lb_ref, seg_ref,
                 wo_ref, g2_ref, wr_ref, xo_ref, h_ref, aff_ref):
    y = yf_ref[...] + yb_ref[...]
    seg = seg_ref[...]
    mean = _dot_f32(y, seg) * (1.0 / RWKV_HEAD)
    yc = y - mean
    var = _dot_f32(yc * yc, seg) * (1.0 / RWKV_HEAD)
    yn = yc * lax.rsqrt(var + GN_EPS) * lw_ref[...] + lb_ref[...]
    orw = ((yn + bonus_ref[...]) * gate_ref[...]).astype(BF16)
    oa = jnp.concatenate([oa_ref[j] for j in range(ATTN_HEADS)], axis=1)
    x = (x_ref[...] + _dot(oa, wo_ref[:ATTN_WIDTH, :]) + _dot(orw, wo_ref[ATTN_WIDTH:, :]))
    xo_ref[...] = x
    ms = jnp.mean(x * x, axis=-1, keepdims=True)
    h = x * lax.rsqrt(ms + NORM_EPS) * g2_ref[...]
    h_ref[...] = h.astype(BF16)
    logits = lax.dot_general(wr_ref[...], h, (((1,), (1,)), ((), ())),
                             preferred_element_type=F32, precision=HIGHEST)
    logits = logits - jnp.max(logits, axis=0, keepdims=True)
    e = jnp.exp(logits)
    aff_ref[...] = e / jnp.sum(e, axis=0, keepdims=True)


def post_mix(y2, bonus, gate, oa, x, lnx_w, lnx_b, seg, w_out, norm2_g, w_router_t, tm):
    n = x.shape[0]
    full = lambda shape: pl.BlockSpec(shape, lambda i: (0,) * len(shape))
    tok = lambda w: pl.BlockSpec((tm, w), lambda i: (i, 0))
    return pl.pallas_call(
        _post_kernel,
        grid=(n // tm,),
        in_specs=[
            pl.BlockSpec((None, tm, RWKV_WIDTH), lambda i: (0, i, 0)),
            pl.BlockSpec((None, tm, RWKV_WIDTH), lambda i: (1, i, 0)),
            tok(RWKV_WIDTH), tok(RWKV_WIDTH),
            pl.BlockSpec((ATTN_HEADS, tm, ATTN_V_DIM), lambda i: (0, i, 0)), tok(D_MODEL),
            full((1, RWKV_WIDTH)), full((1, RWKV_WIDTH)), full((RWKV_WIDTH, RWKV_WIDTH)),
            full((D_MODEL, D_MODEL)), full((1, D_MODEL)), full((N_EXPERTS, D_MODEL)),
        ],
        out_specs=[tok(D_MODEL), tok(D_MODEL), pl.BlockSpec((N_EXPERTS, tm), lambda i: (0, i))],
        out_shape=[
            jax.ShapeDtypeStruct((n, D_MODEL), F32),
            jax.ShapeDtypeStruct((n, D_MODEL), BF16),
            jax.ShapeDtypeStruct((N_EXPERTS, n), F32),
        ],
        compiler_params=_cparams(("parallel",)),
        name="post_mix",
    )(y2, y2, bonus, gate, oa, x, lnx_w, lnx_b, seg, w_out, norm2_g, w_router_t)


COMBINE_RANKS = 256
GATHER_TOKENS = 1024


def _thresh_kernel(aff_ref, thr_ref, need_ref, *, cap):
    bits = pltpu.bitcast(aff_ref[...], jnp.int32)
    e = bits.shape[0]

    def body(it, thr):
        cand = thr | jnp.left_shift(jnp.int32(1), 30 - it)
        cnt = jnp.sum(jnp.where(bits >= cand, 1.0, 0.0), axis=1, keepdims=True)
        return jnp.where(cnt >= cap, cand, thr)

    thr = lax.fori_loop(0, 31, body, jnp.zeros((e, 1), jnp.int32))
    n_gt = jnp.sum(jnp.where(bits > thr, 1.0, 0.0), axis=1, keepdims=True)
    thr_ref[...] = jnp.broadcast_to(thr, thr_ref.shape)
    need_ref[...] = jnp.broadcast_to(cap - n_gt, need_ref.shape)


def _prefix_kernel(aff_ref, thr_ref, need_ref, u_ref, pos_ref, cend_ref, carry_ref):
    @pl.when(pl.program_id(0) == 0)
    def _():
        carry_ref[...] = jnp.zeros(carry_ref.shape, F32)

    bits = pltpu.bitcast(aff_ref[...], jnp.int32)
    e, tr = bits.shape
    thr = thr_ref[:, 0:1]
    need = need_ref[:, 0:1]
    gt = bits > thr
    eq = bits == thr
    marks = jnp.concatenate([jnp.where(gt, 1.0, 0.0), jnp.where(eq, 1.0, 0.0)], axis=0).astype(BF16)
    pre = _dot(marks, u_ref[...]) + carry_ref[...]
    pgt = pre[:e]
    peq = pre[e:]
    sel = jnp.where(gt, 1.0, jnp.where(eq, jnp.where(peq <= need, 1.0, 0.0), 0.0))
    rank = pgt + jnp.minimum(peq, need) - 1.0
    pos_ref[...] = jnp.where(sel > 0.5, rank, -1.0).astype(jnp.int32)
    carry_ref[...] = pre[:, tr - 1:tr]
    cend_ref[...] = jnp.broadcast_to(rank[:, tr - 1:tr] + 1.0, cend_ref.shape)


def select_tokens(aff_t, cap, tr):
    e, n = aff_t.shape
    nb = n // tr
    thr, need = pl.pallas_call(
        functools.partial(_thresh_kernel, cap=float(cap)),
        out_shape=[jax.ShapeDtypeStruct((e, 128), jnp.int32), jax.ShapeDtypeStruct((e, 128), F32)],
        compiler_params=pltpu.CompilerParams(vmem_limit_bytes=VMEM_LIMIT),
        name="route_threshold",
    )(aff_t)
    upper = (jnp.arange(tr)[:, None] <= jnp.arange(tr)[None, :]).astype(BF16)
    pos, cend = pl.pallas_call(
        _prefix_kernel,
        grid=(nb,),
        in_specs=[pl.BlockSpec((e, tr), lambda i: (0, i)),
                  pl.BlockSpec((e, 128), lambda i: (0, 0)),
                  pl.BlockSpec((e, 128), lambda i: (0, 0)),
                  pl.BlockSpec((tr, tr), lambda i: (0, 0))],
        out_specs=[pl.BlockSpec((e, tr), lambda i: (0, i)),
                   pl.BlockSpec((None, e, 128), lambda i: (i, 0, 0))],
        out_shape=[jax.ShapeDtypeStruct((e, n), jnp.int32), jax.ShapeDtypeStruct((nb, e, 128), F32)],
        scratch_shapes=[pltpu.VMEM((2 * e, 1), F32)],
        compiler_params=_cparams(("arbitrary",)),
        name="route_prefix",
    )(aff_t, thr, need, upper)
    return pos, cend[:, :, 0].T.astype(jnp.int32)


def _pick(onehot, values):
    return jnp.sum(jnp.where(onehot, values[None, :], 0), axis=1)


def gather_table(cend, rb, n_rb):
    nb = cend.shape[0]
    cstart = jnp.concatenate([jnp.zeros((1,), jnp.int32), cend[:-1]])
    count = cend - cstart
    ob_first = jnp.minimum(cstart // rb, n_rb - 1)
    ob_last = jnp.where(count > 0, (cend - 1) // rb, ob_first)
    nitems = jnp.where(count > 0, ob_last - ob_first + 1, 0)
    cum = jnp.cumsum(nitems)
    start = cum - nitems
    w = jnp.arange(nb + n_rb, dtype=jnp.int32)
    owner = (w[:, None] >= start[None, :]) & (w[:, None] < cum[None, :])
    valid = w < cum[-1]
    tb = jnp.where(valid, _pick(owner, jnp.arange(nb, dtype=jnp.int32)), nb - 1)
    ob = jnp.where(valid, _pick(owner, ob_first - start) + w, n_rb - 1)
    prev_ob = jnp.concatenate([jnp.full((1,), -1, jnp.int32), ob[:-1]])
    next_ob = jnp.concatenate([ob[1:], jnp.full((1,), -1, jnp.int32)])
    next_valid = jnp.concatenate([valid[1:], jnp.zeros((1,), bool)])
    first = valid & (ob != prev_ob)
    last = valid & ((ob != next_ob) | ~next_valid)
    flags = first.astype(jnp.int32) + 2 * last.astype(jnp.int32) + 4 * valid.astype(jnp.int32)
    return ob, tb, flags


def routing_tables(cend, rb, n_rb):
    ob, tb, flags = jax.vmap(lambda c: gather_table(c, rb, n_rb))(cend)
    return ob.reshape(-1), tb.reshape(-1), flags.reshape(-1)


def combine_table(cend, rb, n_rb):
    e, nb = cend.shape
    cstart = jnp.concatenate([jnp.zeros((e, 1), jnp.int32), cend[:, :-1]], axis=1)
    ob_first = jnp.minimum(cstart // rb, n_rb - 1)
    ob_last = jnp.where(cend > cstart, (cend - 1) // rb, ob_first)
    obf = ob_first.T.reshape(-1)
    nitems = (ob_last - ob_first + 1).T.reshape(-1)
    cum = jnp.cumsum(nitems)
    start = cum - nitems
    length = nb * e + e * n_rb
    w = jnp.arange(length, dtype=jnp.int32)
    owner = (w[:, None] >= start[None, :]) & (w[:, None] < cum[None, :])
    valid = w < cum[-1]
    pair = jnp.where(valid, _pick(owner, jnp.arange(nb * e, dtype=jnp.int32)), nb * e - 1)
    ob = jnp.where(valid, _pick(owner, obf - start) + w, obf[-1] + nitems[-1] - 1)
    ex = pair % e
    first = valid & (_pick(owner, start) == w) & (ex == 0)
    last = valid & (_pick(owner, cum) - 1 == w) & (ex == e - 1)
    flags = first.astype(jnp.int32) + 2 * last.astype(jnp.int32) + 4 * valid.astype(jnp.int32)
    return pair // e, ex, ob, flags


def _gather_ffn_kernel(ob_ref, tb_ref, fl_ref, pos_ref, aff_ref, h_ref, wg_ref, wu_ref, wd_ref, o_ref,
                       x_acc, g_acc, *, length, rb, tg):
    idx = pl.program_id(0) * length + pl.program_id(1)
    flags = fl_ref[idx]
    ob = ob_ref[idx]

    @pl.when((flags & 1) != 0)
    def _():
        x_acc[...] = jnp.zeros(x_acc.shape, F32)
        g_acc[...] = jnp.zeros(g_acc.shape, F32)

    @pl.when((flags & 4) != 0)
    def _():
        rank = ob * rb + lax.broadcasted_iota(jnp.int32, (rb, tg), 0)
        match = pos_ref[...] == rank
        x_acc[...] += _dot(jnp.where(match, 1.0, 0.0).astype(BF16), h_ref[...])
        g_acc[...] += jnp.sum(jnp.where(match, aff_ref[...], 0.0), axis=1, keepdims=True)

    @pl.when((flags & 2) != 0)
    def _():
        x = x_acc[...].astype(BF16)
        hg = _dot(x, wg_ref[...])
        hu = _dot(x, wu_ref[...])
        act = (hg * _sigmoid(hg) * hu).astype(BF16)
        o_ref[...] = (_dot(act, wd_ref[...]) * g_acc[...]).astype(BF16)


def gather_ffn(tables, pos3, aff3, h, wg, wu, wd, cap, rb, tg):
    e = pos3.shape[0]
    n = h.shape[0]
    length = n // tg + cap // rb
    ob_of, tb_of, flags = tables
    tok = pl.BlockSpec((None, 1, tg), lambda e, w, ob, tb, fl: (e, 0, tb[e * length + w]))
    wspec = lambda r, c: pl.BlockSpec((None, r, c), lambda e, w, ob, tb, fl: (e, 0, 0))
    return pl.pallas_call(
        functools.partial(_gather_ffn_kernel, length=length, rb=rb, tg=tg),
        grid_spec=pltpu.PrefetchScalarGridSpec(
            num_scalar_prefetch=3,
            grid=(e, length),
            in_specs=[tok, tok,
                      pl.BlockSpec((tg, D_MODEL), lambda e, w, ob, tb, fl: (tb[e * length + w], 0)),
                      wspec(D_MODEL, EXPERT_FF), wspec(D_MODEL, EXPERT_FF), wspec(EXPERT_FF, D_MODEL)],
            out_specs=pl.BlockSpec((None, rb, D_MODEL), lambda e, w, ob, tb, fl: (e, ob[e * length + w], 0)),
            scratch_shapes=[pltpu.VMEM((rb, D_MODEL), F32), pltpu.VMEM((rb, 1), F32)],
        ),
        out_shape=jax.ShapeDtypeStruct((e, cap, D_MODEL), BF16),
        compiler_params=_cparams(("parallel", "arbitrary")),
        name="gather_ffn",
    )(ob_of, tb_of, flags, pos3, aff3, h, wg, wu, wd)


def _combine_kernel(tb_ref, ex_ref, ob_ref, fl_ref, pos_ref, ye_ref, x_ref, g_ref, o_ref, *, rb, tr, final):
    w = pl.program_id(0)
    flags = fl_ref[w]

    @pl.when((flags & 1) != 0)
    def _():
        o_ref[...] = x_ref[...]

    @pl.when((flags & 4) != 0)
    def _():
        rank = ob_ref[w] * rb + lax.broadcasted_iota(jnp.int32, (rb, tr), 0)
        onehot = jnp.where(pos_ref[...] == rank, 1.0, 0.0).astype(BF16)
        o_ref[...] += _dot_tn(onehot, ye_ref[...])

    if final:
        @pl.when((flags & 2) != 0)
        def _():
            x = o_ref[...]
            ms = jnp.mean(x * x, axis=-1, keepdims=True)
            o_ref[...] = x * lax.rsqrt(ms + NORM_EPS) * g_ref[...]


def combine(table, pos3, ye, x, final_g, tr, rb, final):
    n = x.shape[0]
    tb_of, ex_of, ob_of, flags = table
    return pl.pallas_call(
        functools.partial(_combine_kernel, rb=rb, tr=tr, final=final),
        grid_spec=pltpu.PrefetchScalarGridSpec(
            num_scalar_prefetch=4,
            grid=(tb_of.shape[0],),
            in_specs=[pl.BlockSpec((None, 1, tr), lambda w, tb, ex, ob, fl: (ex[w], 0, tb[w])),
                      pl.BlockSpec((None, rb, D_MODEL), lambda w, tb, ex, ob, fl: (ex[w], ob[w], 0)),
                      pl.BlockSpec((tr, D_MODEL), lambda w, tb, ex, ob, fl: (tb[w], 0)),
                      pl.BlockSpec((1, D_MODEL), lambda w, tb, ex, ob, fl: (0, 0))],
            out_specs=pl.BlockSpec((tr, D_MODEL), lambda w, tb, ex, ob, fl: (tb[w], 0)),
        ),
        out_shape=jax.ShapeDtypeStruct((n, D_MODEL), F32),
        compiler_params=_cparams(("arbitrary",)),
        name="combine",
    )(tb_of, ex_of, ob_of, flags, pos3, ye, x, final_g)


def expert_choice_ffn(x, h, aff_t, wg, wu, wd, final_g, tr, final):
    e, n = aff_t.shape
    cap = CAPACITY_FACTOR * n // e
    pos, cend = select_tokens(aff_t, cap, tr)
    pos3 = pos.reshape(e, 1, n)
    gpb = min(GATHER_TOKENS, n) // tr
    ye = gather_ffn(routing_tables(cend[:, gpb - 1::gpb], tr, cap // tr), pos3, aff_t.reshape(e, 1, n), h,
                    wg, wu, wd, cap, tr, gpb * tr)
    rb = min(COMBINE_RANKS, tr)
    return combine(combine_table(cend, rb, cap // rb), pos3, ye, x, final_g, tr, rb, final)


def _pad_cols(w, width):
    return jnp.pad(w, ((0, 0),) * (w.ndim - 1) + ((0, width - w.shape[-1]),))


def _pad_lora_cols(z):
    o = RWKV_MAIN
    parts = [z[..., :o]]
    for width in (DECAY_LORA, DECAY_LORA, AAA_LORA, AAA_LORA, GATE_LORA):
        parts.append(_pad_cols(z[..., o:o + width], LORA_PAD))
        o += width
    return jnp.concatenate(parts, axis=-1)


def _pad_rows(w, rows):
    return jnp.pad(w, ((0, 0),) * (w.ndim - 2) + ((0, rows - w.shape[-2]), (0, 0)))


def prepare_weights(norm1_g, w_in, lambda_qk, subln_g, shift_mu, w0, w2, a0, a2, g2, k_k, k_a, r_k,
                    lnx_w, lnx_b, w_out, norm2_g, w_router, w_gate, w_up, w_down, final_g):
    scale = ATTN_QK_DIM ** -0.5 * math.log2(math.e)
    w_attn = jnp.concatenate([w_in[..., :ATTN_WIDTH] * scale, w_in[..., ATTN_WIDTH:ATTN_COLS]], axis=-1)
    w_in_p = jnp.concatenate([w_attn, _pad_lora_cols(w_in[..., ATTN_COLS:])], axis=-1).astype(BF16)
    blk = jnp.arange(RWKV_WIDTH) // RWKV_HEAD
    return dict(
        norm1_g=norm1_g[:, None, :], w_in=w_in_p, lambda_qk=lambda_qk, subln_g=subln_g[:, None, :],
        mu=_pad_lora_cols(shift_mu), w0=w0, w2=_pad_rows(w2, LORA_PAD).astype(BF16),
        a0=a0, a2=_pad_rows(a2, LORA_PAD).astype(BF16), g2=_pad_rows(g2, LORA_PAD).astype(BF16),
        k_k=k_k[:, None, :], k_a=k_a[:, None, :], r_k=r_k.reshape(DEPTH, 1, RWKV_WIDTH),
        lnx_w=lnx_w[:, None, :], lnx_b=lnx_b[:, None, :], w_out=w_out.astype(BF16),
        norm2_g=norm2_g[:, None, :], w_router_t=jnp.swapaxes(w_router, 1, 2),
        w_gate=w_gate.astype(BF16), w_up=w_up.astype(BF16), w_down=w_down.astype(BF16),
        final_g=final_g[None, :],
        seg=(blk[:, None] == blk[None, :]).astype(F32),
        scan_consts=scan_constants(),
    )


def trunk(x, p, tiles):
    batch, seq, _ = x.shape
    n = batch * seq
    x = x.reshape(n, D_MODEL)
    attn_consts = attention_constants(tiles["tq"])
    for l in range(DEPTH):
        lambda_init = 0.8 - 0.6 * math.exp(-0.3 * l)
        za, zr = in_proj(x, p["norm1_g"][l], p["w_in"][l], tiles["tm"])
        oa = diff_attention(za, attn_consts, p["lambda_qk"][l], p["subln_g"][l], batch, seq, lambda_init,
                            tiles["tq"])
        r, v, a, lw, kd, bb, bonus, gate = rwkv_prep(
            zr, p["mu"][l], p["w0"][l], p["w2"][l], p["a0"][l], p["a2"][l], p["g2"][l],
            p["k_k"][l], p["k_a"][l], p["r_k"][l], p["seg"], seq, tiles["tm"])
        y2 = rwkv_scan(r, v, a, lw, kd, bb, p["scan_consts"], batch, seq, tiles["tb"])
        x, h, aff_t = post_mix(y2, bonus, gate, oa, x, p["lnx_w"][l], p["lnx_b"][l], p["seg"],
                               p["w_out"][l], p["norm2_g"][l], p["w_router_t"][l], tiles["tm"])
        x = expert_choice_ffn(x, h, aff_t, p["w_gate"][l], p["w_up"][l], p["w_down"][l], p["final_g"],
                              tiles["tr"], final=(l == DEPTH - 1))
    return x.reshape(batch, seq, D_MODEL)


def _tiles(seq):
    return dict(tm=min(512, seq), tq=min(512, seq), tb=min(256, seq), tr=512)


def kernel(x_prompt, x_sample, norm1_g, w_in, lambda_qk, subln_g, shift_mu, w0, w2, a0, a2, g2, k_k, k_a, r_k,
           lnx_w, lnx_b, w_out, norm2_g, w_router, w_gate, w_up, w_down, final_g):
    p = prepare_weights(norm1_g, w_in, lambda_qk, subln_g, shift_mu, w0, w2, a0, a2, g2, k_k, k_a, r_k,
                        lnx_w, lnx_b, w_out, norm2_g, w_router, w_gate, w_up, w_down, final_g)
    y_prompt = trunk(x_prompt, p, _tiles(x_prompt.shape[1]))
    y_sample = trunk(x_sample, p, _tiles(x_sample.shape[1]))
    return (y_prompt, y_sample)
```

```python
import functools
import math

import jax
import jax.numpy as jnp
from jax import lax
from jax.experimental import pallas as pl
from jax.experimental.pallas import tpu as pltpu

F32 = jnp.float32
BF16 = jnp.bfloat16
HIGHEST = lax.Precision.HIGHEST

D_MODEL = 1024
DEPTH = 4
ATTN_WIDTH = 512
ATTN_HEADS = 4
ATTN_QK_DIM = 64
ATTN_V_DIM = 128
RWKV_WIDTH = 512
RWKV_HEAD = 64
RWKV_HEADS = 8
DECAY_LORA = 32
AAA_LORA = 32
GATE_LORA = 96
LORA_PAD = 128
ATTN_COLS = 3 * ATTN_WIDTH
RWKV_MAIN = 3 * RWKV_WIDTH
RWKV_COLS_PAD = RWKV_MAIN + 5 * LORA_PAD
N_EXPERTS = 16
EXPERT_FF = 2 * D_MODEL
CAPACITY_FACTOR = 2
NORM_EPS = 1e-6
GN_EPS = 64e-5

CHUNK = 64
GROUP = 4
GROUP_W = GROUP * RWKV_HEAD
N_LEVELS = CHUNK.bit_length() - 1
SCAN_BLOCKS_PER_STEP = 2
VMEM_LIMIT = 56 * 1024 * 1024


def _cparams(sem):
    return pltpu.CompilerParams(dimension_semantics=sem, vmem_limit_bytes=VMEM_LIMIT)


def _dot(a, b):
    return jnp.dot(a, b, preferred_element_type=F32)


def _dot_nt(a, b):
    return lax.dot_general(a, b, (((1,), (1,)), ((), ())), preferred_element_type=F32)


def _dot_tn(a, b):
    return lax.dot_general(a, b, (((0,), (0,)), ((), ())), preferred_element_type=F32)


def _dot_f32(a, b):
    return jnp.dot(a, b, preferred_element_type=F32, precision=HIGHEST)


def _seg_sum(x, seg):
    hi = x.astype(BF16)
    lo = (x - hi.astype(F32)).astype(BF16)
    return _dot(hi, seg) + _dot(lo, seg)


def _in_proj_kernel(x_ref, g_ref, wqv_ref, wkt_ref, wr_ref, qv_ref, kt_ref, zr_ref):
    x = x_ref[...]
    ms = jnp.mean(x * x, axis=-1, keepdims=True)
    h = (x * lax.rsqrt(ms + NORM_EPS) * g_ref[...]).astype(BF16)
    qv = _dot(h, wqv_ref[...]).astype(BF16)
    for j in range(2 * ATTN_HEADS):
        qv_ref[j] = qv[:, j * ATTN_V_DIM:(j + 1) * ATTN_V_DIM]
    kt = _dot_nt(wkt_ref[...], h).astype(BF16)
    for j in range(ATTN_HEADS):
        kt_ref[j, 0] = kt[j * ATTN_V_DIM:(j + 1) * ATTN_V_DIM, :]
    zr_ref[...] = _dot(h, wr_ref[...])


def in_proj(x, g, w_qv, w_kt, w_r, tm):
    n = x.shape[0]
    full = lambda a: pl.BlockSpec(a.shape, lambda i: (0, 0))
    return pl.pallas_call(
        _in_proj_kernel,
        grid=(n // tm,),
        in_specs=[
            pl.BlockSpec((tm, D_MODEL), lambda i: (i, 0)),
            pl.BlockSpec((1, D_MODEL), lambda i: (0, 0)),
            full(w_qv), full(w_kt), full(w_r),
        ],
        out_specs=[
            pl.BlockSpec((2 * ATTN_HEADS, tm, ATTN_V_DIM), lambda i: (0, i, 0)),
            pl.BlockSpec((ATTN_HEADS, 1, ATTN_V_DIM, tm), lambda i: (0, i, 0, 0)),
            pl.BlockSpec((tm, RWKV_COLS_PAD), lambda i: (i, 0)),
        ],
        out_shape=[
            jax.ShapeDtypeStruct((2 * ATTN_HEADS, n, ATTN_V_DIM), BF16),
            jax.ShapeDtypeStruct((ATTN_HEADS, n // tm, ATTN_V_DIM, tm), BF16),
            jax.ShapeDtypeStruct((n, RWKV_COLS_PAD), F32),
        ],
        compiler_params=_cparams(("parallel",)),
        name="in_proj",
    )(x, g, w_qv, w_kt, w_r)


ROW_BLOCK = 64
POS_BITS = 9
M_INIT = -1e30
QUERY_TILES_PER_STEP = 2


def _attn_kernel(kap_ref, lq_ref, q_ref, k_ref, v_ref, g_ref, qx_ref, kx_ref, dbias_ref, o_ref,
                 q2_ref, s_ref, p_ref, m_ref, al_ref, sh_ref, acc_ref, *, seq, tq, qsub, lambda_init):
    h = pl.program_id(1)
    nk = seq // tq
    kap = kap_ref[h]
    lax.fori_loop(0, qsub, functools.partial(
        _attn_query_tile, pl.program_id(2) * qsub, kap, lq_ref, q_ref, k_ref, v_ref, g_ref, qx_ref, kx_ref,
        dbias_ref, o_ref, q2_ref, s_ref, p_ref, m_ref, al_ref, sh_ref, acc_ref, nk, tq, lambda_init), 0)


def _attn_query_tile(i0, kap, lq_ref, q_ref, k_ref, v_ref, g_ref, qx_ref, kx_ref, dbias_ref, o_ref,
                     q2_ref, s_ref, p_ref, m_ref, al_ref, sh_ref, acc_ref, nk, tq, lambda_init, sub, carry):
    i = i0 + sub
    qrows = pl.ds(pl.multiple_of(sub * tq, tq), tq)
    q = q_ref[qrows, :]
    lane = lax.broadcasted_iota(jnp.int32, q.shape, 1)
    zero = jnp.zeros_like(q)
    q2 = jnp.concatenate([jnp.where(lane < ATTN_QK_DIM, q, zero),
                          jnp.where(lane >= ATTN_QK_DIM, q, zero)], axis=0)
    qx = qx_ref[...]
    qx2 = jnp.concatenate([qx, qx], axis=0)
    q2_ref[0] = jnp.concatenate([q2, qx2], axis=1)
    q2_ref[1] = jnp.concatenate([q2, -qx2], axis=1)
    kx = kx_ref[...]
    ones = jnp.ones((tq, ATTN_V_DIM), BF16)
    m_ref[...] = jnp.full(m_ref.shape, M_INIT, F32)
    acc_ref[...] = jnp.zeros(acc_ref.shape, F32)

    def tile_of(step):
        jj = step - 1
        return jnp.where(step == 0, i, jj + (jj >= i).astype(jnp.int32))

    def scores(step, buf):
        j = tile_of(step)
        k_aug = jnp.concatenate([k_ref[j], kx], axis=0)
        s_ref[buf] = _dot(q2_ref[(j > i).astype(jnp.int32)], k_aug)

    def softmax_and_values(step, buf):
        j = tile_of(step)
        kappa = -kap * jnp.abs(i - j).astype(F32)
        start = pl.multiple_of(j * tq, tq)
        v_aug = jnp.concatenate([v_ref[pl.ds(start, tq), :], ones], axis=1)
        for rb in range(2 * tq // ROW_BLOCK):
            rows = slice(rb * ROW_BLOCK, (rb + 1) * ROW_BLOCK)
            m_old = m_ref[rows, :]
            m_new = jnp.maximum(m_old, jnp.max(s_ref[buf, rows, :], axis=-1, keepdims=True) + kappa)
            al_ref[rows, :] = jnp.exp2(m_old - m_new)
            sh_ref[rows, :] = m_new - kappa
            m_ref[rows, :] = m_new
        for rb in range(2 * tq // ROW_BLOCK):
            rows = slice(rb * ROW_BLOCK, (rb + 1) * ROW_BLOCK)
            p_ref[rows, :] = jnp.exp2(s_ref[buf, rows, :] - sh_ref[rows, :]).astype(BF16)
        acc_ref[...] = al_ref[...] * acc_ref[...] + _dot(p_ref[...], v_aug)

    s_diag = _dot(q2, k_ref[i])
    s_ref[0] = (s_diag.reshape(2, tq, tq) + dbias_ref[...][None]).reshape(2 * tq, tq)

    def pair(t, carry):
        scores(2 * t + 1, 1)
        softmax_and_values(2 * t, 0)
        scores(jnp.minimum(2 * t + 2, nk - 1), 0)
        softmax_and_values(2 * t + 1, 1)
        return carry

    lax.fori_loop(0, nk // 2, pair, 0)

    lq = lq_ref[...]
    lam = (jnp.exp(jnp.sum(lq[0:1] * lq[1:2], axis=-1, keepdims=True))
           - jnp.exp(jnp.sum(lq[2:3] * lq[3:4], axis=-1, keepdims=True)) + lambda_init)
    acc = acc_ref[...]
    o_all = acc[:, :ATTN_V_DIM] / acc[:, ATTN_V_DIM:]
    o = o_all[:tq] - lam * o_all[tq:]
    ms = jnp.mean(o * o, axis=-1, keepdims=True)
    o = o * lax.rsqrt(ms + NORM_EPS) * g_ref[...] * (1.0 - lambda_init)
    o_ref[qrows, :] = o.astype(BF16)
    return carry


def attention_constants(tq):
    assert tq <= (1 << POS_BITS)
    slopes = 2.0 ** (-8.0 * jnp.arange(1, ATTN_HEADS + 1, dtype=F32) / ATTN_HEADS)
    c = slopes * math.log2(math.e)
    cb = c[:, None] * (2.0 ** jnp.arange(POS_BITS, dtype=F32))[None, :]
    hi = cb.astype(BF16).astype(F32)
    lo = (cb - hi).astype(BF16).astype(F32)
    pos = jnp.arange(tq)
    bits = ((pos[:, None] >> jnp.arange(POS_BITS)[None, :]) & 1).astype(F32)
    hb = jnp.broadcast_to(bits[None], (ATTN_HEADS, tq, POS_BITS))
    const = lambda x: jnp.broadcast_to(x[:, None, :], (ATTN_HEADS, tq, POS_BITS))
    pad = jnp.zeros((ATTN_HEADS, tq, ATTN_V_DIM - 4 * POS_BITS), F32)
    qx = jnp.concatenate([const(hi), const(lo), hb, hb, pad], axis=-1).astype(BF16)
    kx = jnp.swapaxes(jnp.concatenate([hb, hb, const(-hi), const(-lo), pad], axis=-1), 1, 2).astype(BF16)
    dist = jnp.abs(pos[:, None] - pos[None, :]).astype(F32)
    dbias = -c[:, None, None] * dist[None]
    kap = c * tq
    return kap, qx, kx, dbias


def diff_attention(qv, kt, consts, lq, subln_g, batch, seq, lambda_init, tq):
    n = batch * seq
    nq = seq // tq
    kap, qx, kx, dbias = consts
    assert nq % 2 == 0, "key tiles are processed in pairs"
    qsub = QUERY_TILES_PER_STEP if nq % QUERY_TILES_PER_STEP == 0 else 1
    nq //= qsub
    kern = functools.partial(_attn_kernel, seq=seq, tq=tq, qsub=qsub, lambda_init=lambda_init)
    per_head = lambda w: pl.BlockSpec((None, tq, w), lambda b, h, i: (h, 0, 0))
    return pl.pallas_call(
        kern,
        grid=(batch, ATTN_HEADS, nq),
        in_specs=[
            pl.BlockSpec(memory_space=pltpu.SMEM),
            pl.BlockSpec((4, ATTN_QK_DIM), lambda b, h, i: (0, 0)),
            pl.BlockSpec((None, qsub * tq, ATTN_V_DIM), lambda b, h, i: (h, b * nq + i, 0)),
            pl.BlockSpec((None, seq // tq, ATTN_V_DIM, tq), lambda b, h, i: (h, b, 0, 0)),
            pl.BlockSpec((None, seq, ATTN_V_DIM), lambda b, h, i: (ATTN_HEADS + h, b, 0)),
            pl.BlockSpec((1, ATTN_V_DIM), lambda b, h, i: (0, 0)),
            per_head(ATTN_V_DIM), pl.BlockSpec((None, ATTN_V_DIM, tq), lambda b, h, i: (h, 0, 0)), per_head(tq),
        ],
        out_specs=pl.BlockSpec((None, qsub * tq, ATTN_V_DIM), lambda b, h, i: (h, b * nq + i, 0)),
        out_shape=jax.ShapeDtypeStruct((ATTN_HEADS, n, ATTN_V_DIM), BF16),
        scratch_shapes=[
            pltpu.VMEM((2, 2 * tq, 2 * ATTN_V_DIM), BF16),
            pltpu.VMEM((2, 2 * tq, tq), F32),
            pltpu.VMEM((2 * tq, tq), BF16),
            pltpu.VMEM((2 * tq, 1), F32),
            pltpu.VMEM((2 * tq, 1), F32),
            pltpu.VMEM((2 * tq, 1), F32),
            pltpu.VMEM((2 * tq, 2 * ATTN_V_DIM), F32),
        ],
        compiler_params=_cparams(("parallel", "parallel", "parallel")),
        name="diff_attention",
    )(kap, lq, qv, kt, qv, subln_g, qx, kx, dbias)


def _softplus(x):
    return jnp.maximum(x, 0.0) + jnp.log(1.0 + jnp.exp(-jnp.abs(x)))


def _sigmoid(x):
    return 1.0 / (1.0 + jnp.exp(-x))


def _prep_kernel(z_ref, zp_ref, zn_ref, mu_ref, w0_ref, w2_ref, a0_ref, a2_ref, g2_ref,
                 kk_ref, ka_ref, rk_ref, seg_ref,
                 r_ref, v_ref, a_ref, lw_ref, kd_ref, bb_ref, bonus_ref, gate_ref, *, seq, tm):
    i = pl.program_id(0)
    z = z_ref[...]
    row = lax.broadcasted_iota(jnp.int32, z.shape, 0)
    first = (i * tm) % seq == 0
    last = ((i + 1) * tm) % seq == 0
    zp = jnp.where(first, 0.0, zp_ref[7:8, :])
    zn = jnp.where(last, 0.0, zn_ref[0:1, :])
    prev = jnp.where(row == 0, zp, pltpu.roll(z, 1, 0))
    nxt = jnp.where(row == tm - 1, zn, pltpu.roll(z, tm - 1, 0))
    z = z + mu_ref[0:1, :] * (prev - z) + mu_ref[1:2, :] * (nxt - z)

    w = RWKV_WIDTH
    r = z[:, 0:w]
    k = z[:, w:2 * w]
    v = z[:, 2 * w:3 * w]
    seg = seg_ref[...]
    kk = k * kk_ref[...]
    ss = _seg_sum(kk * kk, seg)
    kk = kk / jnp.maximum(jnp.sqrt(ss), 1e-12)
    r_ref[...] = r.astype(BF16)
    v_ref[...] = v.astype(BF16)
    a_ref[...] = (-kk).astype(BF16)
    for d in range(2):
        xw = z[:, RWKV_MAIN + d * LORA_PAD:RWKV_MAIN + (d + 1) * LORA_PAD]
        xa = z[:, RWKV_MAIN + (2 + d) * LORA_PAD:RWKV_MAIN + (3 + d) * LORA_PAD]
        wlog = -_softplus(-(w0_ref[d:d + 1, :] + _dot(jnp.tanh(xw).astype(BF16), w2_ref[d]))) - 0.5
        lw_ref[d] = -jnp.exp(wlog)
        iclr = _sigmoid(a0_ref[d:d + 1, :] + _dot(xa.astype(BF16), a2_ref[d]))
        kd_ref[d] = (k * (1.0 + (iclr - 1.0) * ka_ref[...])).astype(BF16)
        bb_ref[d] = (kk * iclr).astype(BF16)
    xg = z[:, RWKV_MAIN + 4 * LORA_PAD:RWKV_MAIN + 5 * LORA_PAD]
    gate_ref[...] = _dot(_sigmoid(xg).astype(BF16), g2_ref[...]).astype(BF16)
    bonus_ref[...] = (_seg_sum(r * k * rk_ref[...], seg) * v).astype(BF16)


def rwkv_prep(zr, mu, w0, w2, a0, a2, g2, k_k, k_a, r_k, seg, seq, tm):
    n, cols = zr.shape
    nb8 = n // 8
    tb8 = tm // 8
    full = lambda shape: pl.BlockSpec(shape, lambda i: (0,) * len(shape))
    tok = pl.BlockSpec((tm, RWKV_WIDTH), lambda i: (i, 0))
    tok2 = pl.BlockSpec((2, tm, RWKV_WIDTH), lambda i: (0, i, 0))
    sds = jax.ShapeDtypeStruct((n, RWKV_WIDTH), BF16)
    sds2 = jax.ShapeDtypeStruct((2, n, RWKV_WIDTH), BF16)
    sds2_f32 = jax.ShapeDtypeStruct((2, n, RWKV_WIDTH), F32)
    return pl.pallas_call(
        functools.partial(_prep_kernel, seq=seq, tm=tm),
        grid=(n // tm,),
        in_specs=[
            pl.BlockSpec((tm, cols), lambda i: (i, 0)),
            pl.BlockSpec((8, cols), lambda i: (jnp.maximum(i * tb8 - 1, 0), 0)),
            pl.BlockSpec((8, cols), lambda i: (jnp.minimum((i + 1) * tb8, nb8 - 1), 0)),
            full((2, cols)), full((2, RWKV_WIDTH)), full((2, LORA_PAD, RWKV_WIDTH)),
            full((2, RWKV_WIDTH)), full((2, LORA_PAD, RWKV_WIDTH)), full((LORA_PAD, RWKV_WIDTH)),
            full((1, RWKV_WIDTH)), full((1, RWKV_WIDTH)), full((1, RWKV_WIDTH)),
            full((RWKV_WIDTH, RWKV_WIDTH)),
        ],
        out_specs=[tok, tok, tok, tok2, tok2, tok2, tok, tok],
        out_shape=[sds, sds, sds, sds2_f32, sds2, sds2, sds, sds],
        compiler_params=_cparams(("parallel",)),
        name="rwkv_prep",
    )(zr, zr, zr, mu, w0, w2, a0, a2, g2, k_k, k_a, r_k, seg)


def _scan_kernel(r_ref, v_ref, a_ref, lw_ref, k_ref, b_ref, tri_ref, strict_ref, incl_ref, lvl_ref, eye_ref, bd_ref,
                 y_ref, h_ref, *, tb, nsub):
    d = pl.program_id(1)
    i = pl.program_id(2)
    nc = tb // CHUNK
    ngrp = RWKV_WIDTH // GROUP_W

    @pl.when(i == 0)
    def _():
        h_ref[...] = jnp.zeros(h_ref.shape, F32)

    tri = tri_ref[...]
    strict = strict_ref[...] > 0.5
    incl = incl_ref[...] > 0.5
    eye = eye_ref[...]
    bdm = bd_ref[...] > 0.5
    ones16 = jnp.ones((16, GROUP_W), BF16)
    row16 = lax.broadcasted_iota(jnp.int32, (16, GROUP_W), 0)

    def bd(x):
        return jnp.where(bdm, jnp.concatenate([x] * GROUP, axis=0), 0.0).astype(BF16)

    def split(x):
        hi = x.astype(BF16)
        return hi, (x - hi.astype(F32)).astype(BF16)

    lax.fori_loop(0, nsub, functools.partial(
        _scan_block, d, nsub, tb, nc, ngrp, bd, split, tri, strict, incl, eye, bdm, ones16, row16,
        r_ref, v_ref, a_ref, lw_ref, k_ref, b_ref, lvl_ref, y_ref, h_ref), 0)


def _scan_block(d, nsub, tb, nc, ngrp, bd, split, tri, strict, incl, eye, bdm, ones16, row16,
                r_ref, v_ref, a_ref, lw_ref, k_ref, b_ref, lvl_ref, y_ref, h_ref, sub, carry):
    base = (sub + d * (nsub - 1 - 2 * sub)) * tb
    slices = []
    for cc in range(nc):
        ci = cc + d * (nc - 1 - 2 * cc)
        slices.append(pl.ds(pl.multiple_of(base + ci * CHUNK, CHUNK), CHUNK))
    groups = [slice(g * GROUP_W, (g + 1) * GROUP_W) for g in range(ngrp)]
    units = [dict(sl=sl, cols=cols) for sl in slices for cols in groups]

    for c in units:
        lw = lw_ref[c["sl"], c["cols"]]
        r = r_ref[c["sl"], c["cols"]].astype(F32)
        v = v_ref[c["sl"], c["cols"]].astype(F32)
        a = a_ref[c["sl"], c["cols"]].astype(F32)
        k = k_ref[c["sl"], c["cols"]].astype(F32)
        b = b_ref[c["sl"], c["cols"]].astype(F32)
        hi, lo = split(lw)
        cs = _dot(tri, jnp.concatenate([hi, lo], axis=1))
        lcum = cs[:, :GROUP_W] + cs[:, GROUP_W:]
        ltot = jnp.where(d == 0, lcum[CHUNK - 1:CHUNK], lcum[0:1])
        at = a * jnp.exp(lcum - lw)
        rt = r * jnp.exp(lcum)
        einv = jnp.exp(-lcum)
        eend = jnp.exp(ltot - lcum)
        c["x"] = jnp.concatenate([at, rt], axis=0).astype(BF16)
        c["ybk"] = jnp.concatenate([bd(b * einv), bd(k * einv)], axis=0)
        c["bkw"] = jnp.concatenate([b * eend, k * eend], axis=0).astype(BF16)
        c["vbd"] = bd(v)
        c["v"] = v.astype(BF16)
        wtot = jnp.exp(ltot)
        whi = wtot.astype(BF16).astype(F32)
        c["whl"] = jnp.where(row16 == 0, whi, jnp.where(row16 == 1, wtot - whi, 0.0)).astype(BF16)
    for c in units:
        amat = _dot_nt(c["x"], c.pop("ybk"))
        c["aab"] = jnp.where(strict, amat[:CHUNK, :GROUP_W], 0.0)
        c["aak"] = jnp.where(strict, amat[:CHUNK, GROUP_W:], 0.0).astype(BF16)
        c["ar"] = jnp.concatenate([jnp.where(incl, amat[CHUNK:, :GROUP_W], 0.0),
                                   jnp.where(incl, amat[CHUNK:, GROUP_W:], 0.0)], axis=1).astype(BF16)
        c["p"] = eye + jnp.where(lvl_ref[0] > 0.5, c["aab"], 0.0)
    for lvl in range(1, N_LEVELS):
        for c in units:
            e = jnp.where(lvl_ref[lvl] > 0.5, c["aab"], 0.0)
            c["g"] = _dot(c["p"].astype(BF16), bd(e)).astype(BF16)
        for c in units:
            c["p"] = c["p"] + _dot(c.pop("g"), bd(c["p"]))
    for c in units:
        c["p"] = c["p"].astype(BF16)
        c["av"] = _dot(c.pop("aak"), c["vbd"])
        c["wcol"] = _dot_tn(c.pop("whl"), ones16)

    for cc in range(nc):
        cu = units[cc * ngrp:(cc + 1) * ngrp]
        hs = [h_ref[g] for g in range(ngrp)]
        hx = [_dot(c["x"], h.astype(BF16)) for c, h in zip(cu, hs)]
        us = [_dot(c["p"], bd(x[:CHUNK] + c["av"])) for c, x in zip(cu, hx)]
        for g, (c, h, x, u) in enumerate(zip(cu, hs, hx, us)):
            y_ref[c["sl"], c["cols"]] = x[CHUNK:] + _dot(c["ar"], jnp.concatenate([bd(u), c["vbd"]], axis=0))
            upd = _dot_tn(c["bkw"], jnp.concatenate([u.astype(BF16), c["v"]], axis=0))
            h_ref[g] = c["wcol"] * h + jnp.where(bdm, upd, 0.0)
    return carry


def rwkv_scan(r, v, a, lw, kd, bb, consts, batch, seq, tb):
    n = batch * seq
    nsub = SCAN_BLOCKS_PER_STEP if seq % (tb * SCAN_BLOCKS_PER_STEP) == 0 else 1
    nblk = seq // (tb * nsub)
    tri, strict, incl, lvl, eye, bdm = consts

    def tmap(b, d, i):
        return (b * nblk + i + d * (nblk - 1 - 2 * i), 0)

    def tmap_d(b, d, i):
        return (d, b * nblk + i + d * (nblk - 1 - 2 * i), 0)

    tok = pl.BlockSpec((nsub * tb, RWKV_WIDTH), tmap)
    tok_d = pl.BlockSpec((None, nsub * tb, RWKV_WIDTH), tmap_d)
    per_d = lambda shape: pl.BlockSpec((None,) + shape, lambda b, d, i: (d,) + (0,) * len(shape))
    full = lambda shape: pl.BlockSpec(shape, lambda b, d, i: (0, 0))
    return pl.pallas_call(
        functools.partial(_scan_kernel, tb=tb, nsub=nsub),
        grid=(batch, 2, nblk),
        in_specs=[tok, tok, tok, tok_d, tok_d, tok_d,
                  per_d((CHUNK, CHUNK)), per_d((CHUNK, GROUP * CHUNK)), per_d((CHUNK, GROUP * CHUNK)),
                  per_d((N_LEVELS, CHUNK, GROUP * CHUNK)),
                  full((CHUNK, GROUP * CHUNK)), full((GROUP_W, GROUP_W))],
        out_specs=tok_d,
        out_shape=jax.ShapeDtypeStruct((2, n, RWKV_WIDTH), F32),
        scratch_shapes=[pltpu.VMEM((RWKV_WIDTH // GROUP_W, GROUP_W, GROUP_W), F32)],
        compiler_params=_cparams(("parallel", "parallel", "arbitrary")),
        name="rwkv_scan",
    )(r, v, a, lw, kd, bb, tri, strict, incl, lvl, eye, bdm)


def scan_constants():
    t = jnp.arange(CHUNK)
    before_eq = (t[None, :] <= t[:, None])
    tri = jnp.stack([before_eq, before_eq.T]).astype(F32)
    incl = jnp.tile(tri, (1, 1, GROUP))
    tri = tri.astype(BF16)
    eye = jnp.tile(jnp.eye(CHUNK, dtype=F32), (1, GROUP))
    strict = incl - eye[None]
    levels = []
    for lv in range(N_LEVELS):
        half = t // (1 << lv)
        fwd = ((half[:, None] // 2 == half[None, :] // 2) & (half[:, None] % 2 == 1) & (half[None, :] % 2 == 0))
        levels.append(jnp.stack([fwd, fwd.T]))
    lvl = jnp.tile(jnp.stack(levels, axis=1).astype(F32), (1, 1, 1, GROUP))
    blk = jnp.arange(GROUP_W) // RWKV_HEAD
    bdm = (blk[:, None] == blk[None, :]).astype(F32)
    return tri, strict, incl, lvl, eye, bdm


def _post_kernel(yf_ref, yb_ref, bonus_ref, gate_ref, oa_ref, x_ref, lw_ref, lb_ref, seg_ref,
                 wo_ref, g2_ref, wr_ref, xo_ref, h_ref, aff_ref):
    y = yf_ref[...] + yb_ref[...]
    seg = seg_ref[...]
    mean = _seg_sum(y, seg) * (1.0 / RWKV_HEAD)
    yc = y - mean
    var = _seg_sum(yc * yc, seg) * (1.0 / RWKV_HEAD)
    yn = yc * lax.rsqrt(var + GN_EPS) * lw_ref[...] + lb_ref[...]
    orw = ((yn + bonus_ref[...]) * gate_ref[...]).astype(BF16)
    oa = jnp.concatenate([oa_ref[j] for j in range(ATTN_HEADS)], axis=1)
    x = (x_ref[...] + _dot(oa, wo_ref[:ATTN_WIDTH, :]) + _dot(orw, wo_ref[ATTN_WIDTH:, :]))
    xo_ref[...] = x
    ms = jnp.mean(x * x, axis=-1, keepdims=True)
    h = x * lax.rsqrt(ms + NORM_EPS) * g2_ref[...]
    h_ref[...] = h.astype(BF16)
    logits = lax.dot_general(wr_ref[...], h, (((1,), (1,)), ((), ())),
                             preferred_element_type=F32, precision=HIGHEST)
    logits = logits - jnp.max(logits, axis=0, keepdims=True)
    e = jnp.exp(logits)
    aff_ref[...] = e / jnp.sum(e, axis=0, keepdims=True)


def post_mix(y2, bonus, gate, oa, x, lnx_w, lnx_b, seg, w_out, norm2_g, w_router_t, tm):
    n = x.shape[0]
    full = lambda shape: pl.BlockSpec(shape, lambda i: (0,) * len(shape))
    tok = lambda w: pl.BlockSpec((tm, w), lambda i: (i, 0))
    return pl.pallas_call(
        _post_kernel,
        grid=(n // tm,),
        in_specs=[
            pl.BlockSpec((None, tm, RWKV_WIDTH), lambda i: (0, i, 0)),
            pl.BlockSpec((None, tm, RWKV_WIDTH), lambda i: (1, i, 0)),
            tok(RWKV_WIDTH), tok(RWKV_WIDTH),
            pl.BlockSpec((ATTN_HEADS, tm, ATTN_V_DIM), lambda i: (0, i, 0)), tok(D_MODEL),
            full((1, RWKV_WIDTH)), full((1, RWKV_WIDTH)), full((RWKV_WIDTH, RWKV_WIDTH)),
            full((D_MODEL, D_MODEL)), full((1, D_MODEL)), full((N_EXPERTS, D_MODEL)),
        ],
        out_specs=[tok(D_MODEL), tok(D_MODEL), pl.BlockSpec((N_EXPERTS, tm), lambda i: (0, i))],
        out_shape=[
            jax.ShapeDtypeStruct((n, D_MODEL), F32),
            jax.ShapeDtypeStruct((n, D_MODEL), BF16),
            jax.ShapeDtypeStruct((N_EXPERTS, n), F32),
        ],
        compiler_params=_cparams(("parallel",)),
        name="post_mix",
    )(y2, y2, bonus, gate, oa, x, lnx_w, lnx_b, seg, w_out, norm2_g, w_router_t)


COMBINE_RANKS = 256
GATHER_TOKENS = 1024


def _thresh_kernel(aff_ref, thr_ref, need_ref, *, cap):
    bits = pltpu.bitcast(aff_ref[...], jnp.int32)
    e = bits.shape[0]

    def body(it, thr):
        cand = thr | jnp.left_shift(jnp.int32(1), 30 - it)
        cnt = jnp.sum(jnp.where(bits >= cand, 1.0, 0.0), axis=1, keepdims=True)
        return jnp.where(cnt >= cap, cand, thr)

    thr = lax.fori_loop(0, 31, body, jnp.zeros((e, 1), jnp.int32))
    n_gt = jnp.sum(jnp.where(bits > thr, 1.0, 0.0), axis=1, keepdims=True)
    thr_ref[...] = jnp.broadcast_to(thr, thr_ref.shape)
    need_ref[...] = jnp.broadcast_to(cap - n_gt, need_ref.shape)


def _prefix_kernel(aff_ref, thr_ref, need_ref, u_ref, pos_ref, cend_ref, carry_ref):
    @pl.when(pl.program_id(0) == 0)
    def _():
        carry_ref[...] = jnp.zeros(carry_ref.shape, F32)

    bits = pltpu.bitcast(aff_ref[...], jnp.int32)
    e, tr = bits.shape
    thr = thr_ref[:, 0:1]
    need = need_ref[:, 0:1]
    gt = bits > thr
    eq = bits == thr
    marks = jnp.concatenate([jnp.where(gt, 1.0, 0.0), jnp.where(eq, 1.0, 0.0)], axis=0).astype(BF16)
    pre = _dot(marks, u_ref[...]) + carry_ref[...]
    pgt = pre[:e]
    peq = pre[e:]
    sel = jnp.where(gt, 1.0, jnp.where(eq, jnp.where(peq <= need, 1.0, 0.0), 0.0))
    rank = pgt + jnp.minimum(peq, need) - 1.0
    pos_ref[...] = jnp.where(sel > 0.5, rank, -1.0).astype(jnp.int32)
    carry_ref[...] = pre[:, tr - 1:tr]
    cend_ref[...] = jnp.broadcast_to(rank[:, tr - 1:tr] + 1.0, cend_ref.shape)


def select_tokens(aff_t, cap, tr):
    e, n = aff_t.shape
    nb = n // tr
    thr, need = pl.pallas_call(
        functools.partial(_thresh_kernel, cap=float(cap)),
        out_shape=[jax.ShapeDtypeStruct((e, 128), jnp.int32), jax.ShapeDtypeStruct((e, 128), F32)],
        compiler_params=pltpu.CompilerParams(vmem_limit_bytes=VMEM_LIMIT),
        name="route_threshold",
    )(aff_t)
    upper = (jnp.arange(tr)[:, None] <= jnp.arange(tr)[None, :]).astype(BF16)
    pos, cend = pl.pallas_call(
        _prefix_kernel,
        grid=(nb,),
        in_specs=[pl.BlockSpec((e, tr), lambda i: (0, i)),
                  pl.BlockSpec((e, 128), lambda i: (0, 0)),
                  pl.BlockSpec((e, 128), lambda i: (0, 0)),
                  pl.BlockSpec((tr, tr), lambda i: (0, 0))],
        out_specs=[pl.BlockSpec((e, tr), lambda i: (0, i)),
                   pl.BlockSpec((None, e, 128), lambda i: (i, 0, 0))],
        out_shape=[jax.ShapeDtypeStruct((e, n), jnp.int32), jax.ShapeDtypeStruct((nb, e, 128), F32)],
        scratch_shapes=[pltpu.VMEM((2 * e, 1), F32)],
        compiler_params=_cparams(("arbitrary",)),
        name="route_prefix",
    )(aff_t, thr, need, upper)
    return pos, cend[:, :, 0].T.astype(jnp.int32)


def _pick(onehot, values):
    return jnp.sum(jnp.where(onehot, values[None, :], 0), axis=1)


def gather_table(cend, rb, n_rb):
    nb = cend.shape[0]
    cstart = jnp.concatenate([jnp.zeros((1,), jnp.int32), cend[:-1]])
    count = cend - cstart
    ob_first = jnp.minimum(cstart // rb, n_rb - 1)
    ob_last = jnp.where(count > 0, (cend - 1) // rb, ob_first)
    nitems = jnp.where(count > 0, ob_last - ob_first + 1, 0)
    cum = jnp.cumsum(nitems)
    start = cum - nitems
    w = jnp.arange(nb + n_rb, dtype=jnp.int32)
    owner = (w[:, None] >= start[None, :]) & (w[:, None] < cum[None, :])
    valid = w < cum[-1]
    tb = jnp.where(valid, _pick(owner, jnp.arange(nb, dtype=jnp.int32)), nb - 1)
    ob = jnp.where(valid, _pick(owner, ob_first - start) + w, n_rb - 1)
    prev_ob = jnp.concatenate([jnp.full((1,), -1, jnp.int32), ob[:-1]])
    next_ob = jnp.concatenate([ob[1:], jnp.full((1,), -1, jnp.int32)])
    next_valid = jnp.concatenate([valid[1:], jnp.zeros((1,), bool)])
    first = valid & (ob != prev_ob)
    last = valid & ((ob != next_ob) | ~next_valid)
    flags = first.astype(jnp.int32) + 2 * last.astype(jnp.int32) + 4 * valid.astype(jnp.int32)
    return ob, tb, flags


def routing_tables(cend, rb, n_rb):
    ob, tb, flags = jax.vmap(lambda c: gather_table(c, rb, n_rb))(cend)
    return ob.reshape(-1), tb.reshape(-1), flags.reshape(-1)


def combine_table(cend, rb, n_rb):
    e, nb = cend.shape
    cstart = jnp.concatenate([jnp.zeros((e, 1), jnp.int32), cend[:, :-1]], axis=1)
    ob_first = jnp.minimum(cstart // rb, n_rb - 1)
    ob_last = jnp.where(cend > cstart, (cend - 1) // rb, ob_first)
    obf = ob_first.T.reshape(-1)
    nitems = (ob_last - ob_first + 1).T.reshape(-1)
    cum = jnp.cumsum(nitems)
    start = cum - nitems
    length = nb * e + e * n_rb
    w = jnp.arange(length, dtype=jnp.int32)
    owner = (w[:, None] >= start[None, :]) & (w[:, None] < cum[None, :])
    valid = w < cum[-1]
    pair = jnp.where(valid, _pick(owner, jnp.arange(nb * e, dtype=jnp.int32)), nb * e - 1)
    ob = jnp.where(valid, _pick(owner, obf - start) + w, obf[-1] + nitems[-1] - 1)
    ex = pair % e
    first = valid & (_pick(owner, start) == w) & (ex == 0)
    last = valid & (_pick(owner, cum) - 1 == w) & (ex == e - 1)
    flags = first.astype(jnp.int32) + 2 * last.astype(jnp.int32) + 4 * valid.astype(jnp.int32)
    return pair // e, ex, ob, flags


def _gather_ffn_kernel(ob_ref, tb_ref, fl_ref, pos_ref, aff_ref, h_ref, wg_ref, wu_ref, wd_ref, o_ref,
                       x_acc, g_acc, *, length, rb, tg):
    idx = pl.program_id(0) * length + pl.program_id(1)
    flags = fl_ref[idx]
    ob = ob_ref[idx]

    @pl.when((flags & 1) != 0)
    def _():
        x_acc[...] = jnp.zeros(x_acc.shape, F32)
        g_acc[...] = jnp.zeros(g_acc.shape, F32)

    @pl.when((flags & 4) != 0)
    def _():
        rank = ob * rb + lax.broadcasted_iota(jnp.int32, (rb, tg), 0)
        match = pos_ref[...] == rank
        x_acc[...] += _dot(jnp.where(match, 1.0, 0.0).astype(BF16), h_ref[...])
        g_acc[...] += jnp.sum(jnp.where(match, aff_ref[...], 0.0), axis=1, keepdims=True)

    @pl.when((flags & 2) != 0)
    def _():
        x = x_acc[...].astype(BF16)
        hg = _dot(x, wg_ref[...])
        hu = _dot(x, wu_ref[...])
        act = (hg * _sigmoid(hg) * hu).astype(BF16)
        o_ref[...] = (_dot(act, wd_ref[...]) * g_acc[...]).astype(BF16)


def gather_ffn(tables, pos3, aff3, h, wg, wu, wd, cap, rb, tg):
    e = pos3.shape[0]
    n = h.shape[0]
    length = n // tg + cap // rb
    ob_of, tb_of, flags = tables
    tok = pl.BlockSpec((None, 1, tg), lambda e, w, ob, tb, fl: (e, 0, tb[e * length + w]))
    wspec = lambda r, c: pl.BlockSpec((None, r, c), lambda e, w, ob, tb, fl: (e, 0, 0))
    return pl.pallas_call(
        functools.partial(_gather_ffn_kernel, length=length, rb=rb, tg=tg),
        grid_spec=pltpu.PrefetchScalarGridSpec(
            num_scalar_prefetch=3,
            grid=(e, length),
            in_specs=[tok, tok,
                      pl.BlockSpec((tg, D_MODEL), lambda e, w, ob, tb, fl: (tb[e * length + w], 0)),
                      wspec(D_MODEL, EXPERT_FF), wspec(D_MODEL, EXPERT_FF), wspec(EXPERT_FF, D_MODEL)],
            out_specs=pl.BlockSpec((None, rb, D_MODEL), lambda e, w, ob, tb, fl: (e, ob[e * length + w], 0)),
            scratch_shapes=[pltpu.VMEM((rb, D_MODEL), F32), pltpu.VMEM((rb, 1), F32)],
        ),
        out_shape=jax.ShapeDtypeStruct((e, cap, D_MODEL), BF16),
        compiler_params=_cparams(("parallel", "arbitrary")),
        name="gather_ffn",
    )(ob_of, tb_of, flags, pos3, aff3, h, wg, wu, wd)


def _combine_kernel(tb_ref, ex_ref, ob_ref, fl_ref, pos_ref, ye_ref, x_ref, g_ref, o_ref, *, rb, tr, final):
    w = pl.program_id(0)
    flags = fl_ref[w]

    @pl.when((flags & 1) != 0)
    def _():
        o_ref[...] = x_ref[...]

    @pl.when((flags & 4) != 0)
    def _():
        rank = ob_ref[w] * rb + lax.broadcasted_iota(jnp.int32, (rb, tr), 0)
        onehot = jnp.where(pos_ref[...] == rank, 1.0, 0.0).astype(BF16)
        o_ref[...] += _dot_tn(onehot, ye_ref[...])

    if final:
        @pl.when((flags & 2) != 0)
        def _():
            x = o_ref[...]
            ms = jnp.mean(x * x, axis=-1, keepdims=True)
            o_ref[...] = x * lax.rsqrt(ms + NORM_EPS) * g_ref[...]


def combine(table, pos3, ye, x, final_g, tr, rb, final):
    n = x.shape[0]
    tb_of, ex_of, ob_of, flags = table
    return pl.pallas_call(
        functools.partial(_combine_kernel, rb=rb, tr=tr, final=final),
        grid_spec=pltpu.PrefetchScalarGridSpec(
            num_scalar_prefetch=4,
            grid=(tb_of.shape[0],),
            in_specs=[pl.BlockSpec((None, 1, tr), lambda w, tb, ex, ob, fl: (ex[w], 0, tb[w])),
                      pl.BlockSpec((None, rb, D_MODEL), lambda w, tb, ex, ob, fl: (ex[w], ob[w], 0)),
                      pl.BlockSpec((tr, D_MODEL), lambda w, tb, ex, ob, fl: (tb[w], 0)),
                      pl.BlockSpec((1, D_MODEL), lambda w, tb, ex, ob, fl: (0, 0))],
            out_specs=pl.BlockSpec((tr, D_MODEL), lambda w, tb, ex, ob, fl: (tb[w], 0)),
        ),
        out_shape=jax.ShapeDtypeStruct((n, D_MODEL), F32),
        compiler_params=_cparams(("arbitrary",)),
        name="combine",
    )(tb_of, ex_of, ob_of, flags, pos3, ye, x, final_g)


def expert_choice_ffn(x, h, aff_t, wg, wu, wd, final_g, tr, final):
    e, n = aff_t.shape
    cap = CAPACITY_FACTOR * n // e
    pos, cend = select_tokens(aff_t, cap, tr)
    pos3 = pos.reshape(e, 1, n)
    gpb = min(GATHER_TOKENS, n) // tr
    ye = gather_ffn(routing_tables(cend[:, gpb - 1::gpb], tr, cap // tr), pos3, aff_t.reshape(e, 1, n), h,
                    wg, wu, wd, cap, tr, gpb * tr)
    rb = min(COMBINE_RANKS, tr)
    return combine(combine_table(cend, rb, cap // rb), pos3, ye, x, final_g, tr, rb, final)


def _pad_cols(w, width):
    return jnp.pad(w, ((0, 0),) * (w.ndim - 1) + ((0, width - w.shape[-1]),))


def _pad_lora_cols(z):
    o = RWKV_MAIN
    parts = [z[..., :o]]
    for width in (DECAY_LORA, DECAY_LORA, AAA_LORA, AAA_LORA, GATE_LORA):
        parts.append(_pad_cols(z[..., o:o + width], LORA_PAD))
        o += width
    return jnp.concatenate(parts, axis=-1)


def _pad_rows(w, rows):
    return jnp.pad(w, ((0, 0),) * (w.ndim - 2) + ((0, rows - w.shape[-2]), (0, 0)))


def prepare_weights(norm1_g, w_in, lambda_qk, subln_g, shift_mu, w0, w2, a0, a2, g2, k_k, k_a, r_k,
                    lnx_w, lnx_b, w_out, norm2_g, w_router, w_gate, w_up, w_down, final_g):
    scale = ATTN_QK_DIM ** -0.5 * math.log2(math.e)
    w_qv = jnp.concatenate([w_in[..., :ATTN_WIDTH] * scale, w_in[..., 2 * ATTN_WIDTH:ATTN_COLS]], axis=-1)
    w_kt = jnp.swapaxes(w_in[..., ATTN_WIDTH:2 * ATTN_WIDTH], 1, 2)
    blk = jnp.arange(RWKV_WIDTH) // RWKV_HEAD
    return dict(
        norm1_g=norm1_g[:, None, :], w_qv=w_qv.astype(BF16), w_kt=w_kt.astype(BF16),
        w_r=_pad_lora_cols(w_in[..., ATTN_COLS:]).astype(BF16), lambda_qk=lambda_qk, subln_g=subln_g[:, None, :],
        mu=_pad_lora_cols(shift_mu), w0=w0, w2=_pad_rows(w2, LORA_PAD).astype(BF16),
        a0=a0, a2=_pad_rows(a2, LORA_PAD).astype(BF16), g2=_pad_rows(g2, LORA_PAD).astype(BF16),
        k_k=k_k[:, None, :], k_a=k_a[:, None, :], r_k=r_k.reshape(DEPTH, 1, RWKV_WIDTH),
        lnx_w=lnx_w[:, None, :], lnx_b=lnx_b[:, None, :], w_out=w_out.astype(BF16),
        norm2_g=norm2_g[:, None, :], w_router_t=jnp.swapaxes(w_router, 1, 2),
        w_gate=w_gate.astype(BF16), w_up=w_up.astype(BF16), w_down=w_down.astype(BF16),
        final_g=final_g[None, :],
        seg=(blk[:, None] == blk[None, :]).astype(BF16),
        scan_consts=scan_constants(),
    )


def trunk(x, p, tiles):
    batch, seq, _ = x.shape
    n = batch * seq
    x = x.reshape(n, D_MODEL)
    attn_consts = attention_constants(tiles["tq"])
    for l in range(DEPTH):
        lambda_init = 0.8 - 0.6 * math.exp(-0.3 * l)
        qv, kt, zr = in_proj(x, p["norm1_g"][l], p["w_qv"][l], p["w_kt"][l], p["w_r"][l], tiles["tq"])
        oa = diff_attention(qv, kt, attn_consts, p["lambda_qk"][l], p["subln_g"][l], batch, seq, lambda_init,
                            tiles["tq"])
        r, v, a, lw, kd, bb, bonus, gate = rwkv_prep(
            zr, p["mu"][l], p["w0"][l], p["w2"][l], p["a0"][l], p["a2"][l], p["g2"][l],
            p["k_k"][l], p["k_a"][l], p["r_k"][l], p["seg"], seq, tiles["tm"])
        y2 = rwkv_scan(r, v, a, lw, kd, bb, p["scan_consts"], batch, seq, tiles["tb"])
        x, h, aff_t = post_mix(y2, bonus, gate, oa, x, p["lnx_w"][l], p["lnx_b"][l], p["seg"],
                               p["w_out"][l], p["norm2_g"][l], p["w_router_t"][l], tiles["tm"])
        x = expert_choice_ffn(x, h, aff_t, p["w_gate"][l], p["w_up"][l], p["w_down"][l], p["final_g"],
                              tiles["tr"], final=(l == DEPTH - 1))
    return x.reshape(batch, seq, D_MODEL)


def _tiles(seq):
    return dict(tm=min(512, seq), tq=min(512, seq), tb=min(256, seq), tr=512)


def kernel(x_prompt, x_sample, norm1_g, w_in, lambda_qk, subln_g, shift_mu, w0, w2, a0, a2, g2, k_k, k_a, r_k,
           lnx_w, lnx_b, w_out, norm2_g, w_router, w_gate, w_up, w_down, final_g):
    p = prepare_weights(norm1_g, w_in, lambda_qk, subln_g, shift_mu, w0, w2, a0, a2, g2, k_k, k_a, r_k,
                        lnx_w, lnx_b, w_out, norm2_g, w_router, w_gate, w_up, w_down, final_g)
    y_prompt = trunk(x_prompt, p, _tiles(x_prompt.shape[1]))
    y_sample = trunk(x_sample, p, _tiles(x_sample.shape[1]))
    return (y_prompt, y_sample)
```

```python
import functools
import math

import jax
import jax.numpy as jnp
from jax import lax
from jax.experimental import pallas as pl
from jax.experimental.pallas import tpu as pltpu

F32 = jnp.float32
BF16 = jnp.bfloat16
HIGHEST = lax.Precision.HIGHEST

D_MODEL = 1024
DEPTH = 4
ATTN_WIDTH = 512
ATTN_HEADS = 4
ATTN_QK_DIM = 64
ATTN_V_DIM = 128
RWKV_WIDTH = 512
RWKV_HEAD = 64
RWKV_HEADS = 8
DECAY_LORA = 32
AAA_LORA = 32
GATE_LORA = 96
LORA_PAD = 128
ATTN_COLS = 3 * ATTN_WIDTH
RWKV_MAIN = 3 * RWKV_WIDTH
RWKV_COLS_PAD = RWKV_MAIN + 5 * LORA_PAD
N_EXPERTS = 16
EXPERT_FF = 2 * D_MODEL
CAPACITY_FACTOR = 2
NORM_EPS = 1e-6
GN_EPS = 64e-5

CHUNK = 64
GROUP = 4
GROUP_W = GROUP * RWKV_HEAD
N_LEVELS = CHUNK.bit_length() - 1
SCAN_BLOCKS_PER_STEP = 2
VMEM_LIMIT = 56 * 1024 * 1024


def _cparams(sem):
    return pltpu.CompilerParams(dimension_semantics=sem, vmem_limit_bytes=VMEM_LIMIT)


def _dot(a, b):
    return jnp.dot(a, b, preferred_element_type=F32)


def _dot_nt(a, b):
    return lax.dot_general(a, b, (((1,), (1,)), ((), ())), preferred_element_type=F32)


def _dot_tn(a, b):
    return lax.dot_general(a, b, (((0,), (0,)), ((), ())), preferred_element_type=F32)


def _dot_f32(a, b):
    return jnp.dot(a, b, preferred_element_type=F32, precision=HIGHEST)


def _seg_sum(x, seg):
    hi = x.astype(BF16)
    lo = (x - hi.astype(F32)).astype(BF16)
    return _dot(hi, seg) + _dot(lo, seg)


def _in_proj_kernel(x_ref, g_ref, wqv_ref, wkt_ref, wr_ref, qv_ref, kt_ref, zr_ref):
    x = x_ref[...]
    ms = jnp.mean(x * x, axis=-1, keepdims=True)
    h = (x * lax.rsqrt(ms + NORM_EPS) * g_ref[...]).astype(BF16)
    qv = _dot(h, wqv_ref[...]).astype(BF16)
    for j in range(2 * ATTN_HEADS):
        qv_ref[j] = qv[:, j * ATTN_V_DIM:(j + 1) * ATTN_V_DIM]
    kt = _dot_nt(wkt_ref[...], h).astype(BF16)
    for j in range(ATTN_HEADS):
        kt_ref[j, 0] = kt[j * ATTN_V_DIM:(j + 1) * ATTN_V_DIM, :]
    zr_ref[...] = _dot(h, wr_ref[...])


def in_proj(x, g, w_qv, w_kt, w_r, tm):
    n = x.shape[0]
    full = lambda a: pl.BlockSpec(a.shape, lambda i: (0, 0))
    return pl.pallas_call(
        _in_proj_kernel,
        grid=(n // tm,),
        in_specs=[
            pl.BlockSpec((tm, D_MODEL), lambda i: (i, 0)),
            pl.BlockSpec((1, D_MODEL), lambda i: (0, 0)),
            full(w_qv), full(w_kt), full(w_r),
        ],
        out_specs=[
            pl.BlockSpec((2 * ATTN_HEADS, tm, ATTN_V_DIM), lambda i: (0, i, 0)),
            pl.BlockSpec((ATTN_HEADS, 1, ATTN_V_DIM, tm), lambda i: (0, i, 0, 0)),
            pl.BlockSpec((tm, RWKV_COLS_PAD), lambda i: (i, 0)),
        ],
        out_shape=[
            jax.ShapeDtypeStruct((2 * ATTN_HEADS, n, ATTN_V_DIM), BF16),
            jax.ShapeDtypeStruct((ATTN_HEADS, n // tm, ATTN_V_DIM, tm), BF16),
            jax.ShapeDtypeStruct((n, RWKV_COLS_PAD), F32),
        ],
        compiler_params=_cparams(("parallel",)),
        name="in_proj",
    )(x, g, w_qv, w_kt, w_r)


ROW_BLOCK = 64
POS_BITS = 9
M_INIT = -1e30
LANES = 128
QUERY_TILES_PER_STEP = 2


def _attn_kernel(kap_ref, lq_ref, q_ref, k_ref, v_ref, g_ref, qx_ref, kx_ref, dbias_ref, o_ref,
                 q2_ref, s_ref, p_ref, m_ref, al_ref, sh_ref, acc_ref, *, seq, tq, qsub, lambda_init):
    h = pl.program_id(1)
    nk = seq // tq
    kap = kap_ref[h]
    lax.fori_loop(0, qsub, functools.partial(
        _attn_query_tile, pl.program_id(2) * qsub, kap, lq_ref, q_ref, k_ref, v_ref, g_ref, qx_ref, kx_ref,
        dbias_ref, o_ref, q2_ref, s_ref, p_ref, m_ref, al_ref, sh_ref, acc_ref, nk, tq, lambda_init), 0)


def _attn_query_tile(i0, kap, lq_ref, q_ref, k_ref, v_ref, g_ref, qx_ref, kx_ref, dbias_ref, o_ref,
                     q2_ref, s_ref, p_ref, m_ref, al_ref, sh_ref, acc_ref, nk, tq, lambda_init, sub, carry):
    i = i0 + sub
    qrows = pl.ds(pl.multiple_of(sub * tq, tq), tq)
    q = q_ref[qrows, :]
    lane = lax.broadcasted_iota(jnp.int32, q.shape, 1)
    zero = jnp.zeros_like(q)
    q2 = jnp.concatenate([jnp.where(lane < ATTN_QK_DIM, q, zero),
                          jnp.where(lane >= ATTN_QK_DIM, q, zero)], axis=0)
    qx = qx_ref[...]
    qx2 = jnp.concatenate([qx, qx], axis=0)
    q2_ref[0] = jnp.concatenate([q2, qx2], axis=1)
    q2_ref[1] = jnp.concatenate([q2, -qx2], axis=1)
    kx = kx_ref[...]
    ones = jnp.ones((tq, ATTN_V_DIM), BF16)
    m_ref[...] = jnp.full(m_ref.shape, M_INIT, F32)
    acc_ref[...] = jnp.zeros(acc_ref.shape, F32)

    def tile_of(step):
        jj = step - 1
        return jnp.where(step == 0, i, jj + (jj >= i).astype(jnp.int32))

    def scores(step, buf):
        j = tile_of(step)
        k_aug = jnp.concatenate([k_ref[j], kx], axis=0)
        s_ref[buf] = _dot(q2_ref[(j > i).astype(jnp.int32)], k_aug)

    def softmax_and_values(step, buf):
        j = tile_of(step)
        kappa = -kap * jnp.abs(i - j).astype(F32)
        start = pl.multiple_of(j * tq, tq)
        v_aug = jnp.concatenate([v_ref[pl.ds(start, tq), :], ones], axis=1)
        for rb in range(2 * tq // ROW_BLOCK):
            rows = slice(rb * ROW_BLOCK, (rb + 1) * ROW_BLOCK)
            m_old = m_ref[rows, :]
            row_max = jnp.max(s_ref[buf, rows, :], axis=-1, keepdims=True)
            m_new = jnp.maximum(m_old, jnp.broadcast_to(row_max, m_old.shape) + kappa)
            al_ref[rows, :] = jnp.exp2(m_old - m_new)
            sh_ref[rows, :] = m_new - kappa
            m_ref[rows, :] = m_new
        for rb in range(2 * tq // ROW_BLOCK):
            rows = slice(rb * ROW_BLOCK, (rb + 1) * ROW_BLOCK)
            shift = pltpu.repeat(sh_ref[rows, :], tq // LANES, axis=1)
            p_ref[rows, :] = jnp.exp2(s_ref[buf, rows, :] - shift).astype(BF16)
        acc_ref[...] = (pltpu.repeat(al_ref[...], 2 * ATTN_V_DIM // LANES, axis=1) * acc_ref[...]
                        + _dot(p_ref[...], v_aug))

    s_diag = _dot(q2, k_ref[i])
    s_ref[0] = (s_diag.reshape(2, tq, tq) + dbias_ref[...][None]).reshape(2 * tq, tq)

    def pair(t, carry):
        scores(2 * t + 1, 1)
        softmax_and_values(2 * t, 0)
        scores(jnp.minimum(2 * t + 2, nk - 1), 0)
        softmax_and_values(2 * t + 1, 1)
        return carry

    lax.fori_loop(0, nk // 2, pair, 0)

    lq = lq_ref[...]
    lam = (jnp.exp(jnp.sum(lq[0:1] * lq[1:2], axis=-1, keepdims=True))
           - jnp.exp(jnp.sum(lq[2:3] * lq[3:4], axis=-1, keepdims=True)) + lambda_init)
    acc = acc_ref[...]
    o_all = acc[:, :ATTN_V_DIM] / acc[:, ATTN_V_DIM:]
    o = o_all[:tq] - lam * o_all[tq:]
    ms = jnp.mean(o * o, axis=-1, keepdims=True)
    o = o * lax.rsqrt(ms + NORM_EPS) * g_ref[...] * (1.0 - lambda_init)
    o_ref[qrows, :] = o.astype(BF16)
    return carry


def attention_constants(tq):
    assert tq <= (1 << POS_BITS)
    slopes = 2.0 ** (-8.0 * jnp.arange(1, ATTN_HEADS + 1, dtype=F32) / ATTN_HEADS)
    c = slopes * math.log2(math.e)
    cb = c[:, None] * (2.0 ** jnp.arange(POS_BITS, dtype=F32))[None, :]
    hi = cb.astype(BF16).astype(F32)
    lo = (cb - hi).astype(BF16).astype(F32)
    pos = jnp.arange(tq)
    bits = ((pos[:, None] >> jnp.arange(POS_BITS)[None, :]) & 1).astype(F32)
    hb = jnp.broadcast_to(bits[None], (ATTN_HEADS, tq, POS_BITS))
    const = lambda x: jnp.broadcast_to(x[:, None, :], (ATTN_HEADS, tq, POS_BITS))
    pad = jnp.zeros((ATTN_HEADS, tq, ATTN_V_DIM - 4 * POS_BITS), F32)
    qx = jnp.concatenate([const(hi), const(lo), hb, hb, pad], axis=-1).astype(BF16)
    kx = jnp.swapaxes(jnp.concatenate([hb, hb, const(-hi), const(-lo), pad], axis=-1), 1, 2).astype(BF16)
    dist = jnp.abs(pos[:, None] - pos[None, :]).astype(F32)
    dbias = -c[:, None, None] * dist[None]
    kap = c * tq
    return kap, qx, kx, dbias


def diff_attention(qv, kt, consts, lq, subln_g, batch, seq, lambda_init, tq):
    n = batch * seq
    nq = seq // tq
    kap, qx, kx, dbias = consts
    assert nq % 2 == 0, "key tiles are processed in pairs"
    qsub = QUERY_TILES_PER_STEP if nq % QUERY_TILES_PER_STEP == 0 else 1
    nq //= qsub
    kern = functools.partial(_attn_kernel, seq=seq, tq=tq, qsub=qsub, lambda_init=lambda_init)
    per_head = lambda w: pl.BlockSpec((None, tq, w), lambda b, h, i: (h, 0, 0))
    return pl.pallas_call(
        kern,
        grid=(batch, ATTN_HEADS, nq),
        in_specs=[
            pl.BlockSpec(memory_space=pltpu.SMEM),
            pl.BlockSpec((4, ATTN_QK_DIM), lambda b, h, i: (0, 0)),
            pl.BlockSpec((None, qsub * tq, ATTN_V_DIM), lambda b, h, i: (h, b * nq + i, 0)),
            pl.BlockSpec((None, seq // tq, ATTN_V_DIM, tq), lambda b, h, i: (h, b, 0, 0)),
            pl.BlockSpec((None, seq, ATTN_V_DIM), lambda b, h, i: (ATTN_HEADS + h, b, 0)),
            pl.BlockSpec((1, ATTN_V_DIM), lambda b, h, i: (0, 0)),
            per_head(ATTN_V_DIM), pl.BlockSpec((None, ATTN_V_DIM, tq), lambda b, h, i: (h, 0, 0)), per_head(tq),
        ],
        out_specs=pl.BlockSpec((None, qsub * tq, ATTN_V_DIM), lambda b, h, i: (h, b * nq + i, 0)),
        out_shape=jax.ShapeDtypeStruct((ATTN_HEADS, n, ATTN_V_DIM), BF16),
        scratch_shapes=[
            pltpu.VMEM((2, 2 * tq, 2 * ATTN_V_DIM), BF16),
            pltpu.VMEM((2, 2 * tq, tq), F32),
            pltpu.VMEM((2 * tq, tq), BF16),
            pltpu.VMEM((2 * tq, LANES), F32),
            pltpu.VMEM((2 * tq, LANES), F32),
            pltpu.VMEM((2 * tq, LANES), F32),
            pltpu.VMEM((2 * tq, 2 * ATTN_V_DIM), F32),
        ],
        compiler_params=_cparams(("parallel", "parallel", "parallel")),
        name="diff_attention",
    )(kap, lq, qv, kt, qv, subln_g, qx, kx, dbias)


def _softplus(x):
    return jnp.maximum(x, 0.0) + jnp.log(1.0 + jnp.exp(-jnp.abs(x)))


def _sigmoid(x):
    return 1.0 / (1.0 + jnp.exp(-x))


def _prep_kernel(z_ref, zp_ref, zn_ref, mu_ref, w0_ref, w2_ref, a0_ref, a2_ref, g2_ref,
                 kk_ref, ka_ref, rk_ref, seg_ref,
                 r_ref, v_ref, a_ref, lw_ref, kd_ref, bb_ref, bonus_ref, gate_ref, *, seq, tm):
    i = pl.program_id(0)
    z = z_ref[...]
    row = lax.broadcasted_iota(jnp.int32, z.shape, 0)
    first = (i * tm) % seq == 0
    last = ((i + 1) * tm) % seq == 0
    zp = jnp.where(first, 0.0, zp_ref[7:8, :])
    zn = jnp.where(last, 0.0, zn_ref[0:1, :])
    prev = jnp.where(row == 0, zp, pltpu.roll(z, 1, 0))
    nxt = jnp.where(row == tm - 1, zn, pltpu.roll(z, tm - 1, 0))
    z = z + mu_ref[0:1, :] * (prev - z) + mu_ref[1:2, :] * (nxt - z)

    w = RWKV_WIDTH
    r = z[:, 0:w]
    k = z[:, w:2 * w]
    v = z[:, 2 * w:3 * w]
    seg = seg_ref[...]
    kk = k * kk_ref[...]
    ss = _seg_sum(kk * kk, seg)
    kk = kk / jnp.maximum(jnp.sqrt(ss), 1e-12)
    r_ref[...] = r.astype(BF16)
    v_ref[...] = v.astype(BF16)
    a_ref[...] = (-kk).astype(BF16)
    for d in range(2):
        xw = z[:, RWKV_MAIN + d * LORA_PAD:RWKV_MAIN + (d + 1) * LORA_PAD]
        xa = z[:, RWKV_MAIN + (2 + d) * LORA_PAD:RWKV_MAIN + (3 + d) * LORA_PAD]
        wlog = -_softplus(-(w0_ref[d:d + 1, :] + _dot(jnp.tanh(xw).astype(BF16), w2_ref[d]))) - 0.5
        lw_ref[d] = -jnp.exp(wlog)
        iclr = _sigmoid(a0_ref[d:d + 1, :] + _dot(xa.astype(BF16), a2_ref[d]))
        kd_ref[d] = (k * (1.0 + (iclr - 1.0) * ka_ref[...])).astype(BF16)
        bb_ref[d] = (kk * iclr).astype(BF16)
    xg = z[:, RWKV_MAIN + 4 * LORA_PAD:RWKV_MAIN + 5 * LORA_PAD]
    gate_ref[...] = _dot(_sigmoid(xg).astype(BF16), g2_ref[...]).astype(BF16)
    bonus_ref[...] = (_seg_sum(r * k * rk_ref[...], seg) * v).astype(BF16)


def rwkv_prep(zr, mu, w0, w2, a0, a2, g2, k_k, k_a, r_k, seg, seq, tm):
    n, cols = zr.shape
    nb8 = n // 8
    tb8 = tm // 8
    full = lambda shape: pl.BlockSpec(shape, lambda i: (0,) * len(shape))
    tok = pl.BlockSpec((tm, RWKV_WIDTH), lambda i: (i, 0))
    tok2 = pl.BlockSpec((2, tm, RWKV_WIDTH), lambda i: (0, i, 0))
    sds = jax.ShapeDtypeStruct((n, RWKV_WIDTH), BF16)
    sds2 = jax.ShapeDtypeStruct((2, n, RWKV_WIDTH), BF16)
    sds2_f32 = jax.ShapeDtypeStruct((2, n, RWKV_WIDTH), F32)
    return pl.pallas_call(
        functools.partial(_prep_kernel, seq=seq, tm=tm),
        grid=(n // tm,),
        in_specs=[
            pl.BlockSpec((tm, cols), lambda i: (i, 0)),
            pl.BlockSpec((8, cols), lambda i: (jnp.maximum(i * tb8 - 1, 0), 0)),
            pl.BlockSpec((8, cols), lambda i: (jnp.minimum((i + 1) * tb8, nb8 - 1), 0)),
            full((2, cols)), full((2, RWKV_WIDTH)), full((2, LORA_PAD, RWKV_WIDTH)),
            full((2, RWKV_WIDTH)), full((2, LORA_PAD, RWKV_WIDTH)), full((LORA_PAD, RWKV_WIDTH)),
            full((1, RWKV_WIDTH)), full((1, RWKV_WIDTH)), full((1, RWKV_WIDTH)),
            full((RWKV_WIDTH, RWKV_WIDTH)),
        ],
        out_specs=[tok, tok, tok, tok2, tok2, tok2, tok, tok],
        out_shape=[sds, sds, sds, sds2_f32, sds2, sds2, sds, sds],
        compiler_params=_cparams(("parallel",)),
        name="rwkv_prep",
    )(zr, zr, zr, mu, w0, w2, a0, a2, g2, k_k, k_a, r_k, seg)


def _scan_kernel(r_ref, v_ref, a_ref, lw_ref, k_ref, b_ref, tri_ref, strict_ref, incl_ref, lvl_ref, eye_ref, bd_ref,
                 y_ref, h_ref, *, tb, nsub):
    d = pl.program_id(1)
    i = pl.program_id(2)
    nc = tb // CHUNK
    ngrp = RWKV_WIDTH // GROUP_W

    @pl.when(i == 0)
    def _():
        h_ref[...] = jnp.zeros(h_ref.shape, F32)

    tri = tri_ref[...]
    strict = strict_ref[...] > 0.5
    incl = incl_ref[...] > 0.5
    eye = eye_ref[...]
    bdm = bd_ref[...] > 0.5
    ones16 = jnp.ones((16, GROUP_W), BF16)
    row16 = lax.broadcasted_iota(jnp.int32, (16, GROUP_W), 0)

    def bd(x):
        return jnp.where(bdm, jnp.concatenate([x] * GROUP, axis=0), 0.0).astype(BF16)

    def split(x):
        hi = x.astype(BF16)
        return hi, (x - hi.astype(F32)).astype(BF16)

    def local_stages(base):
        slices = []
        for cc in range(nc):
            ci = cc + d * (nc - 1 - 2 * cc)
            slices.append(pl.ds(pl.multiple_of(base + ci * CHUNK, CHUNK), CHUNK))
        groups = [slice(g * GROUP_W, (g + 1) * GROUP_W) for g in range(ngrp)]
        units = [dict(sl=sl, cols=cols) for sl in slices for cols in groups]
        for c in units:
            lw = lw_ref[c["sl"], c["cols"]]
            r = r_ref[c["sl"], c["cols"]].astype(F32)
            v = v_ref[c["sl"], c["cols"]].astype(F32)
            a = a_ref[c["sl"], c["cols"]].astype(F32)
            k = k_ref[c["sl"], c["cols"]].astype(F32)
            b = b_ref[c["sl"], c["cols"]].astype(F32)
            hi, lo = split(lw)
            cs = _dot(tri, jnp.concatenate([hi, lo], axis=1))
            lcum = cs[:, :GROUP_W] + cs[:, GROUP_W:]
            ltot = jnp.where(d == 0, lcum[CHUNK - 1:CHUNK], lcum[0:1])
            at = a * jnp.exp(lcum - lw)
            rt = r * jnp.exp(lcum)
            einv = jnp.exp(-lcum)
            eend = jnp.exp(ltot - lcum)
            c["x"] = jnp.concatenate([at, rt], axis=0).astype(BF16)
            c["ybk"] = jnp.concatenate([bd(b * einv), bd(k * einv)], axis=0)
            c["bkw"] = jnp.concatenate([b * eend, k * eend], axis=0).astype(BF16)
            c["vbd"] = bd(v)
            c["v"] = v.astype(BF16)
            wtot = jnp.exp(ltot)
            whi = wtot.astype(BF16).astype(F32)
            c["whl"] = jnp.where(row16 == 0, whi, jnp.where(row16 == 1, wtot - whi, 0.0)).astype(BF16)
        for c in units:
            amat = _dot_nt(c["x"], c.pop("ybk"))
            c["aab"] = jnp.where(strict, amat[:CHUNK, :GROUP_W], 0.0)
            c["aak"] = jnp.where(strict, amat[:CHUNK, GROUP_W:], 0.0).astype(BF16)
            c["ar"] = jnp.concatenate([jnp.where(incl, amat[CHUNK:, :GROUP_W], 0.0),
                                       jnp.where(incl, amat[CHUNK:, GROUP_W:], 0.0)], axis=1).astype(BF16)
            c["p"] = eye + jnp.where(lvl_ref[0] > 0.5, c["aab"], 0.0)
        for lvl in range(1, N_LEVELS):
            for c in units:
                e = jnp.where(lvl_ref[lvl] > 0.5, c["aab"], 0.0)
                c["g"] = _dot(c["p"].astype(BF16), bd(e)).astype(BF16)
            for c in units:
                c["p"] = c["p"] + _dot(c.pop("g"), bd(c["p"]))
        for c in units:
            c["p"] = c["p"].astype(BF16)
            c["av"] = _dot(c.pop("aak"), c["vbd"])
            c["wcol"] = _dot_tn(c.pop("whl"), ones16)
        return units

    def carried_steps(units):
        for cc in range(nc):
            cu = units[cc * ngrp:(cc + 1) * ngrp]
            hs = [h_ref[g] for g in range(ngrp)]
            hx = [_dot(c["x"], h.astype(BF16)) for c, h in zip(cu, hs)]
            us = [_dot(c["p"], bd(x[:CHUNK] + c["av"])) for c, x in zip(cu, hx)]
            for g, (c, h, x, u) in enumerate(zip(cu, hs, hx, us)):
                y_ref[c["sl"], c["cols"]] = x[CHUNK:] + _dot(c["ar"], jnp.concatenate([bd(u), c["vbd"]], axis=0))
                upd = _dot_tn(c["bkw"], jnp.concatenate([u.astype(BF16), c["v"]], axis=0))
                h_ref[g] = c["wcol"] * h + jnp.where(bdm, upd, 0.0)

    def block(sub, carry):
        carried_steps(local_stages((sub + d * (nsub - 1 - 2 * sub)) * tb))
        return carry

    lax.fori_loop(0, nsub, block, 0)


def rwkv_scan(r, v, a, lw, kd, bb, consts, batch, seq, tb):
    n = batch * seq
    nsub = SCAN_BLOCKS_PER_STEP if seq % (tb * SCAN_BLOCKS_PER_STEP) == 0 else 1
    nblk = seq // (tb * nsub)
    tri, strict, incl, lvl, eye, bdm = consts

    def tmap(b, d, i):
        return (b * nblk + i + d * (nblk - 1 - 2 * i), 0)

    def tmap_d(b, d, i):
        return (d, b * nblk + i + d * (nblk - 1 - 2 * i), 0)

    tok = pl.BlockSpec((nsub * tb, RWKV_WIDTH), tmap)
    tok_d = pl.BlockSpec((None, nsub * tb, RWKV_WIDTH), tmap_d)
    per_d = lambda shape: pl.BlockSpec((None,) + shape, lambda b, d, i: (d,) + (0,) * len(shape))
    full = lambda shape: pl.BlockSpec(shape, lambda b, d, i: (0, 0))
    return pl.pallas_call(
        functools.partial(_scan_kernel, tb=tb, nsub=nsub),
        grid=(batch, 2, nblk),
        in_specs=[tok, tok, tok, tok_d, tok_d, tok_d,
                  per_d((CHUNK, CHUNK)), per_d((CHUNK, GROUP * CHUNK)), per_d((CHUNK, GROUP * CHUNK)),
                  per_d((N_LEVELS, CHUNK, GROUP * CHUNK)),
                  full((CHUNK, GROUP * CHUNK)), full((GROUP_W, GROUP_W))],
        out_specs=tok_d,
        out_shape=jax.ShapeDtypeStruct((2, n, RWKV_WIDTH), F32),
        scratch_shapes=[pltpu.VMEM((RWKV_WIDTH // GROUP_W, GROUP_W, GROUP_W), F32)],
        compiler_params=_cparams(("parallel", "parallel", "arbitrary")),
        name="rwkv_scan",
    )(r, v, a, lw, kd, bb, tri, strict, incl, lvl, eye, bdm)


def scan_constants():
    t = jnp.arange(CHUNK)
    before_eq = (t[None, :] <= t[:, None])
    tri = jnp.stack([before_eq, before_eq.T]).astype(F32)
    incl = jnp.tile(tri, (1, 1, GROUP))
    tri = tri.astype(BF16)
    eye = jnp.tile(jnp.eye(CHUNK, dtype=F32), (1, GROUP))
    strict = incl - eye[None]
    levels = []
    for lv in range(N_LEVELS):
        half = t // (1 << lv)
        fwd = ((half[:, None] // 2 == half[None, :] // 2) & (half[:, None] % 2 == 1) & (half[None, :] % 2 == 0))
        levels.append(jnp.stack([fwd, fwd.T]))
    lvl = jnp.tile(jnp.stack(levels, axis=1).astype(F32), (1, 1, 1, GROUP))
    blk = jnp.arange(GROUP_W) // RWKV_HEAD
    bdm = (blk[:, None] == blk[None, :]).astype(F32)
    return tri, strict, incl, lvl, eye, bdm


def _post_kernel(yf_ref, yb_ref, bonus_ref, gate_ref, oa_ref, x_ref, lw_ref, lb_ref, seg_ref,
                 wo_ref, g2_ref, wr_ref, xo_ref, h_ref, aff_ref):
    y = yf_ref[...] + yb_ref[...]
    seg = seg_ref[...]
    mean = _seg_sum(y, seg) * (1.0 / RWKV_HEAD)
    yc = y - mean
    var = _seg_sum(yc * yc, seg) * (1.0 / RWKV_HEAD)
    yn = yc * lax.rsqrt(var + GN_EPS) * lw_ref[...] + lb_ref[...]
    orw = ((yn + bonus_ref[...]) * gate_ref[...]).astype(BF16)
    oa = jnp.concatenate([oa_ref[j] for j in range(ATTN_HEADS)], axis=1)
    x = (x_ref[...] + _dot(oa, wo_ref[:ATTN_WIDTH, :]) + _dot(orw, wo_ref[ATTN_WIDTH:, :]))
    xo_ref[...] = x
    ms = jnp.mean(x * x, axis=-1, keepdims=True)
    h = x * lax.rsqrt(ms + NORM_EPS) * g2_ref[...]
    h_ref[...] = h.astype(BF16)
    logits = lax.dot_general(wr_ref[...], h, (((1,), (1,)), ((), ())),
                             preferred_element_type=F32, precision=HIGHEST)
    logits = logits - jnp.max(logits, axis=0, keepdims=True)
    e = jnp.exp(logits)
    aff_ref[...] = e / jnp.sum(e, axis=0, keepdims=True)


def post_mix(y2, bonus, gate, oa, x, lnx_w, lnx_b, seg, w_out, norm2_g, w_router_t, tm):
    n = x.shape[0]
    full = lambda shape: pl.BlockSpec(shape, lambda i: (0,) * len(shape))
    tok = lambda w: pl.BlockSpec((tm, w), lambda i: (i, 0))
    return pl.pallas_call(
        _post_kernel,
        grid=(n // tm,),
        in_specs=[
            pl.BlockSpec((None, tm, RWKV_WIDTH), lambda i: (0, i, 0)),
            pl.BlockSpec((None, tm, RWKV_WIDTH), lambda i: (1, i, 0)),
            tok(RWKV_WIDTH), tok(RWKV_WIDTH),
            pl.BlockSpec((ATTN_HEADS, tm, ATTN_V_DIM), lambda i: (0, i, 0)), tok(D_MODEL),
            full((1, RWKV_WIDTH)), full((1, RWKV_WIDTH)), full((RWKV_WIDTH, RWKV_WIDTH)),
            full((D_MODEL, D_MODEL)), full((1, D_MODEL)), full((N_EXPERTS, D_MODEL)),
        ],
        out_specs=[tok(D_MODEL), tok(D_MODEL), pl.BlockSpec((N_EXPERTS, tm), lambda i: (0, i))],
        out_shape=[
            jax.ShapeDtypeStruct((n, D_MODEL), F32),
            jax.ShapeDtypeStruct((n, D_MODEL), BF16),
            jax.ShapeDtypeStruct((N_EXPERTS, n), F32),
        ],
        compiler_params=_cparams(("parallel",)),
        name="post_mix",
    )(y2, y2, bonus, gate, oa, x, lnx_w, lnx_b, seg, w_out, norm2_g, w_router_t)


COMBINE_RANKS = 256
ROUTE_TOKENS = 1024


def _thresh_kernel(aff_ref, thr_ref, need_ref, *, cap):
    bits = pltpu.bitcast(aff_ref[...], jnp.int32)
    e = bits.shape[0]

    def body(it, thr):
        cand = thr | jnp.left_shift(jnp.int32(1), 30 - it)
        cnt = jnp.sum(jnp.where(bits >= cand, 1.0, 0.0), axis=1, keepdims=True)
        return jnp.where(cnt >= cap, cand, thr)

    thr = lax.fori_loop(0, 31, body, jnp.zeros((e, 1), jnp.int32))
    n_gt = jnp.sum(jnp.where(bits > thr, 1.0, 0.0), axis=1, keepdims=True)
    thr_ref[...] = jnp.broadcast_to(thr, thr_ref.shape)
    need_ref[...] = jnp.broadcast_to(cap - n_gt, need_ref.shape)


def _prefix_kernel(aff_ref, thr_ref, need_ref, u_ref, pos_ref, cend_ref, carry_ref):
    @pl.when(pl.program_id(0) == 0)
    def _():
        carry_ref[...] = jnp.zeros(carry_ref.shape, F32)

    bits = pltpu.bitcast(aff_ref[...], jnp.int32)
    e, tr = bits.shape
    thr = thr_ref[:, 0:1]
    need = need_ref[:, 0:1]
    gt = bits > thr
    eq = bits == thr
    marks = jnp.concatenate([jnp.where(gt, 1.0, 0.0), jnp.where(eq, 1.0, 0.0)], axis=0).astype(BF16)
    pre = _dot(marks, u_ref[...]) + carry_ref[...]
    pgt = pre[:e]
    peq = pre[e:]
    sel = jnp.where(gt, 1.0, jnp.where(eq, jnp.where(peq <= need, 1.0, 0.0), 0.0))
    rank = pgt + jnp.minimum(peq, need) - 1.0
    pos_ref[...] = jnp.where(sel > 0.5, rank, -1.0).astype(jnp.int32)
    carry_ref[...] = pre[:, tr - 1:tr]
    cend_ref[...] = jnp.broadcast_to(rank[:, tr - 1:tr] + 1.0, cend_ref.shape)


def select_tokens(aff_t, cap, tr):
    e, n = aff_t.shape
    nb = n // tr
    thr, need = pl.pallas_call(
        functools.partial(_thresh_kernel, cap=float(cap)),
        out_shape=[jax.ShapeDtypeStruct((e, 128), jnp.int32), jax.ShapeDtypeStruct((e, 128), F32)],
        compiler_params=pltpu.CompilerParams(vmem_limit_bytes=VMEM_LIMIT),
        name="route_threshold",
    )(aff_t)
    upper = (jnp.arange(tr)[:, None] <= jnp.arange(tr)[None, :]).astype(BF16)
    pos, cend = pl.pallas_call(
        _prefix_kernel,
        grid=(nb,),
        in_specs=[pl.BlockSpec((e, tr), lambda i: (0, i)),
                  pl.BlockSpec((e, 128), lambda i: (0, 0)),
                  pl.BlockSpec((e, 128), lambda i: (0, 0)),
                  pl.BlockSpec((tr, tr), lambda i: (0, 0))],
        out_specs=[pl.BlockSpec((e, tr), lambda i: (0, i)),
                   pl.BlockSpec((None, e, 128), lambda i: (i, 0, 0))],
        out_shape=[jax.ShapeDtypeStruct((e, n), jnp.int32), jax.ShapeDtypeStruct((nb, e, 128), F32)],
        scratch_shapes=[pltpu.VMEM((2 * e, 1), F32)],
        compiler_params=_cparams(("arbitrary",)),
        name="route_prefix",
    )(aff_t, thr, need, upper)
    return pos, cend[:, :, 0].T.astype(jnp.int32)


def _pick(onehot, values):
    return jnp.sum(jnp.where(onehot, values[None, :], 0), axis=1)


def gather_table(cend, rb, n_rb):
    nb = cend.shape[0]
    cstart = jnp.concatenate([jnp.zeros((1,), jnp.int32), cend[:-1]])
    count = cend - cstart
    ob_first = jnp.minimum(cstart // rb, n_rb - 1)
    ob_last = jnp.where(count > 0, (cend - 1) // rb, ob_first)
    nitems = jnp.where(count > 0, ob_last - ob_first + 1, 0)
    cum = jnp.cumsum(nitems)
    start = cum - nitems
    w = jnp.arange(nb + n_rb, dtype=jnp.int32)
    owner = (w[:, None] >= start[None, :]) & (w[:, None] < cum[None, :])
    valid = w < cum[-1]
    tb = jnp.where(valid, _pick(owner, jnp.arange(nb, dtype=jnp.int32)), nb - 1)
    ob = jnp.where(valid, _pick(owner, ob_first - start) + w, n_rb - 1)
    prev_ob = jnp.concatenate([jnp.full((1,), -1, jnp.int32), ob[:-1]])
    next_ob = jnp.concatenate([ob[1:], jnp.full((1,), -1, jnp.int32)])
    next_valid = jnp.concatenate([valid[1:], jnp.zeros((1,), bool)])
    first = valid & (ob != prev_ob)
    last = valid & ((ob != next_ob) | ~next_valid)
    flags = first.astype(jnp.int32) + 2 * last.astype(jnp.int32) + 4 * valid.astype(jnp.int32)
    return ob, tb, flags


def routing_tables(cend, rb, n_rb):
    ob, tb, flags = jax.vmap(lambda c: gather_table(c, rb, n_rb))(cend)
    return ob.reshape(-1), tb.reshape(-1), flags.reshape(-1)


def combine_table(cend, rb, n_rb):
    e, nb = cend.shape
    cstart = jnp.concatenate([jnp.zeros((e, 1), jnp.int32), cend[:, :-1]], axis=1)
    ob_first = jnp.minimum(cstart // rb, n_rb - 1)
    ob_last = jnp.where(cend > cstart, (cend - 1) // rb, ob_first)
    obf = ob_first.T.reshape(-1)
    nitems = (ob_last - ob_first + 1).T.reshape(-1)
    cum = jnp.cumsum(nitems)
    start = cum - nitems
    length = nb * e + e * n_rb
    w = jnp.arange(length, dtype=jnp.int32)
    owner = (w[:, None] >= start[None, :]) & (w[:, None] < cum[None, :])
    valid = w < cum[-1]
    pair = jnp.where(valid, _pick(owner, jnp.arange(nb * e, dtype=jnp.int32)), nb * e - 1)
    ob = jnp.where(valid, _pick(owner, obf - start) + w, obf[-1] + nitems[-1] - 1)
    ex = pair % e
    first = valid & (_pick(owner, start) == w) & (ex == 0)
    last = valid & (_pick(owner, cum) - 1 == w) & (ex == e - 1)
    flags = first.astype(jnp.int32) + 2 * last.astype(jnp.int32) + 4 * valid.astype(jnp.int32)
    return pair // e, ex, ob, flags


def _gather_ffn_kernel(ob_ref, tb_ref, fl_ref, pos_ref, aff_ref, h_ref, wg_ref, wu_ref, wd_ref, o_ref,
                       x_acc, g_acc, *, length, rb, tg):
    idx = pl.program_id(0) * length + pl.program_id(1)
    flags = fl_ref[idx]
    ob = ob_ref[idx]

    @pl.when((flags & 1) != 0)
    def _():
        x_acc[...] = jnp.zeros(x_acc.shape, F32)
        g_acc[...] = jnp.zeros(g_acc.shape, F32)

    @pl.when((flags & 4) != 0)
    def _():
        rank = ob * rb + lax.broadcasted_iota(jnp.int32, (rb, tg), 0)
        match = pos_ref[...] == rank
        x_acc[...] += _dot(jnp.where(match, 1.0, 0.0).astype(BF16), h_ref[...])
        g_acc[...] += jnp.sum(jnp.where(match, aff_ref[...], 0.0), axis=1, keepdims=True)

    @pl.when((flags & 2) != 0)
    def _():
        x = x_acc[...].astype(BF16)
        hg = _dot(x, wg_ref[...])
        hu = _dot(x, wu_ref[...])
        act = (hg * _sigmoid(hg) * hu).astype(BF16)
        o_ref[...] = (_dot(act, wd_ref[...]) * g_acc[...]).astype(BF16)


def gather_ffn(tables, pos3, aff3, h, wg, wu, wd, cap, rb, tg):
    e = pos3.shape[0]
    n = h.shape[0]
    length = n // tg + cap // rb
    ob_of, tb_of, flags = tables
    tok = pl.BlockSpec((None, 1, tg), lambda e, w, ob, tb, fl: (e, 0, tb[e * length + w]))
    wspec = lambda r, c: pl.BlockSpec((None, r, c), lambda e, w, ob, tb, fl: (e, 0, 0))
    return pl.pallas_call(
        functools.partial(_gather_ffn_kernel, length=length, rb=rb, tg=tg),
        grid_spec=pltpu.PrefetchScalarGridSpec(
            num_scalar_prefetch=3,
            grid=(e, length),
            in_specs=[tok, tok,
                      pl.BlockSpec((tg, D_MODEL), lambda e, w, ob, tb, fl: (tb[e * length + w], 0)),
                      wspec(D_MODEL, EXPERT_FF), wspec(D_MODEL, EXPERT_FF), wspec(EXPERT_FF, D_MODEL)],
            out_specs=pl.BlockSpec((None, rb, D_MODEL), lambda e, w, ob, tb, fl: (e, ob[e * length + w], 0)),
            scratch_shapes=[pltpu.VMEM((rb, D_MODEL), F32), pltpu.VMEM((rb, 1), F32)],
        ),
        out_shape=jax.ShapeDtypeStruct((e, cap, D_MODEL), BF16),
        compiler_params=_cparams(("parallel", "arbitrary")),
        name="gather_ffn",
    )(ob_of, tb_of, flags, pos3, aff3, h, wg, wu, wd)


def _combine_kernel(tb_ref, ex_ref, ob_ref, fl_ref, pos_ref, ye_ref, x_ref, g_ref, o_ref, *, rb, tr, final):
    w = pl.program_id(0)
    flags = fl_ref[w]

    @pl.when((flags & 1) != 0)
    def _():
        o_ref[...] = x_ref[...]

    @pl.when((flags & 4) != 0)
    def _():
        rank = ob_ref[w] * rb + lax.broadcasted_iota(jnp.int32, (rb, tr), 0)
        onehot = jnp.where(pos_ref[...] == rank, 1.0, 0.0).astype(BF16)
        o_ref[...] += _dot_tn(onehot, ye_ref[...])

    if final:
        @pl.when((flags & 2) != 0)
        def _():
            x = o_ref[...]
            ms = jnp.mean(x * x, axis=-1, keepdims=True)
            o_ref[...] = x * lax.rsqrt(ms + NORM_EPS) * g_ref[...]


def combine(table, pos3, ye, x, final_g, tr, rb, final):
    n = x.shape[0]
    tb_of, ex_of, ob_of, flags = table
    return pl.pallas_call(
        functools.partial(_combine_kernel, rb=rb, tr=tr, final=final),
        grid_spec=pltpu.PrefetchScalarGridSpec(
            num_scalar_prefetch=4,
            grid=(tb_of.shape[0],),
            in_specs=[pl.BlockSpec((None, 1, tr), lambda w, tb, ex, ob, fl: (ex[w], 0, tb[w])),
                      pl.BlockSpec((None, rb, D_MODEL), lambda w, tb, ex, ob, fl: (ex[w], ob[w], 0)),
                      pl.BlockSpec((tr, D_MODEL), lambda w, tb, ex, ob, fl: (tb[w], 0)),
                      pl.BlockSpec((1, D_MODEL), lambda w, tb, ex, ob, fl: (0, 0))],
            out_specs=pl.BlockSpec((tr, D_MODEL), lambda w, tb, ex, ob, fl: (tb[w], 0)),
        ),
        out_shape=jax.ShapeDtypeStruct((n, D_MODEL), F32),
        compiler_params=_cparams(("arbitrary",)),
        name="combine",
    )(tb_of, ex_of, ob_of, flags, pos3, ye, x, final_g)


def expert_choice_ffn(x, h, aff_t, wg, wu, wd, final_g, tr, final):
    e, n = aff_t.shape
    cap = CAPACITY_FACTOR * n // e
    pos, cend = select_tokens(aff_t, cap, tr)
    pos3 = pos.reshape(e, 1, n)
    gpb = min(ROUTE_TOKENS, n) // tr
    cend = cend[:, gpb - 1::gpb]
    ye = gather_ffn(routing_tables(cend, tr, cap // tr), pos3, aff_t.reshape(e, 1, n), h,
                    wg, wu, wd, cap, tr, gpb * tr)
    rb = min(COMBINE_RANKS, tr)
    return combine(combine_table(cend, rb, cap // rb), pos3, ye, x, final_g, gpb * tr, rb, final)


def _pad_cols(w, width):
    return jnp.pad(w, ((0, 0),) * (w.ndim - 1) + ((0, width - w.shape[-1]),))


def _pad_lora_cols(z):
    o = RWKV_MAIN
    parts = [z[..., :o]]
    for width in (DECAY_LORA, DECAY_LORA, AAA_LORA, AAA_LORA, GATE_LORA):
        parts.append(_pad_cols(z[..., o:o + width], LORA_PAD))
        o += width
    return jnp.concatenate(parts, axis=-1)


def _pad_rows(w, rows):
    return jnp.pad(w, ((0, 0),) * (w.ndim - 2) + ((0, rows - w.shape[-2]), (0, 0)))


def prepare_weights(norm1_g, w_in, lambda_qk, subln_g, shift_mu, w0, w2, a0, a2, g2, k_k, k_a, r_k,
                    lnx_w, lnx_b, w_out, norm2_g, w_router, w_gate, w_up, w_down, final_g):
    scale = ATTN_QK_DIM ** -0.5 * math.log2(math.e)
    w_qv = jnp.concatenate([w_in[..., :ATTN_WIDTH] * scale, w_in[..., 2 * ATTN_WIDTH:ATTN_COLS]], axis=-1)
    w_kt = jnp.swapaxes(w_in[..., ATTN_WIDTH:2 * ATTN_WIDTH], 1, 2)
    blk = jnp.arange(RWKV_WIDTH) // RWKV_HEAD
    return dict(
        norm1_g=norm1_g[:, None, :], w_qv=w_qv.astype(BF16), w_kt=w_kt.astype(BF16),
        w_r=_pad_lora_cols(w_in[..., ATTN_COLS:]).astype(BF16), lambda_qk=lambda_qk, subln_g=subln_g[:, None, :],
        mu=_pad_lora_cols(shift_mu), w0=w0, w2=_pad_rows(w2, LORA_PAD).astype(BF16),
        a0=a0, a2=_pad_rows(a2, LORA_PAD).astype(BF16), g2=_pad_rows(g2, LORA_PAD).astype(BF16),
        k_k=k_k[:, None, :], k_a=k_a[:, None, :], r_k=r_k.reshape(DEPTH, 1, RWKV_WIDTH),
        lnx_w=lnx_w[:, None, :], lnx_b=lnx_b[:, None, :], w_out=w_out.astype(BF16),
        norm2_g=norm2_g[:, None, :], w_router_t=jnp.swapaxes(w_router, 1, 2),
        w_gate=w_gate.astype(BF16), w_up=w_up.astype(BF16), w_down=w_down.astype(BF16),
        final_g=final_g[None, :],
        seg=(blk[:, None] == blk[None, :]).astype(BF16),
        scan_consts=scan_constants(),
    )


def trunk(x, p, tiles):
    batch, seq, _ = x.shape
    n = batch * seq
    x = x.reshape(n, D_MODEL)
    attn_consts = attention_constants(tiles["tq"])
    for l in range(DEPTH):
        lambda_init = 0.8 - 0.6 * math.exp(-0.3 * l)
        qv, kt, zr = in_proj(x, p["norm1_g"][l], p["w_qv"][l], p["w_kt"][l], p["w_r"][l], tiles["tq"])
        oa = diff_attention(qv, kt, attn_consts, p["lambda_qk"][l], p["subln_g"][l], batch, seq, lambda_init,
                            tiles["tq"])
        r, v, a, lw, kd, bb, bonus, gate = rwkv_prep(
            zr, p["mu"][l], p["w0"][l], p["w2"][l], p["a0"][l], p["a2"][l], p["g2"][l],
            p["k_k"][l], p["k_a"][l], p["r_k"][l], p["seg"], seq, tiles["tm"])
        y2 = rwkv_scan(r, v, a, lw, kd, bb, p["scan_consts"], batch, seq, tiles["tb"])
        x, h, aff_t = post_mix(y2, bonus, gate, oa, x, p["lnx_w"][l], p["lnx_b"][l], p["seg"],
                               p["w_out"][l], p["norm2_g"][l], p["w_router_t"][l], tiles["tm"])
        x = expert_choice_ffn(x, h, aff_t, p["w_gate"][l], p["w_up"][l], p["w_down"][l], p["final_g"],
                              tiles["tr"], final=(l == DEPTH - 1))
    return x.reshape(batch, seq, D_MODEL)


def _tiles(seq):
    return dict(tm=min(512, seq), tq=min(512, seq), tb=min(256, seq), tr=512)


def kernel(x_prompt, x_sample, norm1_g, w_in, lambda_qk, subln_g, shift_mu, w0, w2, a0, a2, g2, k_k, k_a, r_k,
           lnx_w, lnx_b, w_out, norm2_g, w_router, w_gate, w_up, w_down, final_g):
    p = prepare_weights(norm1_g, w_in, lambda_qk, subln_g, shift_mu, w0, w2, a0, a2, g2, k_k, k_a, r_k,
                        lnx_w, lnx_b, w_out, norm2_g, w_router, w_gate, w_up, w_down, final_g)
    y_prompt = trunk(x_prompt, p, _tiles(x_prompt.shape[1]))
    y_sample = trunk(x_sample, p, _tiles(x_sample.shape[1]))
    return (y_prompt, y_sample)
```

```python
import functools
import math

import jax
import jax.numpy as jnp
from jax import lax
from jax.experimental import pallas as pl
from jax.experimental.pallas import tpu as pltpu

F32 = jnp.float32
BF16 = jnp.bfloat16
HIGHEST = lax.Precision.HIGHEST

D_MODEL = 1024
DEPTH = 4
ATTN_WIDTH = 512
ATTN_HEADS = 4
ATTN_QK_DIM = 64
ATTN_V_DIM = 128
RWKV_WIDTH = 512
RWKV_HEAD = 64
DECAY_LORA = 32
AAA_LORA = 32
GATE_LORA = 96
LORA_PAD = 128
ATTN_COLS = 3 * ATTN_WIDTH
RWKV_MAIN = 3 * RWKV_WIDTH
RWKV_COLS_PAD = RWKV_MAIN + 5 * LORA_PAD
N_EXPERTS = 16
EXPERT_FF = 2 * D_MODEL
CAPACITY_FACTOR = 2
NORM_EPS = 1e-6
GN_EPS = 64e-5

CHUNK = 64
GROUP = 4
GROUP_W = GROUP * RWKV_HEAD
N_LEVELS = CHUNK.bit_length() - 1
SCAN_BLOCKS_PER_STEP = 2
VMEM_LIMIT = 56 * 1024 * 1024


def _cparams(sem):
    return pltpu.CompilerParams(dimension_semantics=sem, vmem_limit_bytes=VMEM_LIMIT)


def _dot(a, b):
    return jnp.dot(a, b, preferred_element_type=F32)


def _dot_nt(a, b):
    return lax.dot_general(a, b, (((1,), (1,)), ((), ())), preferred_element_type=F32)


def _dot_tn(a, b):
    return lax.dot_general(a, b, (((0,), (0,)), ((), ())), preferred_element_type=F32)


def _seg_sum(x, seg):
    hi = x.astype(BF16)
    lo = (x - hi.astype(F32)).astype(BF16)
    return _dot(hi, seg) + _dot(lo, seg)


def _in_proj_kernel(x_ref, g_ref, wqv_ref, wkt_ref, wr_ref, qv_ref, kt_ref, zr_ref):
    x = x_ref[...]
    ms = jnp.mean(x * x, axis=-1, keepdims=True)
    h = (x * lax.rsqrt(ms + NORM_EPS) * g_ref[...]).astype(BF16)
    qv = _dot(h, wqv_ref[...]).astype(BF16)
    for j in range(2 * ATTN_HEADS):
        qv_ref[j] = qv[:, j * ATTN_V_DIM:(j + 1) * ATTN_V_DIM]
    kt = _dot_nt(wkt_ref[...], h).astype(BF16)
    for j in range(ATTN_HEADS):
        kt_ref[j, 0] = kt[j * ATTN_V_DIM:(j + 1) * ATTN_V_DIM, :]
    zr_ref[...] = _dot(h, wr_ref[...])


def in_proj(x, g, w_qv, w_kt, w_r, tm):
    n = x.shape[0]
    full = lambda a: pl.BlockSpec(a.shape, lambda i: (0, 0))
    return pl.pallas_call(
        _in_proj_kernel,
        grid=(n // tm,),
        in_specs=[
            pl.BlockSpec((tm, D_MODEL), lambda i: (i, 0)),
            pl.BlockSpec((1, D_MODEL), lambda i: (0, 0)),
            full(w_qv), full(w_kt), full(w_r),
        ],
        out_specs=[
            pl.BlockSpec((2 * ATTN_HEADS, tm, ATTN_V_DIM), lambda i: (0, i, 0)),
            pl.BlockSpec((ATTN_HEADS, 1, ATTN_V_DIM, tm), lambda i: (0, i, 0, 0)),
            pl.BlockSpec((tm, RWKV_COLS_PAD), lambda i: (i, 0)),
        ],
        out_shape=[
            jax.ShapeDtypeStruct((2 * ATTN_HEADS, n, ATTN_V_DIM), BF16),
            jax.ShapeDtypeStruct((ATTN_HEADS, n // tm, ATTN_V_DIM, tm), BF16),
            jax.ShapeDtypeStruct((n, RWKV_COLS_PAD), F32),
        ],
        compiler_params=_cparams(("parallel",)),
        name="in_proj",
    )(x, g, w_qv, w_kt, w_r)


ROW_BLOCK = 64
POS_BITS = 9
M_INIT = -1e30
LANES = 128
QUERY_TILES_PER_STEP = 2


def _attn_kernel(kap_ref, lq_ref, q_ref, k_ref, v_ref, g_ref, qx_ref, kx_ref, dbias_ref, o_ref,
                 q2_ref, s_ref, p_ref, m_ref, al_ref, sh_ref, acc_ref, *, seq, tq, qsub, lambda_init):
    h = pl.program_id(1)
    nk = seq // tq
    kap = kap_ref[h]
    lax.fori_loop(0, qsub, functools.partial(
        _attn_query_tile, pl.program_id(2) * qsub, kap, lq_ref, q_ref, k_ref, v_ref, g_ref, qx_ref, kx_ref,
        dbias_ref, o_ref, q2_ref, s_ref, p_ref, m_ref, al_ref, sh_ref, acc_ref, nk, tq, lambda_init), 0)


def _attn_query_tile(i0, kap, lq_ref, q_ref, k_ref, v_ref, g_ref, qx_ref, kx_ref, dbias_ref, o_ref,
                     q2_ref, s_ref, p_ref, m_ref, al_ref, sh_ref, acc_ref, nk, tq, lambda_init, sub, carry):
    i = i0 + sub
    qrows = pl.ds(pl.multiple_of(sub * tq, tq), tq)
    q = q_ref[qrows, :]
    lane = lax.broadcasted_iota(jnp.int32, q.shape, 1)
    zero = jnp.zeros_like(q)
    q2 = jnp.concatenate([jnp.where(lane < ATTN_QK_DIM, q, zero),
                          jnp.where(lane >= ATTN_QK_DIM, q, zero)], axis=0)
    qx = qx_ref[...]
    qx2 = jnp.concatenate([qx, qx], axis=0)
    q2_ref[0] = jnp.concatenate([q2, qx2], axis=1)
    q2_ref[1] = jnp.concatenate([q2, -qx2], axis=1)
    kx = kx_ref[...]
    ones = jnp.ones((tq, ATTN_V_DIM), BF16)
    m_ref[...] = jnp.full(m_ref.shape, M_INIT, F32)
    acc_ref[...] = jnp.zeros(acc_ref.shape, F32)

    def tile_of(step):
        jj = step - 1
        return jnp.where(step == 0, i, jj + (jj >= i).astype(jnp.int32))

    def scores(step, buf):
        j = tile_of(step)
        k_aug = jnp.concatenate([k_ref[j], kx], axis=0)
        s_ref[buf] = _dot(q2_ref[(j > i).astype(jnp.int32)], k_aug)

    def softmax_and_values(step, buf):
        j = tile_of(step)
        kappa = -kap * jnp.abs(i - j).astype(F32)
        start = pl.multiple_of(j * tq, tq)
        v_aug = jnp.concatenate([v_ref[pl.ds(start, tq), :], ones], axis=1)
        for rb in range(2 * tq // ROW_BLOCK):
            rows = slice(rb * ROW_BLOCK, (rb + 1) * ROW_BLOCK)
            m_old = m_ref[rows, :]
            row_max = jnp.max(s_ref[buf, rows, :], axis=-1, keepdims=True)
            m_new = jnp.maximum(m_old, jnp.broadcast_to(row_max, m_old.shape) + kappa)
            al_ref[rows, :] = jnp.exp2(m_old - m_new)
            sh_ref[rows, :] = m_new - kappa
            m_ref[rows, :] = m_new
        for rb in range(2 * tq // ROW_BLOCK):
            rows = slice(rb * ROW_BLOCK, (rb + 1) * ROW_BLOCK)
            shift = pltpu.repeat(sh_ref[rows, :], tq // LANES, axis=1)
            p_ref[rows, :] = jnp.exp2(s_ref[buf, rows, :] - shift).astype(BF16)
        acc_ref[...] = (pltpu.repeat(al_ref[...], 2 * ATTN_V_DIM // LANES, axis=1) * acc_ref[...]
                        + _dot(p_ref[...], v_aug))

    s_diag = _dot(q2, k_ref[i])
    s_ref[0] = (s_diag.reshape(2, tq, tq) + dbias_ref[...][None]).reshape(2 * tq, tq)

    def pair(t, carry):
        scores(2 * t + 1, 1)
        softmax_and_values(2 * t, 0)
        scores(jnp.minimum(2 * t + 2, nk - 1), 0)
        softmax_and_values(2 * t + 1, 1)
        return carry

    lax.fori_loop(0, nk // 2, pair, 0)

    lq = lq_ref[...]
    lam = (jnp.exp(jnp.sum(lq[0:1] * lq[1:2], axis=-1, keepdims=True))
           - jnp.exp(jnp.sum(lq[2:3] * lq[3:4], axis=-1, keepdims=True)) + lambda_init)
    acc = acc_ref[...]
    o_all = acc[:, :ATTN_V_DIM] / acc[:, ATTN_V_DIM:]
    o = o_all[:tq] - lam * o_all[tq:]
    ms = jnp.mean(o * o, axis=-1, keepdims=True)
    o = o * lax.rsqrt(ms + NORM_EPS) * g_ref[...] * (1.0 - lambda_init)
    o_ref[qrows, :] = o.astype(BF16)
    return carry


def attention_constants(tq):
    assert tq <= (1 << POS_BITS)
    slopes = 2.0 ** (-8.0 * jnp.arange(1, ATTN_HEADS + 1, dtype=F32) / ATTN_HEADS)
    c = slopes * math.log2(math.e)
    cb = c[:, None] * (2.0 ** jnp.arange(POS_BITS, dtype=F32))[None, :]
    hi = cb.astype(BF16).astype(F32)
    lo = (cb - hi).astype(BF16).astype(F32)
    pos = jnp.arange(tq)
    bits = ((pos[:, None] >> jnp.arange(POS_BITS)[None, :]) & 1).astype(F32)
    hb = jnp.broadcast_to(bits[None], (ATTN_HEADS, tq, POS_BITS))
    const = lambda x: jnp.broadcast_to(x[:, None, :], (ATTN_HEADS, tq, POS_BITS))
    pad = jnp.zeros((ATTN_HEADS, tq, ATTN_V_DIM - 4 * POS_BITS), F32)
    qx = jnp.concatenate([const(hi), const(lo), hb, hb, pad], axis=-1).astype(BF16)
    kx = jnp.swapaxes(jnp.concatenate([hb, hb, const(-hi), const(-lo), pad], axis=-1), 1, 2).astype(BF16)
    dist = jnp.abs(pos[:, None] - pos[None, :]).astype(F32)
    dbias = -c[:, None, None] * dist[None]
    kap = c * tq
    return kap, qx, kx, dbias


def diff_attention(qv, kt, consts, lq, subln_g, batch, seq, lambda_init, tq):
    n = batch * seq
    nq = seq // tq
    kap, qx, kx, dbias = consts
    assert nq % 2 == 0, "key tiles are processed in pairs"
    qsub = QUERY_TILES_PER_STEP if nq % QUERY_TILES_PER_STEP == 0 else 1
    nq //= qsub
    kern = functools.partial(_attn_kernel, seq=seq, tq=tq, qsub=qsub, lambda_init=lambda_init)
    per_head = lambda w: pl.BlockSpec((None, tq, w), lambda b, h, i: (h, 0, 0))
    return pl.pallas_call(
        kern,
        grid=(batch, ATTN_HEADS, nq),
        in_specs=[
            pl.BlockSpec(memory_space=pltpu.SMEM),
            pl.BlockSpec((4, ATTN_QK_DIM), lambda b, h, i: (0, 0)),
            pl.BlockSpec((None, qsub * tq, ATTN_V_DIM), lambda b, h, i: (h, b * nq + i, 0)),
            pl.BlockSpec((None, seq // tq, ATTN_V_DIM, tq), lambda b, h, i: (h, b, 0, 0)),
            pl.BlockSpec((None, seq, ATTN_V_DIM), lambda b, h, i: (ATTN_HEADS + h, b, 0)),
            pl.BlockSpec((1, ATTN_V_DIM), lambda b, h, i: (0, 0)),
            per_head(ATTN_V_DIM), pl.BlockSpec((None, ATTN_V_DIM, tq), lambda b, h, i: (h, 0, 0)), per_head(tq),
        ],
        out_specs=pl.BlockSpec((None, qsub * tq, ATTN_V_DIM), lambda b, h, i: (h, b * nq + i, 0)),
        out_shape=jax.ShapeDtypeStruct((ATTN_HEADS, n, ATTN_V_DIM), BF16),
        scratch_shapes=[
            pltpu.VMEM((2, 2 * tq, 2 * ATTN_V_DIM), BF16),
            pltpu.VMEM((2, 2 * tq, tq), F32),
            pltpu.VMEM((2 * tq, tq), BF16),
            pltpu.VMEM((2 * tq, LANES), F32),
            pltpu.VMEM((2 * tq, LANES), F32),
            pltpu.VMEM((2 * tq, LANES), F32),
            pltpu.VMEM((2 * tq, 2 * ATTN_V_DIM), F32),
        ],
        compiler_params=_cparams(("parallel", "parallel", "parallel")),
        name="diff_attention",
    )(kap, lq, qv, kt, qv, subln_g, qx, kx, dbias)


def _softplus(x):
    return jnp.maximum(x, 0.0) + jnp.log(1.0 + jnp.exp(-jnp.abs(x)))


def _sigmoid(x):
    return 1.0 / (1.0 + jnp.exp(-x))


def _prep_kernel(z_ref, zp_ref, zn_ref, mu_ref, w0_ref, w2_ref, a0_ref, a2_ref, g2_ref,
                 kk_ref, ka_ref, rk_ref, seg_ref,
                 r_ref, v_ref, a_ref, lw_ref, kd_ref, bb_ref, bonus_ref, gate_ref, *, seq, tm):
    i = pl.program_id(0)
    z = z_ref[...]
    row = lax.broadcasted_iota(jnp.int32, z.shape, 0)
    first = (i * tm) % seq == 0
    last = ((i + 1) * tm) % seq == 0
    zp = jnp.where(first, 0.0, zp_ref[7:8, :])
    zn = jnp.where(last, 0.0, zn_ref[0:1, :])
    prev = jnp.where(row == 0, zp, pltpu.roll(z, 1, 0))
    nxt = jnp.where(row == tm - 1, zn, pltpu.roll(z, tm - 1, 0))
    z = z + mu_ref[0:1, :] * (prev - z) + mu_ref[1:2, :] * (nxt - z)

    w = RWKV_WIDTH
    r = z[:, 0:w]
    k = z[:, w:2 * w]
    v = z[:, 2 * w:3 * w]
    seg = seg_ref[...]
    kk = k * kk_ref[...]
    ss = _seg_sum(kk * kk, seg)
    kk = kk / jnp.maximum(jnp.sqrt(ss), 1e-12)
    r_ref[...] = r.astype(BF16)
    v_ref[...] = v.astype(BF16)
    a_ref[...] = (-kk).astype(BF16)
    for d in range(2):
        xw = z[:, RWKV_MAIN + d * LORA_PAD:RWKV_MAIN + (d + 1) * LORA_PAD]
        xa = z[:, RWKV_MAIN + (2 + d) * LORA_PAD:RWKV_MAIN + (3 + d) * LORA_PAD]
        wlog = -_softplus(-(w0_ref[d:d + 1, :] + _dot(jnp.tanh(xw).astype(BF16), w2_ref[d]))) - 0.5
        lw_ref[d] = -jnp.exp(wlog)
        iclr = _sigmoid(a0_ref[d:d + 1, :] + _dot(xa.astype(BF16), a2_ref[d]))
        kd_ref[d] = (k * (1.0 + (iclr - 1.0) * ka_ref[...])).astype(BF16)
        bb_ref[d] = (kk * iclr).astype(BF16)
    xg = z[:, RWKV_MAIN + 4 * LORA_PAD:RWKV_MAIN + 5 * LORA_PAD]
    gate_ref[...] = _dot(_sigmoid(xg).astype(BF16), g2_ref[...]).astype(BF16)
    bonus_ref[...] = (_seg_sum(r * k * rk_ref[...], seg) * v).astype(BF16)


def rwkv_prep(zr, mu, w0, w2, a0, a2, g2, k_k, k_a, r_k, seg, seq, tm):
    n, cols = zr.shape
    nb8 = n // 8
    tb8 = tm // 8
    full = lambda shape: pl.BlockSpec(shape, lambda i: (0,) * len(shape))
    tok = pl.BlockSpec((tm, RWKV_WIDTH), lambda i: (i, 0))
    tok2 = pl.BlockSpec((2, tm, RWKV_WIDTH), lambda i: (0, i, 0))
    sds = jax.ShapeDtypeStruct((n, RWKV_WIDTH), BF16)
    sds2 = jax.ShapeDtypeStruct((2, n, RWKV_WIDTH), BF16)
    sds2_f32 = jax.ShapeDtypeStruct((2, n, RWKV_WIDTH), F32)
    return pl.pallas_call(
        functools.partial(_prep_kernel, seq=seq, tm=tm),
        grid=(n // tm,),
        in_specs=[
            pl.BlockSpec((tm, cols), lambda i: (i, 0)),
            pl.BlockSpec((8, cols), lambda i: (jnp.maximum(i * tb8 - 1, 0), 0)),
            pl.BlockSpec((8, cols), lambda i: (jnp.minimum((i + 1) * tb8, nb8 - 1), 0)),
            full((2, cols)), full((2, RWKV_WIDTH)), full((2, LORA_PAD, RWKV_WIDTH)),
            full((2, RWKV_WIDTH)), full((2, LORA_PAD, RWKV_WIDTH)), full((LORA_PAD, RWKV_WIDTH)),
            full((1, RWKV_WIDTH)), full((1, RWKV_WIDTH)), full((1, RWKV_WIDTH)),
            full((RWKV_WIDTH, RWKV_WIDTH)),
        ],
        out_specs=[tok, tok, tok, tok2, tok2, tok2, tok, tok],
        out_shape=[sds, sds, sds, sds2_f32, sds2, sds2, sds, sds],
        compiler_params=_cparams(("parallel",)),
        name="rwkv_prep",
    )(zr, zr, zr, mu, w0, w2, a0, a2, g2, k_k, k_a, r_k, seg)


def _scan_kernel(r_ref, v_ref, a_ref, lw_ref, k_ref, b_ref, tri_ref, strict_ref, incl_ref, lvl_ref, eye_ref, bd_ref,
                 y_ref, h_ref, *, tb, nsub):
    d = pl.program_id(1)
    i = pl.program_id(2)
    nc = tb // CHUNK
    ngrp = RWKV_WIDTH // GROUP_W

    @pl.when(i == 0)
    def _():
        h_ref[...] = jnp.zeros(h_ref.shape, F32)

    tri = tri_ref[...]
    strict = strict_ref[...] > 0.5
    incl = incl_ref[...] > 0.5
    eye = eye_ref[...]
    bdm = bd_ref[...] > 0.5
    ones16 = jnp.ones((16, GROUP_W), BF16)
    row16 = lax.broadcasted_iota(jnp.int32, (16, GROUP_W), 0)

    def bd(x):
        return jnp.where(bdm, jnp.concatenate([x] * GROUP, axis=0), 0.0).astype(BF16)

    def split(x):
        hi = x.astype(BF16)
        return hi, (x - hi.astype(F32)).astype(BF16)

    def local_stages(base):
        slices = []
        for cc in range(nc):
            ci = cc + d * (nc - 1 - 2 * cc)
            slices.append(pl.ds(pl.multiple_of(base + ci * CHUNK, CHUNK), CHUNK))
        groups = [slice(g * GROUP_W, (g + 1) * GROUP_W) for g in range(ngrp)]
        units = [dict(sl=sl, cols=cols) for sl in slices for cols in groups]
        for c in units:
            lw = lw_ref[c["sl"], c["cols"]]
            r = r_ref[c["sl"], c["cols"]].astype(F32)
            v = v_ref[c["sl"], c["cols"]].astype(F32)
            a = a_ref[c["sl"], c["cols"]].astype(F32)
            k = k_ref[c["sl"], c["cols"]].astype(F32)
            b = b_ref[c["sl"], c["cols"]].astype(F32)
            hi, lo = split(lw)
            cs = _dot(tri, jnp.concatenate([hi, lo], axis=1))
            lcum = cs[:, :GROUP_W] + cs[:, GROUP_W:]
            ltot = jnp.where(d == 0, lcum[CHUNK - 1:CHUNK], lcum[0:1])
            at = a * jnp.exp(lcum - lw)
            rt = r * jnp.exp(lcum)
            einv = jnp.exp(-lcum)
            eend = jnp.exp(ltot - lcum)
            c["x"] = jnp.concatenate([at, rt], axis=0).astype(BF16)
            c["ybk"] = jnp.concatenate([bd(b * einv), bd(k * einv)], axis=0)
            c["bkw"] = jnp.concatenate([b * eend, k * eend], axis=0).astype(BF16)
            c["vbd"] = bd(v)
            c["v"] = v.astype(BF16)
            wtot = jnp.exp(ltot)
            whi = wtot.astype(BF16).astype(F32)
            c["whl"] = jnp.where(row16 == 0, whi, jnp.where(row16 == 1, wtot - whi, 0.0)).astype(BF16)
        for c in units:
            amat = _dot_nt(c["x"], c.pop("ybk"))
            c["aab"] = jnp.where(strict, amat[:CHUNK, :GROUP_W], 0.0)
            c["aak"] = jnp.where(strict, amat[:CHUNK, GROUP_W:], 0.0).astype(BF16)
            c["ar"] = jnp.concatenate([jnp.where(incl, amat[CHUNK:, :GROUP_W], 0.0),
                                       jnp.where(incl, amat[CHUNK:, GROUP_W:], 0.0)], axis=1).astype(BF16)
            c["p"] = eye + jnp.where(lvl_ref[0] > 0.5, c["aab"], 0.0)
        for lvl in range(1, N_LEVELS):
            for c in units:
                e = jnp.where(lvl_ref[lvl] > 0.5, c["aab"], 0.0)
                c["g"] = _dot(c["p"].astype(BF16), bd(e)).astype(BF16)
            for c in units:
                c["p"] = c["p"] + _dot(c.pop("g"), bd(c["p"]))
        for c in units:
            c["p"] = c["p"].astype(BF16)
            c["av"] = _dot(c.pop("aak"), c["vbd"])
            c["wcol"] = _dot_tn(c.pop("whl"), ones16)
        return units

    def carried_steps(units):
        for cc in range(nc):
            cu = units[cc * ngrp:(cc + 1) * ngrp]
            hs = [h_ref[g] for g in range(ngrp)]
            hx = [_dot(c["x"], h.astype(BF16)) for c, h in zip(cu, hs)]
            us = [_dot(c["p"], bd(x[:CHUNK] + c["av"])) for c, x in zip(cu, hx)]
            for g, (c, h, x, u) in enumerate(zip(cu, hs, hx, us)):
                y_ref[c["sl"], c["cols"]] = x[CHUNK:] + _dot(c["ar"], jnp.concatenate([bd(u), c["vbd"]], axis=0))
                upd = _dot_tn(c["bkw"], jnp.concatenate([u.astype(BF16), c["v"]], axis=0))
                h_ref[g] = c["wcol"] * h + jnp.where(bdm, upd, 0.0)

    def block(sub, carry):
        carried_steps(local_stages((sub + d * (nsub - 1 - 2 * sub)) * tb))
        return carry

    lax.fori_loop(0, nsub, block, 0)


def rwkv_scan(r, v, a, lw, kd, bb, consts, batch, seq, tb):
    n = batch * seq
    nsub = SCAN_BLOCKS_PER_STEP if seq % (tb * SCAN_BLOCKS_PER_STEP) == 0 else 1
    nblk = seq // (tb * nsub)
    tri, strict, incl, lvl, eye, bdm = consts

    def tmap(b, d, i):
        return (b * nblk + i + d * (nblk - 1 - 2 * i), 0)

    def tmap_d(b, d, i):
        return (d, b * nblk + i + d * (nblk - 1 - 2 * i), 0)

    tok = pl.BlockSpec((nsub * tb, RWKV_WIDTH), tmap)
    tok_d = pl.BlockSpec((None, nsub * tb, RWKV_WIDTH), tmap_d)
    per_d = lambda shape: pl.BlockSpec((None,) + shape, lambda b, d, i: (d,) + (0,) * len(shape))
    full = lambda shape: pl.BlockSpec(shape, lambda b, d, i: (0, 0))
    return pl.pallas_call(
        functools.partial(_scan_kernel, tb=tb, nsub=nsub),
        grid=(batch, 2, nblk),
        in_specs=[tok, tok, tok, tok_d, tok_d, tok_d,
                  per_d((CHUNK, CHUNK)), per_d((CHUNK, GROUP * CHUNK)), per_d((CHUNK, GROUP * CHUNK)),
                  per_d((N_LEVELS, CHUNK, GROUP * CHUNK)),
                  full((CHUNK, GROUP * CHUNK)), full((GROUP_W, GROUP_W))],
        out_specs=tok_d,
        out_shape=jax.ShapeDtypeStruct((2, n, RWKV_WIDTH), F32),
        scratch_shapes=[pltpu.VMEM((RWKV_WIDTH // GROUP_W, GROUP_W, GROUP_W), F32)],
        compiler_params=_cparams(("parallel", "parallel", "arbitrary")),
        name="rwkv_scan",
    )(r, v, a, lw, kd, bb, tri, strict, incl, lvl, eye, bdm)


def scan_constants():
    t = jnp.arange(CHUNK)
    before_eq = (t[None, :] <= t[:, None])
    tri = jnp.stack([before_eq, before_eq.T]).astype(F32)
    incl = jnp.tile(tri, (1, 1, GROUP))
    tri = tri.astype(BF16)
    eye = jnp.tile(jnp.eye(CHUNK, dtype=F32), (1, GROUP))
    strict = incl - eye[None]
    levels = []
    for lv in range(N_LEVELS):
        half = t // (1 << lv)
        fwd = ((half[:, None] // 2 == half[None, :] // 2) & (half[:, None] % 2 == 1) & (half[None, :] % 2 == 0))
        levels.append(jnp.stack([fwd, fwd.T]))
    lvl = jnp.tile(jnp.stack(levels, axis=1).astype(F32), (1, 1, 1, GROUP))
    blk = jnp.arange(GROUP_W) // RWKV_HEAD
    bdm = (blk[:, None] == blk[None, :]).astype(F32)
    return tri, strict, incl, lvl, eye, bdm


def _post_kernel(yf_ref, yb_ref, bonus_ref, gate_ref, oa_ref, x_ref, lw_ref, lb_ref, seg_ref,
                 wo_ref, g2_ref, wr_ref, xo_ref, h_ref, aff_ref):
    y = yf_ref[...] + yb_ref[...]
    seg = seg_ref[...]
    mean = _seg_sum(y, seg) * (1.0 / RWKV_HEAD)
    yc = y - mean
    var = _seg_sum(yc * yc, seg) * (1.0 / RWKV_HEAD)
    yn = yc * lax.rsqrt(var + GN_EPS) * lw_ref[...] + lb_ref[...]
    orw = ((yn + bonus_ref[...]) * gate_ref[...]).astype(BF16)
    oa = jnp.concatenate([oa_ref[j] for j in range(ATTN_HEADS)], axis=1)
    x = (x_ref[...] + _dot(oa, wo_ref[:ATTN_WIDTH, :]) + _dot(orw, wo_ref[ATTN_WIDTH:, :]))
    xo_ref[...] = x
    ms = jnp.mean(x * x, axis=-1, keepdims=True)
    h = x * lax.rsqrt(ms + NORM_EPS) * g2_ref[...]
    h_ref[...] = h.astype(BF16)
    logits = lax.dot_general(wr_ref[...], h, (((1,), (1,)), ((), ())),
                             preferred_element_type=F32, precision=HIGHEST)
    logits = logits - jnp.max(logits, axis=0, keepdims=True)
    e = jnp.exp(logits)
    aff_ref[...] = e / jnp.sum(e, axis=0, keepdims=True)


def post_mix(y2, bonus, gate, oa, x, lnx_w, lnx_b, seg, w_out, norm2_g, w_router_t, tm):
    n = x.shape[0]
    full = lambda shape: pl.BlockSpec(shape, lambda i: (0,) * len(shape))
    tok = lambda w: pl.BlockSpec((tm, w), lambda i: (i, 0))
    return pl.pallas_call(
        _post_kernel,
        grid=(n // tm,),
        in_specs=[
            pl.BlockSpec((None, tm, RWKV_WIDTH), lambda i: (0, i, 0)),
            pl.BlockSpec((None, tm, RWKV_WIDTH), lambda i: (1, i, 0)),
            tok(RWKV_WIDTH), tok(RWKV_WIDTH),
            pl.BlockSpec((ATTN_HEADS, tm, ATTN_V_DIM), lambda i: (0, i, 0)), tok(D_MODEL),
            full((1, RWKV_WIDTH)), full((1, RWKV_WIDTH)), full((RWKV_WIDTH, RWKV_WIDTH)),
            full((D_MODEL, D_MODEL)), full((1, D_MODEL)), full((N_EXPERTS, D_MODEL)),
        ],
        out_specs=[tok(D_MODEL), tok(D_MODEL), pl.BlockSpec((N_EXPERTS, tm), lambda i: (0, i))],
        out_shape=[
            jax.ShapeDtypeStruct((n, D_MODEL), F32),
            jax.ShapeDtypeStruct((n, D_MODEL), BF16),
            jax.ShapeDtypeStruct((N_EXPERTS, n), F32),
        ],
        compiler_params=_cparams(("parallel",)),
        name="post_mix",
    )(y2, y2, bonus, gate, oa, x, lnx_w, lnx_b, seg, w_out, norm2_g, w_router_t)


COMBINE_RANKS = 256
ROUTE_TOKENS = 1024


def _thresh_kernel(aff_ref, thr_ref, need_ref, *, cap):
    bits = pltpu.bitcast(aff_ref[...], jnp.int32)
    e = bits.shape[0]

    def body(it, thr):
        cand = thr | jnp.left_shift(jnp.int32(1), 30 - it)
        cnt = jnp.sum(jnp.where(bits >= cand, 1.0, 0.0), axis=1, keepdims=True)
        return jnp.where(cnt >= cap, cand, thr)

    thr = lax.fori_loop(0, 31, body, jnp.zeros((e, 1), jnp.int32))
    n_gt = jnp.sum(jnp.where(bits > thr, 1.0, 0.0), axis=1, keepdims=True)
    thr_ref[...] = jnp.broadcast_to(thr, thr_ref.shape)
    need_ref[...] = jnp.broadcast_to(cap - n_gt, need_ref.shape)


def _prefix_kernel(aff_ref, thr_ref, need_ref, u_ref, pos_ref, cend_ref, carry_ref):
    @pl.when(pl.program_id(0) == 0)
    def _():
        carry_ref[...] = jnp.zeros(carry_ref.shape, F32)

    bits = pltpu.bitcast(aff_ref[...], jnp.int32)
    e, tr = bits.shape
    thr = thr_ref[:, 0:1]
    need = need_ref[:, 0:1]
    gt = bits > thr
    eq = bits == thr
    marks = jnp.concatenate([jnp.where(gt, 1.0, 0.0), jnp.where(eq, 1.0, 0.0)], axis=0).astype(BF16)
    pre = _dot(marks, u_ref[...]) + carry_ref[...]
    pgt = pre[:e]
    peq = pre[e:]
    sel = jnp.where(gt, 1.0, jnp.where(eq, jnp.where(peq <= need, 1.0, 0.0), 0.0))
    rank = pgt + jnp.minimum(peq, need) - 1.0
    pos_ref[...] = jnp.where(sel > 0.5, rank, -1.0).astype(jnp.int32)
    carry_ref[...] = pre[:, tr - 1:tr]
    cend_ref[...] = jnp.broadcast_to(rank[:, tr - 1:tr] + 1.0, cend_ref.shape)


def select_tokens(aff_t, cap, tr):
    e, n = aff_t.shape
    nb = n // tr
    thr, need = pl.pallas_call(
        functools.partial(_thresh_kernel, cap=float(cap)),
        out_shape=[jax.ShapeDtypeStruct((e, 128), jnp.int32), jax.ShapeDtypeStruct((e, 128), F32)],
        compiler_params=pltpu.CompilerParams(vmem_limit_bytes=VMEM_LIMIT),
        name="route_threshold",
    )(aff_t)
    upper = (jnp.arange(tr)[:, None] <= jnp.arange(tr)[None, :]).astype(BF16)
    pos, cend = pl.pallas_call(
        _prefix_kernel,
        grid=(nb,),
        in_specs=[pl.BlockSpec((e, tr), lambda i: (0, i)),
                  pl.BlockSpec((e, 128), lambda i: (0, 0)),
                  pl.BlockSpec((e, 128), lambda i: (0, 0)),
                  pl.BlockSpec((tr, tr), lambda i: (0, 0))],
        out_specs=[pl.BlockSpec((e, tr), lambda i: (0, i)),
                   pl.BlockSpec((None, e, 128), lambda i: (i, 0, 0))],
        out_shape=[jax.ShapeDtypeStruct((e, n), jnp.int32), jax.ShapeDtypeStruct((nb, e, 128), F32)],
        scratch_shapes=[pltpu.VMEM((2 * e, 1), F32)],
        compiler_params=_cparams(("arbitrary",)),
        name="route_prefix",
    )(aff_t, thr, need, upper)
    return pos, cend[:, :, 0].T.astype(jnp.int32)


def _pick(onehot, values):
    return jnp.sum(jnp.where(onehot, values[None, :], 0), axis=1)


def gather_table(cend, rb, n_rb):
    nb = cend.shape[0]
    cstart = jnp.concatenate([jnp.zeros((1,), jnp.int32), cend[:-1]])
    count = cend - cstart
    ob_first = jnp.minimum(cstart // rb, n_rb - 1)
    ob_last = jnp.where(count > 0, (cend - 1) // rb, ob_first)
    nitems = jnp.where(count > 0, ob_last - ob_first + 1, 0)
    cum = jnp.cumsum(nitems)
    start = cum - nitems
    w = jnp.arange(nb + n_rb, dtype=jnp.int32)
    owner = (w[:, None] >= start[None, :]) & (w[:, None] < cum[None, :])
    valid = w < cum[-1]
    tb = jnp.where(valid, _pick(owner, jnp.arange(nb, dtype=jnp.int32)), nb - 1)
    ob = jnp.where(valid, _pick(owner, ob_first - start) + w, n_rb - 1)
    prev_ob = jnp.concatenate([jnp.full((1,), -1, jnp.int32), ob[:-1]])
    next_ob = jnp.concatenate([ob[1:], jnp.full((1,), -1, jnp.int32)])
    next_valid = jnp.concatenate([valid[1:], jnp.zeros((1,), bool)])
    first = valid & (ob != prev_ob)
    last = valid & ((ob != next_ob) | ~next_valid)
    flags = first.astype(jnp.int32) + 2 * last.astype(jnp.int32) + 4 * valid.astype(jnp.int32)
    return ob, tb, flags


def routing_tables(cend, rb, n_rb):
    ob, tb, flags = jax.vmap(lambda c: gather_table(c, rb, n_rb))(cend)
    return ob.reshape(-1), tb.reshape(-1), flags.reshape(-1)


def combine_table(cend, rb, n_rb):
    e, nb = cend.shape
    cstart = jnp.concatenate([jnp.zeros((e, 1), jnp.int32), cend[:, :-1]], axis=1)
    ob_first = jnp.minimum(cstart // rb, n_rb - 1)
    ob_last = jnp.where(cend > cstart, (cend - 1) // rb, ob_first)
    obf = ob_first.T.reshape(-1)
    nitems = (ob_last - ob_first + 1).T.reshape(-1)
    cum = jnp.cumsum(nitems)
    start = cum - nitems
    length = nb * e + e * n_rb
    w = jnp.arange(length, dtype=jnp.int32)
    owner = (w[:, None] >= start[None, :]) & (w[:, None] < cum[None, :])
    valid = w < cum[-1]
    pair = jnp.where(valid, _pick(owner, jnp.arange(nb * e, dtype=jnp.int32)), nb * e - 1)
    ob = jnp.where(valid, _pick(owner, obf - start) + w, obf[-1] + nitems[-1] - 1)
    ex = pair % e
    first = valid & (_pick(owner, start) == w) & (ex == 0)
    last = valid & (_pick(owner, cum) - 1 == w) & (ex == e - 1)
    flags = first.astype(jnp.int32) + 2 * last.astype(jnp.int32) + 4 * valid.astype(jnp.int32)
    return pair // e, ex, ob, flags


def _gather_ffn_kernel(ob_ref, tb_ref, fl_ref, pos_ref, aff_ref, h_ref, wg_ref, wu_ref, wd_ref, o_ref,
                       x_acc, g_acc, *, length, rb, tg):
    idx = pl.program_id(0) * length + pl.program_id(1)
    flags = fl_ref[idx]
    ob = ob_ref[idx]

    @pl.when((flags & 1) != 0)
    def _():
        x_acc[...] = jnp.zeros(x_acc.shape, F32)
        g_acc[...] = jnp.zeros(g_acc.shape, F32)

    @pl.when((flags & 4) != 0)
    def _():
        rank = ob * rb + lax.broadcasted_iota(jnp.int32, (rb, tg), 0)
        match = pos_ref[...] == rank
        x_acc[...] += _dot(jnp.where(match, 1.0, 0.0).astype(BF16), h_ref[...])
        g_acc[...] += jnp.sum(jnp.where(match, aff_ref[...], 0.0), axis=1, keepdims=True)

    @pl.when((flags & 2) != 0)
    def _():
        x = x_acc[...].astype(BF16)
        hg = _dot(x, wg_ref[...])
        hu = _dot(x, wu_ref[...])
        act = (hg * _sigmoid(hg) * hu).astype(BF16)
        o_ref[...] = (_dot(act, wd_ref[...]) * g_acc[...]).astype(BF16)


def gather_ffn(tables, pos3, aff3, h, wg, wu, wd, cap, rb, tg):
    e = pos3.shape[0]
    n = h.shape[0]
    length = n // tg + cap // rb
    ob_of, tb_of, flags = tables
    tok = pl.BlockSpec((None, 1, tg), lambda e, w, ob, tb, fl: (e, 0, tb[e * length + w]))
    wspec = lambda r, c: pl.BlockSpec((None, r, c), lambda e, w, ob, tb, fl: (e, 0, 0))
    return pl.pallas_call(
        functools.partial(_gather_ffn_kernel, length=length, rb=rb, tg=tg),
        grid_spec=pltpu.PrefetchScalarGridSpec(
            num_scalar_prefetch=3,
            grid=(e, length),
            in_specs=[tok, tok,
                      pl.BlockSpec((tg, D_MODEL), lambda e, w, ob, tb, fl: (tb[e * length + w], 0)),
                      wspec(D_MODEL, EXPERT_FF), wspec(D_MODEL, EXPERT_FF), wspec(EXPERT_FF, D_MODEL)],
            out_specs=pl.BlockSpec((None, rb, D_MODEL), lambda e, w, ob, tb, fl: (e, ob[e * length + w], 0)),
            scratch_shapes=[pltpu.VMEM((rb, D_MODEL), F32), pltpu.VMEM((rb, 1), F32)],
        ),
        out_shape=jax.ShapeDtypeStruct((e, cap, D_MODEL), BF16),
        compiler_params=_cparams(("parallel", "arbitrary")),
        name="gather_ffn",
    )(ob_of, tb_of, flags, pos3, aff3, h, wg, wu, wd)


def _combine_kernel(tb_ref, ex_ref, ob_ref, fl_ref, pos_ref, ye_ref, x_ref, g_ref, o_ref, *, rb, tr, final):
    w = pl.program_id(0)
    flags = fl_ref[w]

    @pl.when((flags & 1) != 0)
    def _():
        o_ref[...] = x_ref[...]

    @pl.when((flags & 4) != 0)
    def _():
        rank = ob_ref[w] * rb + lax.broadcasted_iota(jnp.int32, (rb, tr), 0)
        onehot = jnp.where(pos_ref[...] == rank, 1.0, 0.0).astype(BF16)
        o_ref[...] += _dot_tn(onehot, ye_ref[...])

    if final:
        @pl.when((flags & 2) != 0)
        def _():
            x = o_ref[...]
            ms = jnp.mean(x * x, axis=-1, keepdims=True)
            o_ref[...] = x * lax.rsqrt(ms + NORM_EPS) * g_ref[...]


def combine(table, pos3, ye, x, final_g, tr, rb, final):
    n = x.shape[0]
    tb_of, ex_of, ob_of, flags = table
    return pl.pallas_call(
        functools.partial(_combine_kernel, rb=rb, tr=tr, final=final),
        grid_spec=pltpu.PrefetchScalarGridSpec(
            num_scalar_prefetch=4,
            grid=(tb_of.shape[0],),
            in_specs=[pl.BlockSpec((None, 1, tr), lambda w, tb, ex, ob, fl: (ex[w], 0, tb[w])),
                      pl.BlockSpec((None, rb, D_MODEL), lambda w, tb, ex, ob, fl: (ex[w], ob[w], 0)),
                      pl.BlockSpec((tr, D_MODEL), lambda w, tb, ex, ob, fl: (tb[w], 0)),
                      pl.BlockSpec((1, D_MODEL), lambda w, tb, ex, ob, fl: (0, 0))],
            out_specs=pl.BlockSpec((tr, D_MODEL), lambda w, tb, ex, ob, fl: (tb[w], 0)),
        ),
        out_shape=jax.ShapeDtypeStruct((n, D_MODEL), F32),
        compiler_params=_cparams(("arbitrary",)),
        name="combine",
    )(tb_of, ex_of, ob_of, flags, pos3, ye, x, final_g)


def expert_choice_ffn(x, h, aff_t, wg, wu, wd, final_g, tr, final):
    e, n = aff_t.shape
    cap = CAPACITY_FACTOR * n // e
    pos, cend = select_tokens(aff_t, cap, tr)
    pos3 = pos.reshape(e, 1, n)
    gpb = min(ROUTE_TOKENS, n) // tr
    cend = cend[:, gpb - 1::gpb]
    ye = gather_ffn(routing_tables(cend, tr, cap // tr), pos3, aff_t.reshape(e, 1, n), h,
                    wg, wu, wd, cap, tr, gpb * tr)
    rb = min(COMBINE_RANKS, tr)
    return combine(combine_table(cend, rb, cap // rb), pos3, ye, x, final_g, gpb * tr, rb, final)


def _pad_cols(w, width):
    return jnp.pad(w, ((0, 0),) * (w.ndim - 1) + ((0, width - w.shape[-1]),))


def _pad_lora_cols(z):
    o = RWKV_MAIN
    parts = [z[..., :o]]
    for width in (DECAY_LORA, DECAY_LORA, AAA_LORA, AAA_LORA, GATE_LORA):
        parts.append(_pad_cols(z[..., o:o + width], LORA_PAD))
        o += width
    return jnp.concatenate(parts, axis=-1)


def _pad_rows(w, rows):
    return jnp.pad(w, ((0, 0),) * (w.ndim - 2) + ((0, rows - w.shape[-2]), (0, 0)))


def prepare_weights(norm1_g, w_in, lambda_qk, subln_g, shift_mu, w0, w2, a0, a2, g2, k_k, k_a, r_k,
                    lnx_w, lnx_b, w_out, norm2_g, w_router, w_gate, w_up, w_down, final_g):
    scale = ATTN_QK_DIM ** -0.5 * math.log2(math.e)
    w_qv = jnp.concatenate([w_in[..., :ATTN_WIDTH] * scale, w_in[..., 2 * ATTN_WIDTH:ATTN_COLS]], axis=-1)
    w_kt = jnp.swapaxes(w_in[..., ATTN_WIDTH:2 * ATTN_WIDTH], 1, 2)
    blk = jnp.arange(RWKV_WIDTH) // RWKV_HEAD
    return dict(
        norm1_g=norm1_g[:, None, :], w_qv=w_qv.astype(BF16), w_kt=w_kt.astype(BF16),
        w_r=_pad_lora_cols(w_in[..., ATTN_COLS:]).astype(BF16), lambda_qk=lambda_qk, subln_g=subln_g[:, None, :],
        mu=_pad_lora_cols(shift_mu), w0=w0, w2=_pad_rows(w2, LORA_PAD).astype(BF16),
        a0=a0, a2=_pad_rows(a2, LORA_PAD).astype(BF16), g2=_pad_rows(g2, LORA_PAD).astype(BF16),
        k_k=k_k[:, None, :], k_a=k_a[:, None, :], r_k=r_k.reshape(DEPTH, 1, RWKV_WIDTH),
        lnx_w=lnx_w[:, None, :], lnx_b=lnx_b[:, None, :], w_out=w_out.astype(BF16),
        norm2_g=norm2_g[:, None, :], w_router_t=jnp.swapaxes(w_router, 1, 2),
        w_gate=w_gate.astype(BF16), w_up=w_up.astype(BF16), w_down=w_down.astype(BF16),
        final_g=final_g[None, :],
        seg=(blk[:, None] == blk[None, :]).astype(BF16),
        scan_consts=scan_constants(),
    )


def trunk(x, p, tiles):
    batch, seq, _ = x.shape
    n = batch * seq
    x = x.reshape(n, D_MODEL)
    attn_consts = attention_constants(tiles["tq"])
    for l in range(DEPTH):
        lambda_init = 0.8 - 0.6 * math.exp(-0.3 * l)
        qv, kt, zr = in_proj(x, p["norm1_g"][l], p["w_qv"][l], p["w_kt"][l], p["w_r"][l], tiles["tq"])
        oa = diff_attention(qv, kt, attn_consts, p["lambda_qk"][l], p["subln_g"][l], batch, seq, lambda_init,
                            tiles["tq"])
        r, v, a, lw, kd, bb, bonus, gate = rwkv_prep(
            zr, p["mu"][l], p["w0"][l], p["w2"][l], p["a0"][l], p["a2"][l], p["g2"][l],
            p["k_k"][l], p["k_a"][l], p["r_k"][l], p["seg"], seq, tiles["tm"])
        y2 = rwkv_scan(r, v, a, lw, kd, bb, p["scan_consts"], batch, seq, tiles["tb"])
        x, h, aff_t = post_mix(y2, bonus, gate, oa, x, p["lnx_w"][l], p["lnx_b"][l], p["seg"],
                               p["w_out"][l], p["norm2_g"][l], p["w_router_t"][l], tiles["tm"])
        x = expert_choice_ffn(x, h, aff_t, p["w_gate"][l], p["w_up"][l], p["w_down"][l], p["final_g"],
                              tiles["tr"], final=(l == DEPTH - 1))
    return x.reshape(batch, seq, D_MODEL)


def _tiles(seq):
    return dict(tm=min(512, seq), tq=min(512, seq), tb=min(256, seq), tr=512)


def kernel(x_prompt, x_sample, norm1_g, w_in, lambda_qk, subln_g, shift_mu, w0, w2, a0, a2, g2, k_k, k_a, r_k,
           lnx_w, lnx_b, w_out, norm2_g, w_router, w_gate, w_up, w_down, final_g):
    p = prepare_weights(norm1_g, w_in, lambda_qk, subln_g, shift_mu, w0, w2, a0, a2, g2, k_k, k_a, r_k,
                        lnx_w, lnx_b, w_out, norm2_g, w_router, w_gate, w_up, w_down, final_g)
    y_prompt = trunk(x_prompt, p, _tiles(x_prompt.shape[1]))
    y_sample = trunk(x_sample, p, _tiles(x_sample.shape[1]))
    return (y_prompt, y_sample)
```

```python
import functools
import math

import jax
import jax.numpy as jnp
from jax import lax
from jax.experimental import pallas as pl
from jax.experimental.pallas import tpu as pltpu

F32 = jnp.float32
BF16 = jnp.bfloat16
HIGHEST = lax.Precision.HIGHEST

D_MODEL = 1024
DEPTH = 4
ATTN_WIDTH = 512
ATTN_HEADS = 4
ATTN_QK_DIM = 64
ATTN_V_DIM = 128
RWKV_WIDTH = 512
RWKV_HEAD = 64
DECAY_LORA = 32
AAA_LORA = 32
GATE_LORA = 96
LORA_PAD = 128
ATTN_COLS = 3 * ATTN_WIDTH
RWKV_MAIN = 3 * RWKV_WIDTH
RWKV_COLS_PAD = RWKV_MAIN + 5 * LORA_PAD
N_EXPERTS = 16
EXPERT_FF = 2 * D_MODEL
CAPACITY_FACTOR = 2
NORM_EPS = 1e-6
GN_EPS = 64e-5

CHUNK = 64
GROUP = 4
GROUP_W = GROUP * RWKV_HEAD
N_LEVELS = CHUNK.bit_length() - 1
SCAN_BLOCKS_PER_STEP = 2
VMEM_LIMIT = 56 * 1024 * 1024


def _cparams(sem):
    return pltpu.CompilerParams(dimension_semantics=sem, vmem_limit_bytes=VMEM_LIMIT)


def _dot(a, b):
    return jnp.dot(a, b, preferred_element_type=F32)


def _dot_nt(a, b):
    return lax.dot_general(a, b, (((1,), (1,)), ((), ())), preferred_element_type=F32)


def _dot_tn(a, b):
    return lax.dot_general(a, b, (((0,), (0,)), ((), ())), preferred_element_type=F32)


def _seg_sum(x, seg):
    hi = x.astype(BF16)
    lo = (x - hi.astype(F32)).astype(BF16)
    return _dot(hi, seg) + _dot(lo, seg)


def _in_proj_kernel(x_ref, g_ref, wqv_ref, wkt_ref, wr_ref, qv_ref, kt_ref, zr_ref):
    x = x_ref[...]
    ms = jnp.mean(x * x, axis=-1, keepdims=True)
    h = (x * lax.rsqrt(ms + NORM_EPS) * g_ref[...]).astype(BF16)
    qv = _dot(h, wqv_ref[...]).astype(BF16)
    for j in range(2 * ATTN_HEADS):
        qv_ref[j] = qv[:, j * ATTN_V_DIM:(j + 1) * ATTN_V_DIM]
    kt = _dot_nt(wkt_ref[...], h).astype(BF16)
    for j in range(ATTN_HEADS):
        kt_ref[j, 0] = kt[j * ATTN_V_DIM:(j + 1) * ATTN_V_DIM, :]
    zr_ref[...] = _dot(h, wr_ref[...])


def in_proj(x, g, w_qv, w_kt, w_r, tm):
    n = x.shape[0]
    full = lambda a: pl.BlockSpec(a.shape, lambda i: (0, 0))
    return pl.pallas_call(
        _in_proj_kernel,
        grid=(n // tm,),
        in_specs=[
            pl.BlockSpec((tm, D_MODEL), lambda i: (i, 0)),
            pl.BlockSpec((1, D_MODEL), lambda i: (0, 0)),
            full(w_qv), full(w_kt), full(w_r),
        ],
        out_specs=[
            pl.BlockSpec((2 * ATTN_HEADS, tm, ATTN_V_DIM), lambda i: (0, i, 0)),
            pl.BlockSpec((ATTN_HEADS, 1, ATTN_V_DIM, tm), lambda i: (0, i, 0, 0)),
            pl.BlockSpec((tm, RWKV_COLS_PAD), lambda i: (i, 0)),
        ],
        out_shape=[
            jax.ShapeDtypeStruct((2 * ATTN_HEADS, n, ATTN_V_DIM), BF16),
            jax.ShapeDtypeStruct((ATTN_HEADS, n // tm, ATTN_V_DIM, tm), BF16),
            jax.ShapeDtypeStruct((n, RWKV_COLS_PAD), F32),
        ],
        compiler_params=_cparams(("parallel",)),
        name="in_proj",
    )(x, g, w_qv, w_kt, w_r)


ROW_BLOCK = 64
POS_BITS = 9
M_INIT = -1e30
LANES = 128
QUERY_TILES_PER_STEP = 2


def _attn_kernel(kap_ref, lq_ref, q_ref, k_ref, v_ref, g_ref, qx_ref, kx_ref, dbias_ref, o_ref,
                 q2_ref, s_ref, p_ref, m_ref, al_ref, sh_ref, acc_ref, *, seq, tq, qsub, lambda_init):
    h = pl.program_id(1)
    nk = seq // tq
    kap = kap_ref[h]
    lax.fori_loop(0, qsub, functools.partial(
        _attn_query_tile, pl.program_id(2) * qsub, kap, lq_ref, q_ref, k_ref, v_ref, g_ref, qx_ref, kx_ref,
        dbias_ref, o_ref, q2_ref, s_ref, p_ref, m_ref, al_ref, sh_ref, acc_ref, nk, tq, lambda_init), 0)


def _attn_query_tile(i0, kap, lq_ref, q_ref, k_ref, v_ref, g_ref, qx_ref, kx_ref, dbias_ref, o_ref,
                     q2_ref, s_ref, p_ref, m_ref, al_ref, sh_ref, acc_ref, nk, tq, lambda_init, sub, carry):
    i = i0 + sub
    qrows = pl.ds(pl.multiple_of(sub * tq, tq), tq)
    q = q_ref[qrows, :]
    lane = lax.broadcasted_iota(jnp.int32, q.shape, 1)
    zero = jnp.zeros_like(q)
    q2 = jnp.concatenate([jnp.where(lane < ATTN_QK_DIM, q, zero),
                          jnp.where(lane >= ATTN_QK_DIM, q, zero)], axis=0)
    qx = qx_ref[...]
    qx2 = jnp.concatenate([qx, qx], axis=0)
    q2_ref[0] = jnp.concatenate([q2, qx2], axis=1)
    q2_ref[1] = jnp.concatenate([q2, -qx2], axis=1)
    kx = kx_ref[...]
    ones = jnp.ones((tq, ATTN_V_DIM), BF16)
    m_ref[...] = jnp.full(m_ref.shape, M_INIT, F32)
    acc_ref[...] = jnp.zeros(acc_ref.shape, F32)

    def tile_of(step):
        jj = step - 1
        return jnp.where(step == 0, i, jj + (jj >= i).astype(jnp.int32))

    def scores(step, buf):
        j = tile_of(step)
        k_aug = jnp.concatenate([k_ref[j], kx], axis=0)
        s_ref[buf] = _dot(q2_ref[(j > i).astype(jnp.int32)], k_aug)

    def softmax_and_values(step, buf):
        j = tile_of(step)
        kappa = -kap * jnp.abs(i - j).astype(F32)
        start = pl.multiple_of(j * tq, tq)
        v_aug = jnp.concatenate([v_ref[pl.ds(start, tq), :], ones], axis=1)
        for rb in range(2 * tq // ROW_BLOCK):
            rows = slice(rb * ROW_BLOCK, (rb + 1) * ROW_BLOCK)
            m_old = m_ref[rows, :]
            row_max = jnp.max(s_ref[buf, rows, :], axis=-1, keepdims=True)
            m_new = jnp.maximum(m_old, jnp.broadcast_to(row_max, m_old.shape) + kappa)
            al_ref[rows, :] = jnp.exp2(m_old - m_new)
            sh_ref[rows, :] = m_new - kappa
            m_ref[rows, :] = m_new
        for rb in range(2 * tq // ROW_BLOCK):
            rows = slice(rb * ROW_BLOCK, (rb + 1) * ROW_BLOCK)
            shift = jnp.tile(sh_ref[rows, :], (1, tq // LANES))
            p_ref[rows, :] = jnp.exp2(s_ref[buf, rows, :] - shift).astype(BF16)
        acc_ref[...] = (jnp.tile(al_ref[...], (1, 2 * ATTN_V_DIM // LANES)) * acc_ref[...]
                        + _dot(p_ref[...], v_aug))

    s_diag = _dot(q2, k_ref[i])
    s_ref[0] = (s_diag.reshape(2, tq, tq) + dbias_ref[...][None]).reshape(2 * tq, tq)

    def pair(t, carry):
        scores(2 * t + 1, 1)
        softmax_and_values(2 * t, 0)
        scores(jnp.minimum(2 * t + 2, nk - 1), 0)
        softmax_and_values(2 * t + 1, 1)
        return carry

    lax.fori_loop(0, nk // 2, pair, 0)

    lq = lq_ref[...]
    lam = (jnp.exp(jnp.sum(lq[0:1] * lq[1:2], axis=-1, keepdims=True))
           - jnp.exp(jnp.sum(lq[2:3] * lq[3:4], axis=-1, keepdims=True)) + lambda_init)
    acc = acc_ref[...]
    o_all = acc[:, :ATTN_V_DIM] / acc[:, ATTN_V_DIM:]
    o = o_all[:tq] - lam * o_all[tq:]
    ms = jnp.mean(o * o, axis=-1, keepdims=True)
    o = o * lax.rsqrt(ms + NORM_EPS) * g_ref[...] * (1.0 - lambda_init)
    o_ref[qrows, :] = o.astype(BF16)
    return carry


def attention_constants(tq):
    assert tq <= (1 << POS_BITS)
    slopes = 2.0 ** (-8.0 * jnp.arange(1, ATTN_HEADS + 1, dtype=F32) / ATTN_HEADS)
    c = slopes * math.log2(math.e)
    cb = c[:, None] * (2.0 ** jnp.arange(POS_BITS, dtype=F32))[None, :]
    hi = cb.astype(BF16).astype(F32)
    lo = (cb - hi).astype(BF16).astype(F32)
    pos = jnp.arange(tq)
    bits = ((pos[:, None] >> jnp.arange(POS_BITS)[None, :]) & 1).astype(F32)
    hb = jnp.broadcast_to(bits[None], (ATTN_HEADS, tq, POS_BITS))
    const = lambda x: jnp.broadcast_to(x[:, None, :], (ATTN_HEADS, tq, POS_BITS))
    pad = jnp.zeros((ATTN_HEADS, tq, ATTN_V_DIM - 4 * POS_BITS), F32)
    qx = jnp.concatenate([const(hi), const(lo), hb, hb, pad], axis=-1).astype(BF16)
    kx = jnp.swapaxes(jnp.concatenate([hb, hb, const(-hi), const(-lo), pad], axis=-1), 1, 2).astype(BF16)
    dist = jnp.abs(pos[:, None] - pos[None, :]).astype(F32)
    dbias = -c[:, None, None] * dist[None]
    kap = c * tq
    return kap, qx, kx, dbias


def diff_attention(qv, kt, consts, lq, subln_g, batch, seq, lambda_init, tq):
    n = batch * seq
    nq = seq // tq
    kap, qx, kx, dbias = consts
    assert nq % 2 == 0, "key tiles are processed in pairs"
    qsub = QUERY_TILES_PER_STEP if nq % QUERY_TILES_PER_STEP == 0 else 1
    nq //= qsub
    kern = functools.partial(_attn_kernel, seq=seq, tq=tq, qsub=qsub, lambda_init=lambda_init)
    per_head = lambda w: pl.BlockSpec((None, tq, w), lambda b, h, i: (h, 0, 0))
    return pl.pallas_call(
        kern,
        grid=(batch, ATTN_HEADS, nq),
        in_specs=[
            pl.BlockSpec(memory_space=pltpu.SMEM),
            pl.BlockSpec((4, ATTN_QK_DIM), lambda b, h, i: (0, 0)),
            pl.BlockSpec((None, qsub * tq, ATTN_V_DIM), lambda b, h, i: (h, b * nq + i, 0)),
            pl.BlockSpec((None, seq // tq, ATTN_V_DIM, tq), lambda b, h, i: (h, b, 0, 0)),
            pl.BlockSpec((None, seq, ATTN_V_DIM), lambda b, h, i: (ATTN_HEADS + h, b, 0)),
            pl.BlockSpec((1, ATTN_V_DIM), lambda b, h, i: (0, 0)),
            per_head(ATTN_V_DIM), pl.BlockSpec((None, ATTN_V_DIM, tq), lambda b, h, i: (h, 0, 0)), per_head(tq),
        ],
        out_specs=pl.BlockSpec((None, qsub * tq, ATTN_V_DIM), lambda b, h, i: (h, b * nq + i, 0)),
        out_shape=jax.ShapeDtypeStruct((ATTN_HEADS, n, ATTN_V_DIM), BF16),
        scratch_shapes=[
            pltpu.VMEM((2, 2 * tq, 2 * ATTN_V_DIM), BF16),
            pltpu.VMEM((2, 2 * tq, tq), F32),
            pltpu.VMEM((2 * tq, tq), BF16),
            pltpu.VMEM((2 * tq, LANES), F32),
            pltpu.VMEM((2 * tq, LANES), F32),
            pltpu.VMEM((2 * tq, LANES), F32),
            pltpu.VMEM((2 * tq, 2 * ATTN_V_DIM), F32),
        ],
        compiler_params=_cparams(("parallel", "parallel", "parallel")),
        name="diff_attention",
    )(kap, lq, qv, kt, qv, subln_g, qx, kx, dbias)


def _softplus(x):
    return jnp.maximum(x, 0.0) + jnp.log(1.0 + jnp.exp(-jnp.abs(x)))


def _sigmoid(x):
    return 1.0 / (1.0 + jnp.exp(-x))


def _prep_kernel(z_ref, zp_ref, zn_ref, mu_ref, w0_ref, w2_ref, a0_ref, a2_ref, g2_ref,
                 kk_ref, ka_ref, rk_ref, seg_ref,
                 r_ref, v_ref, a_ref, lw_ref, kd_ref, bb_ref, bonus_ref, gate_ref, *, seq, tm):
    i = pl.program_id(0)
    z = z_ref[...]
    row = lax.broadcasted_iota(jnp.int32, z.shape, 0)
    first = (i * tm) % seq == 0
    last = ((i + 1) * tm) % seq == 0
    zp = jnp.where(first, 0.0, zp_ref[7:8, :])
    zn = jnp.where(last, 0.0, zn_ref[0:1, :])
    prev = jnp.where(row == 0, zp, pltpu.roll(z, 1, 0))
    nxt = jnp.where(row == tm - 1, zn, pltpu.roll(z, tm - 1, 0))
    z = z + mu_ref[0:1, :] * (prev - z) + mu_ref[1:2, :] * (nxt - z)

    w = RWKV_WIDTH
    r = z[:, 0:w]
    k = z[:, w:2 * w]
    v = z[:, 2 * w:3 * w]
    seg = seg_ref[...]
    kk = k * kk_ref[...]
    ss = _seg_sum(kk * kk, seg)
    kk = kk / jnp.maximum(jnp.sqrt(ss), 1e-12)
    r_ref[...] = r.astype(BF16)
    v_ref[...] = v.astype(BF16)
    a_ref[...] = (-kk).astype(BF16)
    for d in range(2):
        xw = z[:, RWKV_MAIN + d * LORA_PAD:RWKV_MAIN + (d + 1) * LORA_PAD]
        xa = z[:, RWKV_MAIN + (2 + d) * LORA_PAD:RWKV_MAIN + (3 + d) * LORA_PAD]
        wlog = -_softplus(-(w0_ref[d:d + 1, :] + _dot(jnp.tanh(xw).astype(BF16), w2_ref[d]))) - 0.5
        lw_ref[d] = -jnp.exp(wlog)
        iclr = _sigmoid(a0_ref[d:d + 1, :] + _dot(xa.astype(BF16), a2_ref[d]))
        kd_ref[d] = (k * (1.0 + (iclr - 1.0) * ka_ref[...])).astype(BF16)
        bb_ref[d] = (kk * iclr).astype(BF16)
    xg = z[:, RWKV_MAIN + 4 * LORA_PAD:RWKV_MAIN + 5 * LORA_PAD]
    gate_ref[...] = _dot(_sigmoid(xg).astype(BF16), g2_ref[...]).astype(BF16)
    bonus_ref[...] = (_seg_sum(r * k * rk_ref[...], seg) * v).astype(BF16)


def rwkv_prep(zr, mu, w0, w2, a0, a2, g2, k_k, k_a, r_k, seg, seq, tm):
    n, cols = zr.shape
    nb8 = n // 8
    tb8 = tm // 8
    full = lambda shape: pl.BlockSpec(shape, lambda i: (0,) * len(shape))
    tok = pl.BlockSpec((tm, RWKV_WIDTH), lambda i: (i, 0))
    tok2 = pl.BlockSpec((2, tm, RWKV_WIDTH), lambda i: (0, i, 0))
    sds = jax.ShapeDtypeStruct((n, RWKV_WIDTH), BF16)
    sds2 = jax.ShapeDtypeStruct((2, n, RWKV_WIDTH), BF16)
    sds2_f32 = jax.ShapeDtypeStruct((2, n, RWKV_WIDTH), F32)
    return pl.pallas_call(
        functools.partial(_prep_kernel, seq=seq, tm=tm),
        grid=(n // tm,),
        in_specs=[
            pl.BlockSpec((tm, cols), lambda i: (i, 0)),
            pl.BlockSpec((8, cols), lambda i: (jnp.maximum(i * tb8 - 1, 0), 0)),
            pl.BlockSpec((8, cols), lambda i: (jnp.minimum((i + 1) * tb8, nb8 - 1), 0)),
            full((2, cols)), full((2, RWKV_WIDTH)), full((2, LORA_PAD, RWKV_WIDTH)),
            full((2, RWKV_WIDTH)), full((2, LORA_PAD, RWKV_WIDTH)), full((LORA_PAD, RWKV_WIDTH)),
            full((1, RWKV_WIDTH)), full((1, RWKV_WIDTH)), full((1, RWKV_WIDTH)),
            full((RWKV_WIDTH, RWKV_WIDTH)),
        ],
        out_specs=[tok, tok, tok, tok2, tok2, tok2, tok, tok],
        out_shape=[sds, sds, sds, sds2_f32, sds2, sds2, sds, sds],
        compiler_params=_cparams(("parallel",)),
        name="rwkv_prep",
    )(zr, zr, zr, mu, w0, w2, a0, a2, g2, k_k, k_a, r_k, seg)


def _scan_kernel(r_ref, v_ref, a_ref, lw_ref, k_ref, b_ref, tri_ref, strict_ref, incl_ref, lvl_ref, eye_ref, bd_ref,
                 y_ref, h_ref, *, tb, nsub):
    d = pl.program_id(1)
    i = pl.program_id(2)
    nc = tb // CHUNK
    ngrp = RWKV_WIDTH // GROUP_W

    @pl.when(i == 0)
    def _():
        h_ref[...] = jnp.zeros(h_ref.shape, F32)

    tri = tri_ref[...]
    strict = strict_ref[...] > 0.5
    incl = incl_ref[...] > 0.5
    eye = eye_ref[...]
    bdm = bd_ref[...] > 0.5
    ones16 = jnp.ones((16, GROUP_W), BF16)
    row16 = lax.broadcasted_iota(jnp.int32, (16, GROUP_W), 0)

    def bd(x):
        return jnp.where(bdm, jnp.concatenate([x] * GROUP, axis=0), 0.0).astype(BF16)

    def split(x):
        hi = x.astype(BF16)
        return hi, (x - hi.astype(F32)).astype(BF16)

    def local_stages(base):
        slices = []
        for cc in range(nc):
            ci = cc + d * (nc - 1 - 2 * cc)
            slices.append(pl.ds(pl.multiple_of(base + ci * CHUNK, CHUNK), CHUNK))
        groups = [slice(g * GROUP_W, (g + 1) * GROUP_W) for g in range(ngrp)]
        units = [dict(sl=sl, cols=cols) for sl in slices for cols in groups]
        for c in units:
            lw = lw_ref[c["sl"], c["cols"]]
            r = r_ref[c["sl"], c["cols"]].astype(F32)
            v = v_ref[c["sl"], c["cols"]].astype(F32)
            a = a_ref[c["sl"], c["cols"]].astype(F32)
            k = k_ref[c["sl"], c["cols"]].astype(F32)
            b = b_ref[c["sl"], c["cols"]].astype(F32)
            hi, lo = split(lw)
            cs = _dot(tri, jnp.concatenate([hi, lo], axis=1))
            lcum = cs[:, :GROUP_W] + cs[:, GROUP_W:]
            ltot = jnp.where(d == 0, lcum[CHUNK - 1:CHUNK], lcum[0:1])
            at = a * jnp.exp(lcum - lw)
            rt = r * jnp.exp(lcum)
            einv = jnp.exp(-lcum)
            eend = jnp.exp(ltot - lcum)
            c["x"] = jnp.concatenate([at, rt], axis=0).astype(BF16)
            c["ybk"] = jnp.concatenate([bd(b * einv), bd(k * einv)], axis=0)
            c["bkw"] = jnp.concatenate([b * eend, k * eend], axis=0).astype(BF16)
            c["vbd"] = bd(v)
            c["v"] = v.astype(BF16)
            wtot = jnp.exp(ltot)
            whi = wtot.astype(BF16).astype(F32)
            c["whl"] = jnp.where(row16 == 0, whi, jnp.where(row16 == 1, wtot - whi, 0.0)).astype(BF16)
        for c in units:
            amat = _dot_nt(c["x"], c.pop("ybk"))
            c["aab"] = jnp.where(strict, amat[:CHUNK, :GROUP_W], 0.0)
            c["aak"] = jnp.where(strict, amat[:CHUNK, GROUP_W:], 0.0).astype(BF16)
            c["ar"] = jnp.concatenate([jnp.where(incl, amat[CHUNK:, :GROUP_W], 0.0),
                                       jnp.where(incl, amat[CHUNK:, GROUP_W:], 0.0)], axis=1).astype(BF16)
            c["p"] = eye + jnp.where(lvl_ref[0] > 0.5, c["aab"], 0.0)
        for lvl in range(1, N_LEVELS):
            for c in units:
                e = jnp.where(lvl_ref[lvl] > 0.5, c["aab"], 0.0)
                c["g"] = _dot(c["p"].astype(BF16), bd(e)).astype(BF16)
            for c in units:
                c["p"] = c["p"] + _dot(c.pop("g"), bd(c["p"]))
        for c in units:
            c["p"] = c["p"].astype(BF16)
            c["av"] = _dot(c.pop("aak"), c["vbd"])
            c["wcol"] = _dot_tn(c.pop("whl"), ones16)
        return units

    def carried_steps(units):
        for cc in range(nc):
            cu = units[cc * ngrp:(cc + 1) * ngrp]
            hs = [h_ref[g] for g in range(ngrp)]
            hx = [_dot(c["x"], h.astype(BF16)) for c, h in zip(cu, hs)]
            us = [_dot(c["p"], bd(x[:CHUNK] + c["av"])) for c, x in zip(cu, hx)]
            for g, (c, h, x, u) in enumerate(zip(cu, hs, hx, us)):
                y_ref[c["sl"], c["cols"]] = x[CHUNK:] + _dot(c["ar"], jnp.concatenate([bd(u), c["vbd"]], axis=0))
                upd = _dot_tn(c["bkw"], jnp.concatenate([u.astype(BF16), c["v"]], axis=0))
                h_ref[g] = c["wcol"] * h + jnp.where(bdm, upd, 0.0)

    def block(sub, carry):
        carried_steps(local_stages((sub + d * (nsub - 1 - 2 * sub)) * tb))
        return carry

    lax.fori_loop(0, nsub, block, 0)


def rwkv_scan(r, v, a, lw, kd, bb, consts, batch, seq, tb):
    n = batch * seq
    nsub = SCAN_BLOCKS_PER_STEP if seq % (tb * SCAN_BLOCKS_PER_STEP) == 0 else 1
    nblk = seq // (tb * nsub)
    tri, strict, incl, lvl, eye, bdm = consts

    def tmap(b, d, i):
        return (b * nblk + i + d * (nblk - 1 - 2 * i), 0)

    def tmap_d(b, d, i):
        return (d, b * nblk + i + d * (nblk - 1 - 2 * i), 0)

    tok = pl.BlockSpec((nsub * tb, RWKV_WIDTH), tmap)
    tok_d = pl.BlockSpec((None, nsub * tb, RWKV_WIDTH), tmap_d)
    per_d = lambda shape: pl.BlockSpec((None,) + shape, lambda b, d, i: (d,) + (0,) * len(shape))
    full = lambda shape: pl.BlockSpec(shape, lambda b, d, i: (0, 0))
    return pl.pallas_call(
        functools.partial(_scan_kernel, tb=tb, nsub=nsub),
        grid=(batch, 2, nblk),
        in_specs=[tok, tok, tok, tok_d, tok_d, tok_d,
                  per_d((CHUNK, CHUNK)), per_d((CHUNK, GROUP * CHUNK)), per_d((CHUNK, GROUP * CHUNK)),
                  per_d((N_LEVELS, CHUNK, GROUP * CHUNK)),
                  full((CHUNK, GROUP * CHUNK)), full((GROUP_W, GROUP_W))],
        out_specs=tok_d,
        out_shape=jax.ShapeDtypeStruct((2, n, RWKV_WIDTH), F32),
        scratch_shapes=[pltpu.VMEM((RWKV_WIDTH // GROUP_W, GROUP_W, GROUP_W), F32)],
        compiler_params=_cparams(("parallel", "parallel", "arbitrary")),
        name="rwkv_scan",
    )(r, v, a, lw, kd, bb, tri, strict, incl, lvl, eye, bdm)


def scan_constants():
    t = jnp.arange(CHUNK)
    before_eq = (t[None, :] <= t[:, None])
    tri = jnp.stack([before_eq, before_eq.T]).astype(F32)
    incl = jnp.tile(tri, (1, 1, GROUP))
    tri = tri.astype(BF16)
    eye = jnp.tile(jnp.eye(CHUNK, dtype=F32), (1, GROUP))
    strict = incl - eye[None]
    levels = []
    for lv in range(N_LEVELS):
        half = t // (1 << lv)
        fwd = ((half[:, None] // 2 == half[None, :] // 2) & (half[:, None] % 2 == 1) & (half[None, :] % 2 == 0))
        levels.append(jnp.stack([fwd, fwd.T]))
    lvl = jnp.tile(jnp.stack(levels, axis=1).astype(F32), (1, 1, 1, GROUP))
    blk = jnp.arange(GROUP_W) // RWKV_HEAD
    bdm = (blk[:, None] == blk[None, :]).astype(F32)
    return tri, strict, incl, lvl, eye, bdm


def _post_kernel(yf_ref, yb_ref, bonus_ref, gate_ref, oa_ref, x_ref, lw_ref, lb_ref, seg_ref,
                 wo_ref, g2_ref, wr_ref, xo_ref, h_ref, aff_ref):
    y = yf_ref[...] + yb_ref[...]
    seg = seg_ref[...]
    mean = _seg_sum(y, seg) * (1.0 / RWKV_HEAD)
    yc = y - mean
    var = _seg_sum(yc * yc, seg) * (1.0 / RWKV_HEAD)
    yn = yc * lax.rsqrt(var + GN_EPS) * lw_ref[...] + lb_ref[...]
    orw = ((yn + bonus_ref[...]) * gate_ref[...]).astype(BF16)
    oa = jnp.concatenate([oa_ref[j] for j in range(ATTN_HEADS)], axis=1)
    x = (x_ref[...] + _dot(oa, wo_ref[:ATTN_WIDTH, :]) + _dot(orw, wo_ref[ATTN_WIDTH:, :]))
    xo_ref[...] = x
    ms = jnp.mean(x * x, axis=-1, keepdims=True)
    h = x * lax.rsqrt(ms + NORM_EPS) * g2_ref[...]
    h_ref[...] = h.astype(BF16)
    logits = lax.dot_general(wr_ref[...], h, (((1,), (1,)), ((), ())),
                             preferred_element_type=F32, precision=HIGHEST)
    logits = logits - jnp.max(logits, axis=0, keepdims=True)
    e = jnp.exp(logits)
    aff_ref[...] = e / jnp.sum(e, axis=0, keepdims=True)


def post_mix(y2, bonus, gate, oa, x, lnx_w, lnx_b, seg, w_out, norm2_g, w_router_t, tm):
    n = x.shape[0]
    full = lambda shape: pl.BlockSpec(shape, lambda i: (0,) * len(shape))
    tok = lambda w: pl.BlockSpec((tm, w), lambda i: (i, 0))
    return pl.pallas_call(
        _post_kernel,
        grid=(n // tm,),
        in_specs=[
            pl.BlockSpec((None, tm, RWKV_WIDTH), lambda i: (0, i, 0)),
            pl.BlockSpec((None, tm, RWKV_WIDTH), lambda i: (1, i, 0)),
            tok(RWKV_WIDTH), tok(RWKV_WIDTH),
            pl.BlockSpec((ATTN_HEADS, tm, ATTN_V_DIM), lambda i: (0, i, 0)), tok(D_MODEL),
            full((1, RWKV_WIDTH)), full((1, RWKV_WIDTH)), full((RWKV_WIDTH, RWKV_WIDTH)),
            full((D_MODEL, D_MODEL)), full((1, D_MODEL)), full((N_EXPERTS, D_MODEL)),
        ],
        out_specs=[tok(D_MODEL), tok(D_MODEL), pl.BlockSpec((N_EXPERTS, tm), lambda i: (0, i))],
        out_shape=[
            jax.ShapeDtypeStruct((n, D_MODEL), F32),
            jax.ShapeDtypeStruct((n, D_MODEL), BF16),
            jax.ShapeDtypeStruct((N_EXPERTS, n), F32),
        ],
        compiler_params=_cparams(("parallel",)),
        name="post_mix",
    )(y2, y2, bonus, gate, oa, x, lnx_w, lnx_b, seg, w_out, norm2_g, w_router_t)


COMBINE_RANKS = 256
ROUTE_TOKENS = 1024


def _thresh_kernel(aff_ref, thr_ref, need_ref, *, cap):
    bits = pltpu.bitcast(aff_ref[...], jnp.int32)
    e = bits.shape[0]

    def body(it, thr):
        cand = thr | jnp.left_shift(jnp.int32(1), 30 - it)
        cnt = jnp.sum(jnp.where(bits >= cand, 1.0, 0.0), axis=1, keepdims=True)
        return jnp.where(cnt >= cap, cand, thr)

    thr = lax.fori_loop(0, 31, body, jnp.zeros((e, 1), jnp.int32))
    n_gt = jnp.sum(jnp.where(bits > thr, 1.0, 0.0), axis=1, keepdims=True)
    thr_ref[...] = jnp.broadcast_to(thr, thr_ref.shape)
    need_ref[...] = jnp.broadcast_to(cap - n_gt, need_ref.shape)


def _prefix_kernel(aff_ref, thr_ref, need_ref, u_ref, pos_ref, cend_ref, carry_ref):
    @pl.when(pl.program_id(0) == 0)
    def _():
        carry_ref[...] = jnp.zeros(carry_ref.shape, F32)

    bits = pltpu.bitcast(aff_ref[...], jnp.int32)
    e, tr = bits.shape
    thr = thr_ref[:, 0:1]
    need = need_ref[:, 0:1]
    gt = bits > thr
    eq = bits == thr
    marks = jnp.concatenate([jnp.where(gt, 1.0, 0.0), jnp.where(eq, 1.0, 0.0)], axis=0).astype(BF16)
    pre = _dot(marks, u_ref[...]) + carry_ref[...]
    pgt = pre[:e]
    peq = pre[e:]
    sel = jnp.where(gt, 1.0, jnp.where(eq, jnp.where(peq <= need, 1.0, 0.0), 0.0))
    rank = pgt + jnp.minimum(peq, need) - 1.0
    pos_ref[...] = jnp.where(sel > 0.5, rank, -1.0).astype(jnp.int32)
    carry_ref[...] = pre[:, tr - 1:tr]
    cend_ref[...] = jnp.broadcast_to(rank[:, tr - 1:tr] + 1.0, cend_ref.shape)


def select_tokens(aff_t, cap, tr):
    e, n = aff_t.shape
    nb = n // tr
    thr, need = pl.pallas_call(
        functools.partial(_thresh_kernel, cap=float(cap)),
        out_shape=[jax.ShapeDtypeStruct((e, 128), jnp.int32), jax.ShapeDtypeStruct((e, 128), F32)],
        compiler_params=pltpu.CompilerParams(vmem_limit_bytes=VMEM_LIMIT),
        name="route_threshold",
    )(aff_t)
    upper = (jnp.arange(tr)[:, None] <= jnp.arange(tr)[None, :]).astype(BF16)
    pos, cend = pl.pallas_call(
        _prefix_kernel,
        grid=(nb,),
        in_specs=[pl.BlockSpec((e, tr), lambda i: (0, i)),
                  pl.BlockSpec((e, 128), lambda i: (0, 0)),
                  pl.BlockSpec((e, 128), lambda i: (0, 0)),
                  pl.BlockSpec((tr, tr), lambda i: (0, 0))],
        out_specs=[pl.BlockSpec((e, tr), lambda i: (0, i)),
                   pl.BlockSpec((None, e, 128), lambda i: (i, 0, 0))],
        out_shape=[jax.ShapeDtypeStruct((e, n), jnp.int32), jax.ShapeDtypeStruct((nb, e, 128), F32)],
        scratch_shapes=[pltpu.VMEM((2 * e, 1), F32)],
        compiler_params=_cparams(("arbitrary",)),
        name="route_prefix",
    )(aff_t, thr, need, upper)
    return pos, cend[:, :, 0].T.astype(jnp.int32)


def _pick(onehot, values):
    return jnp.sum(jnp.where(onehot, values[None, :], 0), axis=1)


def gather_table(cend, rb, n_rb):
    nb = cend.shape[0]
    cstart = jnp.concatenate([jnp.zeros((1,), jnp.int32), cend[:-1]])
    count = cend - cstart
    ob_first = jnp.minimum(cstart // rb, n_rb - 1)
    ob_last = jnp.where(count > 0, (cend - 1) // rb, ob_first)
    nitems = jnp.where(count > 0, ob_last - ob_first + 1, 0)
    cum = jnp.cumsum(nitems)
    start = cum - nitems
    w = jnp.arange(nb + n_rb, dtype=jnp.int32)
    owner = (w[:, None] >= start[None, :]) & (w[:, None] < cum[None, :])
    valid = w < cum[-1]
    tb = jnp.where(valid, _pick(owner, jnp.arange(nb, dtype=jnp.int32)), nb - 1)
    ob = jnp.where(valid, _pick(owner, ob_first - start) + w, n_rb - 1)
    prev_ob = jnp.concatenate([jnp.full((1,), -1, jnp.int32), ob[:-1]])
    next_ob = jnp.concatenate([ob[1:], jnp.full((1,), -1, jnp.int32)])
    next_valid = jnp.concatenate([valid[1:], jnp.zeros((1,), bool)])
    first = valid & (ob != prev_ob)
    last = valid & ((ob != next_ob) | ~next_valid)
    flags = first.astype(jnp.int32) + 2 * last.astype(jnp.int32) + 4 * valid.astype(jnp.int32)
    return ob, tb, flags


def routing_tables(cend, rb, n_rb):
    ob, tb, flags = jax.vmap(lambda c: gather_table(c, rb, n_rb))(cend)
    return ob.reshape(-1), tb.reshape(-1), flags.reshape(-1)


def combine_table(cend, rb, n_rb):
    e, nb = cend.shape
    cstart = jnp.concatenate([jnp.zeros((e, 1), jnp.int32), cend[:, :-1]], axis=1)
    ob_first = jnp.minimum(cstart // rb, n_rb - 1)
    ob_last = jnp.where(cend > cstart, (cend - 1) // rb, ob_first)
    obf = ob_first.T.reshape(-1)
    nitems = (ob_last - ob_first + 1).T.reshape(-1)
    cum = jnp.cumsum(nitems)
    start = cum - nitems
    length = nb * e + e * n_rb
    w = jnp.arange(length, dtype=jnp.int32)
    owner = (w[:, None] >= start[None, :]) & (w[:, None] < cum[None, :])
    valid = w < cum[-1]
    pair = jnp.where(valid, _pick(owner, jnp.arange(nb * e, dtype=jnp.int32)), nb * e - 1)
    ob = jnp.where(valid, _pick(owner, obf - start) + w, obf[-1] + nitems[-1] - 1)
    ex = pair % e
    first = valid & (_pick(owner, start) == w) & (ex == 0)
    last = valid & (_pick(owner, cum) - 1 == w) & (ex == e - 1)
    flags = first.astype(jnp.int32) + 2 * last.astype(jnp.int32) + 4 * valid.astype(jnp.int32)
    return pair // e, ex, ob, flags


def _gather_ffn_kernel(ob_ref, tb_ref, fl_ref, pos_ref, aff_ref, h_ref, wg_ref, wu_ref, wd_ref, o_ref,
                       x_acc, g_acc, *, length, rb, tg):
    idx = pl.program_id(0) * length + pl.program_id(1)
    flags = fl_ref[idx]
    ob = ob_ref[idx]

    @pl.when((flags & 1) != 0)
    def _():
        x_acc[...] = jnp.zeros(x_acc.shape, F32)
        g_acc[...] = jnp.zeros(g_acc.shape, F32)

    @pl.when((flags & 4) != 0)
    def _():
        rank = ob * rb + lax.broadcasted_iota(jnp.int32, (rb, tg), 0)
        match = pos_ref[...] == rank
        x_acc[...] += _dot(jnp.where(match, 1.0, 0.0).astype(BF16), h_ref[...])
        g_acc[...] += jnp.sum(jnp.where(match, aff_ref[...], 0.0), axis=1, keepdims=True)

    @pl.when((flags & 2) != 0)
    def _():
        x = x_acc[...].astype(BF16)
        hg = _dot(x, wg_ref[...])
        hu = _dot(x, wu_ref[...])
        act = (hg * _sigmoid(hg) * hu).astype(BF16)
        o_ref[...] = (_dot(act, wd_ref[...]) * g_acc[...]).astype(BF16)


def gather_ffn(tables, pos3, aff3, h, wg, wu, wd, cap, rb, tg):
    e = pos3.shape[0]
    n = h.shape[0]
    length = n // tg + cap // rb
    ob_of, tb_of, flags = tables
    tok = pl.BlockSpec((None, 1, tg), lambda e, w, ob, tb, fl: (e, 0, tb[e * length + w]))
    wspec = lambda r, c: pl.BlockSpec((None, r, c), lambda e, w, ob, tb, fl: (e, 0, 0))
    return pl.pallas_call(
        functools.partial(_gather_ffn_kernel, length=length, rb=rb, tg=tg),
        grid_spec=pltpu.PrefetchScalarGridSpec(
            num_scalar_prefetch=3,
            grid=(e, length),
            in_specs=[tok, tok,
                      pl.BlockSpec((tg, D_MODEL), lambda e, w, ob, tb, fl: (tb[e * length + w], 0)),
                      wspec(D_MODEL, EXPERT_FF), wspec(D_MODEL, EXPERT_FF), wspec(EXPERT_FF, D_MODEL)],
            out_specs=pl.BlockSpec((None, rb, D_MODEL), lambda e, w, ob, tb, fl: (e, ob[e * length + w], 0)),
            scratch_shapes=[pltpu.VMEM((rb, D_MODEL), F32), pltpu.VMEM((rb, 1), F32)],
        ),
        out_shape=jax.ShapeDtypeStruct((e, cap, D_MODEL), BF16),
        compiler_params=_cparams(("parallel", "arbitrary")),
        name="gather_ffn",
    )(ob_of, tb_of, flags, pos3, aff3, h, wg, wu, wd)


def _combine_kernel(tb_ref, ex_ref, ob_ref, fl_ref, pos_ref, ye_ref, x_ref, g_ref, o_ref, *, rb, tr, final):
    w = pl.program_id(0)
    flags = fl_ref[w]

    @pl.when((flags & 1) != 0)
    def _():
        o_ref[...] = x_ref[...]

    @pl.when((flags & 4) != 0)
    def _():
        rank = ob_ref[w] * rb + lax.broadcasted_iota(jnp.int32, (rb, tr), 0)
        onehot = jnp.where(pos_ref[...] == rank, 1.0, 0.0).astype(BF16)
        o_ref[...] += _dot_tn(onehot, ye_ref[...])

    if final:
        @pl.when((flags & 2) != 0)
        def _():
            x = o_ref[...]
            ms = jnp.mean(x * x, axis=-1, keepdims=True)
            o_ref[...] = x * lax.rsqrt(ms + NORM_EPS) * g_ref[...]


def combine(table, pos3, ye, x, final_g, tr, rb, final):
    n = x.shape[0]
    tb_of, ex_of, ob_of, flags = table
    return pl.pallas_call(
        functools.partial(_combine_kernel, rb=rb, tr=tr, final=final),
        grid_spec=pltpu.PrefetchScalarGridSpec(
            num_scalar_prefetch=4,
            grid=(tb_of.shape[0],),
            in_specs=[pl.BlockSpec((None, 1, tr), lambda w, tb, ex, ob, fl: (ex[w], 0, tb[w])),
                      pl.BlockSpec((None, rb, D_MODEL), lambda w, tb, ex, ob, fl: (ex[w], ob[w], 0)),
                      pl.BlockSpec((tr, D_MODEL), lambda w, tb, ex, ob, fl: (tb[w], 0)),
                      pl.BlockSpec((1, D_MODEL), lambda w, tb, ex, ob, fl: (0, 0))],
            out_specs=pl.BlockSpec((tr, D_MODEL), lambda w, tb, ex, ob, fl: (tb[w], 0)),
        ),
        out_shape=jax.ShapeDtypeStruct((n, D_MODEL), F32),
        compiler_params=_cparams(("arbitrary",)),
        name="combine",
    )(tb_of, ex_of, ob_of, flags, pos3, ye, x, final_g)


def expert_choice_ffn(x, h, aff_t, wg, wu, wd, final_g, tr, final):
    e, n = aff_t.shape
    cap = CAPACITY_FACTOR * n // e
    pos, cend = select_tokens(aff_t, cap, tr)
    pos3 = pos.reshape(e, 1, n)
    gpb = min(ROUTE_TOKENS, n) // tr
    cend = cend[:, gpb - 1::gpb]
    ye = gather_ffn(routing_tables(cend, tr, cap // tr), pos3, aff_t.reshape(e, 1, n), h,
                    wg, wu, wd, cap, tr, gpb * tr)
    rb = min(COMBINE_RANKS, tr)
    return combine(combine_table(cend, rb, cap // rb), pos3, ye, x, final_g, gpb * tr, rb, final)


def _pad_cols(w, width):
    return jnp.pad(w, ((0, 0),) * (w.ndim - 1) + ((0, width - w.shape[-1]),))


def _pad_lora_cols(z):
    o = RWKV_MAIN
    parts = [z[..., :o]]
    for width in (DECAY_LORA, DECAY_LORA, AAA_LORA, AAA_LORA, GATE_LORA):
        parts.append(_pad_cols(z[..., o:o + width], LORA_PAD))
        o += width
    return jnp.concatenate(parts, axis=-1)


def _pad_rows(w, rows):
    return jnp.pad(w, ((0, 0),) * (w.ndim - 2) + ((0, rows - w.shape[-2]), (0, 0)))


def prepare_weights(norm1_g, w_in, lambda_qk, subln_g, shift_mu, w0, w2, a0, a2, g2, k_k, k_a, r_k,
                    lnx_w, lnx_b, w_out, norm2_g, w_router, w_gate, w_up, w_down, final_g):
    scale = ATTN_QK_DIM ** -0.5 * math.log2(math.e)
    w_qv = jnp.concatenate([w_in[..., :ATTN_WIDTH] * scale, w_in[..., 2 * ATTN_WIDTH:ATTN_COLS]], axis=-1)
    w_kt = jnp.swapaxes(w_in[..., ATTN_WIDTH:2 * ATTN_WIDTH], 1, 2)
    blk = jnp.arange(RWKV_WIDTH) // RWKV_HEAD
    return dict(
        norm1_g=norm1_g[:, None, :], w_qv=w_qv.astype(BF16), w_kt=w_kt.astype(BF16),
        w_r=_pad_lora_cols(w_in[..., ATTN_COLS:]).astype(BF16), lambda_qk=lambda_qk, subln_g=subln_g[:, None, :],
        mu=_pad_lora_cols(shift_mu), w0=w0, w2=_pad_rows(w2, LORA_PAD).astype(BF16),
        a0=a0, a2=_pad_rows(a2, LORA_PAD).astype(BF16), g2=_pad_rows(g2, LORA_PAD).astype(BF16),
        k_k=k_k[:, None, :], k_a=k_a[:, None, :], r_k=r_k.reshape(DEPTH, 1, RWKV_WIDTH),
        lnx_w=lnx_w[:, None, :], lnx_b=lnx_b[:, None, :], w_out=w_out.astype(BF16),
        norm2_g=norm2_g[:, None, :], w_router_t=jnp.swapaxes(w_router, 1, 2),
        w_gate=w_gate.astype(BF16), w_up=w_up.astype(BF16), w_down=w_down.astype(BF16),
        final_g=final_g[None, :],
        seg=(blk[:, None] == blk[None, :]).astype(BF16),
        scan_consts=scan_constants(),
    )


def trunk(x, p, tiles):
    batch, seq, _ = x.shape
    n = batch * seq
    x = x.reshape(n, D_MODEL)
    attn_consts = attention_constants(tiles["tq"])
    for l in range(DEPTH):
        lambda_init = 0.8 - 0.6 * math.exp(-0.3 * l)
        qv, kt, zr = in_proj(x, p["norm1_g"][l], p["w_qv"][l], p["w_kt"][l], p["w_r"][l], tiles["tq"])
        oa = diff_attention(qv, kt, attn_consts, p["lambda_qk"][l], p["subln_g"][l], batch, seq, lambda_init,
                            tiles["tq"])
        r, v, a, lw, kd, bb, bonus, gate = rwkv_prep(
            zr, p["mu"][l], p["w0"][l], p["w2"][l], p["a0"][l], p["a2"][l], p["g2"][l],
            p["k_k"][l], p["k_a"][l], p["r_k"][l], p["seg"], seq, tiles["tm"])
        y2 = rwkv_scan(r, v, a, lw, kd, bb, p["scan_consts"], batch, seq, tiles["tb"])
        x, h, aff_t = post_mix(y2, bonus, gate, oa, x, p["lnx_w"][l], p["lnx_b"][l], p["seg"],
                               p["w_out"][l], p["norm2_g"][l], p["w_router_t"][l], tiles["tm"])
        x = expert_choice_ffn(x, h, aff_t, p["w_gate"][l], p["w_up"][l], p["w_down"][l], p["final_g"],
                              tiles["tr"], final=(l == DEPTH - 1))
    return x.reshape(batch, seq, D_MODEL)


def _tiles(seq):
    return dict(tm=min(512, seq), tq=min(512, seq), tb=min(256, seq), tr=512)


def kernel(x_prompt, x_sample, norm1_g, w_in, lambda_qk, subln_g, shift_mu, w0, w2, a0, a2, g2, k_k, k_a, r_k,
           lnx_w, lnx_b, w_out, norm2_g, w_router, w_gate, w_up, w_down, final_g):
    p = prepare_weights(norm1_g, w_in, lambda_qk, subln_g, shift_mu, w0, w2, a0, a2, g2, k_k, k_a, r_k,
                        lnx_w, lnx_b, w_out, norm2_g, w_router, w_gate, w_up, w_down, final_g)
    y_prompt = trunk(x_prompt, p, _tiles(x_prompt.shape[1]))
    y_sample = trunk(x_sample, p, _tiles(x_sample.shape[1]))
    return (y_prompt, y_sample)
```
